```python
import jax
import jax.numpy as jnp
from jax import lax
import numpy as np

D_MODEL = 2048
BATCH = 2
SEQ = 4096
DEPTH = 4
DEC_BATCH = 32
DEC_SEQ = 1
PAST_LEN = 16384
PAGE_SIZE = 128

HEAD_DIM = 64
N_HEADS_A = D_MODEL // 256
N_HEADS_B = D_MODEL // 128
N_KV_B = N_HEADS_B // 8
Q_PER_KV_B = N_HEADS_B // N_KV_B
D_CONV = D_MODEL // 4
D_MIX = (N_HEADS_A + N_HEADS_B) * HEAD_DIM + D_CONV
DILATED_GROUPS = ((128, 1), (512, 4), (2048, 16))
WIN_A_MAX = 2048
WIN_B = 128
BLOCK = 128
CONV_WIDTH = 31
ROT_DIM = HEAD_DIM // 4
ROPE_THETA = 500000.0
D_FF = 128 * ((8 * D_MODEL // 3 + 127) // 128)
N_SUB = 3
RMS_EPS = 1e-6
LN_EPS = 1e-5
NEG_INF = -1e30
ATTN_SCALE = HEAD_DIM ** -0.5
PROJ_SIZES = (N_HEADS_A * HEAD_DIM, N_HEADS_A * HEAD_DIM, N_HEADS_A * HEAD_DIM,
              N_HEADS_B * HEAD_DIM, N_KV_B * HEAD_DIM, N_KV_B * HEAD_DIM, 2 * D_CONV)
N_IN = sum(PROJ_SIZES)

kernel_name = 'hybrid_dilated_swa_conformer_step'


def rms_norm(x, g):
    x32 = x.astype(jnp.float32)
    y = x32 * lax.rsqrt(jnp.mean(x32 * x32, axis=-1, keepdims=True) + RMS_EPS)
    return y.astype(x.dtype) * g


def layer_norm(x, g, b):
    x32 = x.astype(jnp.float32)
    xc = x32 - jnp.mean(x32, axis=-1, keepdims=True)
    y = xc * lax.rsqrt(jnp.mean(xc * xc, axis=-1, keepdims=True) + LN_EPS)
    return y.astype(x.dtype) * g + b


def rope_partial(x, pos):
    half = ROT_DIM // 2
    inv_freq = 1.0 / (ROPE_THETA ** (jnp.arange(half, dtype=jnp.float32) / half))
    ang = pos.astype(jnp.float32)[:, None] * inv_freq[None, :]
    cos = jnp.cos(ang)[None, :, None, :].astype(x.dtype)
    sin = jnp.sin(ang)[None, :, None, :].astype(x.dtype)
    x1 = x[..., :half]
    x2 = x[..., half:ROT_DIM]
    return jnp.concatenate([x1 * cos - x2 * sin, x2 * cos + x1 * sin, x[..., ROT_DIM:]], axis=-1)


def swiglu(h, w_gu, w_down):
    gate, up = jnp.split(h @ w_gu, 2, axis=-1)
    return (jax.nn.silu(gate) * up) @ w_down


def split_projection(u, pos):
    b, t, _ = u.shape
    offs = [int(o) for o in np.cumsum(PROJ_SIZES)[:-1]]
    qa, ka, va, qb, kb, vb, glu_in = jnp.split(u, offs, axis=-1)
    qa = rope_partial(qa.reshape(b, t, N_HEADS_A, HEAD_DIM), pos)[:, :, :, None, :]
    ka = rope_partial(ka.reshape(b, t, N_HEADS_A, HEAD_DIM), pos)
    va = va.reshape(b, t, N_HEADS_A, HEAD_DIM)
    qb = rope_partial(qb.reshape(b, t, N_HEADS_B, HEAD_DIM), pos).reshape(b, t, N_KV_B, Q_PER_KV_B, HEAD_DIM)
    kb = rope_partial(kb.reshape(b, t, N_KV_B, HEAD_DIM), pos)
    vb = vb.reshape(b, t, N_KV_B, HEAD_DIM)
    val, gate = jnp.split(glu_in, 2, axis=-1)
    glu = val * jax.nn.sigmoid(gate)
    return qa, ka, va, qb, kb, vb, glu


def banded_window_attention(q, k, v, window):
    b, l, h, g, e = q.shape
    nb = -(-l // BLOCK)
    pad = nb * BLOCK - l
    qp = jnp.pad(q, ((0, 0), (0, pad), (0, 0), (0, 0), (0, 0))).reshape(b, nb, BLOCK, h, g, e)
    kp = jnp.pad(k, ((0, 0), (BLOCK, pad), (0, 0), (0, 0))).reshape(b, nb + 1, BLOCK, h, e)
    vp = jnp.pad(v, ((0, 0), (BLOCK, pad), (0, 0), (0, 0))).reshape(b, nb + 1, BLOCK, h, e)
    kb = jnp.concatenate([kp[:, :-1], kp[:, 1:]], axis=2)
    vb = jnp.concatenate([vp[:, :-1], vp[:, 1:]], axis=2)
    s = jnp.einsum('bnqhge,bnkhe->bnhgqk', qp, kb).astype(jnp.float32) * ATTN_SCALE
    qi = jnp.arange(BLOCK)[:, None]
    kj = jnp.arange(2 * BLOCK)[None, :]
    dist = qi + BLOCK - kj
    key_pos = jnp.arange(nb)[:, None, None] * BLOCK + kj[None] - BLOCK
    mask = (dist >= 0)[None] & (dist <= window)[None] & (key_pos >= 0)
    s = jnp.where(mask[None, :, None, None], s, NEG_INF)
    lse = jax.nn.logsumexp(s, axis=-1)
    p = jnp.exp(s - lse[..., None]).astype(v.dtype)
    o = jnp.einsum('bnhgqk,bnkhe->bnqhge', p, vb).reshape(b, nb * BLOCK, h, g, e)[:, :l]
    lse = lse.transpose(0, 1, 4, 2, 3).reshape(b, nb * BLOCK, h, g)[:, :l]
    return o, lse


def dilated_window_attention_prompt(q, k, v, window, dilation):
    b, l = q.shape[0], q.shape[1]
    m = l // dilation

    def to_residue(z):
        z = z.reshape((b, m, dilation) + z.shape[2:])
        z = jnp.moveaxis(z, 2, 1)
        return z.reshape((b * dilation, m) + z.shape[3:])

    def from_residue(z):
        z = z.reshape((b, dilation, m) + z.shape[2:])
        z = jnp.moveaxis(z, 1, 2)
        return z.reshape((b, l) + z.shape[3:])

    o, lse = banded_window_attention(to_residue(q), to_residue(k), to_residue(v), window // dilation)
    return from_residue(o), from_residue(lse)


def gathered_window_attention(q, ext_k, ext_v, n_past, window, dilation):
    t = q.shape[1]
    j = jnp.arange(window // dilation + 1)
    idx = n_past + jnp.arange(t)[:, None] - dilation * j[None, :]
    valid = idx >= 0
    idx = jnp.maximum(idx, 0)
    kg = ext_k[:, idx]
    vg = ext_v[:, idx]
    s = jnp.einsum('bthge,btjhe->bthgj', q, kg).astype(jnp.float32) * ATTN_SCALE
    s = jnp.where(valid[None, :, None, None, :], s, NEG_INF)
    lse = jax.nn.logsumexp(s, axis=-1)
    p = jnp.exp(s - lse[..., None]).astype(ext_v.dtype)
    o = jnp.einsum('bthgj,btjhe->bthge', p, vg)
    return o, lse


def merge_by_denominator(outs, lses):
    w = jax.nn.softmax(jnp.stack(lses, axis=0), axis=0)
    return jnp.sum(w[..., None].astype(outs[0].dtype) * jnp.stack(outs, axis=0), axis=0)


def apply_sinks(o, lse, sinks):
    sink = sinks.astype(jnp.float32).reshape(N_KV_B, Q_PER_KV_B)
    total = jnp.logaddexp(lse, sink)
    return o * jnp.exp(lse - total)[..., None].astype(o.dtype)


def conv_module_tail(ext, p):
    y = lax.conv_general_dilated(ext, p['conv_w'][:, None, :], (1,), 'VALID',
                                 dimension_numbers=('NWC', 'WIO', 'NWC'),
                                 feature_group_count=D_CONV) + p['conv_b']
    return jax.nn.silu(layer_norm(y, p['conv_ln_g'], p['conv_ln_b']))


def mixer_prompt(u, pos, p):
    b, t, _ = u.shape
    qa, ka, va, qb, kb, vb, glu = split_projection(u, pos)
    outs, lses = [], []
    for window, dilation in DILATED_GROUPS:
        o_g, lse_g = dilated_window_attention_prompt(qa, ka, va, window, dilation)
        outs.append(o_g)
        lses.append(lse_g)
    oa = merge_by_denominator(outs, lses)
    ob, lse_b = banded_window_attention(qb, kb, vb, WIN_B)
    ob = apply_sinks(ob, lse_b, p['sinks'])
    ext_c = jnp.pad(glu, ((0, 0), (CONV_WIDTH - 1, 0), (0, 0)))
    oc = conv_module_tail(ext_c, p)
    mix = jnp.concatenate([oa.reshape(b, t, -1), ob.reshape(b, t, -1), oc], axis=-1)
    len_a = min(WIN_A_MAX, t)
    len_b = min(WIN_B, t)
    new_state = (ka[:, t - len_a:], va[:, t - len_a:], kb[:, t - len_b:], vb[:, t - len_b:], ext_c[:, t:])
    return mix, new_state


def mixer_sample(u, pos, cache_ak, cache_av, cache_bk, cache_bv, state_c, p):
    b, t, _ = u.shape
    qa, ka, va, qb, kb, vb, glu = split_projection(u, pos)
    len_a = cache_ak.shape[1]
    len_b = cache_bk.shape[1]
    ext_ak = jnp.concatenate([cache_ak, ka], axis=1)
    ext_av = jnp.concatenate([cache_av, va], axis=1)
    ext_bk = jnp.concatenate([cache_bk, kb], axis=1)
    ext_bv = jnp.concatenate([cache_bv, vb], axis=1)
    outs, lses = [], []
    for window, dilation in DILATED_GROUPS:
        o_g, lse_g = gathered_window_attention(qa, ext_ak, ext_av, len_a, window, dilation)
        outs.append(o_g)
        lses.append(lse_g)
    oa = merge_by_denominator(outs, lses)
    ob, lse_b = gathered_window_attention(qb, ext_bk, ext_bv, len_b, WIN_B, 1)
    ob = apply_sinks(ob, lse_b, p['sinks'])
    ext_c = jnp.concatenate([state_c, glu], axis=1)
    oc = conv_module_tail(ext_c, p)
    mix = jnp.concatenate([oa.reshape(b, t, -1), ob.reshape(b, t, -1), oc], axis=-1)
    new_state = (ext_ak[:, t:], ext_av[:, t:], ext_bk[:, t:], ext_bv[:, t:], ext_c[:, t:])
    return mix, new_state


def apply_layer(x, c, p, mixer, mixer_args):
    b, d = c.shape
    mod = (jax.nn.silu(c) @ p['w_ada'] + p['b_ada']).reshape(b, 1, 3 * N_SUB, d)

    def modulated_pre(h, s):
        return rms_norm(h, p['g_pre'][s]) * (1.0 + mod[:, :, 3 * s + 1]) + mod[:, :, 3 * s]

    def gated_post(y, s, weight):
        return weight * mod[:, :, 3 * s + 2] * rms_norm(y, p['g_post'][s])

    x = x + gated_post(swiglu(modulated_pre(x, 0), p['w_gu'][0], p['w_down'][0]), 0, 0.5)
    mix, new_state = mixer(modulated_pre(x, 1) @ p['w_in'], *mixer_args)
    x = x + gated_post(mix @ p['w_out'], 1, 1.0)
    x = x + gated_post(swiglu(modulated_pre(x, 2), p['w_gu'][1], p['w_down'][1]), 2, 0.5)
    return x, new_state


def _normal(key, shape, scale):
    return jax.random.normal(key, shape, jnp.float32) * scale


def setup_inputs(seed: int = 0) -> dict:
    key = jax.random.key(seed)
    ks = jax.random.split(key, 24)
    len_a = min(WIN_A_MAX, PAST_LEN)
    len_b = min(WIN_B, PAST_LEN)
    return {
        'x_prompt': _normal(ks[0], (BATCH, SEQ, D_MODEL), 1.0),
        'x_sample': _normal(ks[1], (DEC_BATCH, DEC_SEQ, D_MODEL), 1.0),
        'c_prompt': _normal(ks[2], (BATCH, D_MODEL), 1.0),
        'c_sample': _normal(ks[3], (DEC_BATCH, D_MODEL), 1.0),
        'cache_a_k': _normal(ks[4], (DEPTH, DEC_BATCH, len_a, N_HEADS_A, HEAD_DIM), 1.0),
        'cache_a_v': _normal(ks[5], (DEPTH, DEC_BATCH, len_a, N_HEADS_A, HEAD_DIM), 1.0),
        'cache_b_k': _normal(ks[6], (DEPTH, DEC_BATCH, len_b, N_KV_B, HEAD_DIM), 1.0),
        'cache_b_v': _normal(ks[7], (DEPTH, DEC_BATCH, len_b, N_KV_B, HEAD_DIM), 1.0),
        'state_c_conv': _normal(ks[8], (DEPTH, DEC_BATCH, CONV_WIDTH - 1, D_CONV), 0.5),
        'w_ada': _normal(ks[9], (DEPTH, D_MODEL, 3 * N_SUB * D_MODEL), 0.5 * D_MODEL ** -0.5),
        'b_ada': _normal(ks[10], (DEPTH, 3 * N_SUB * D_MODEL), 0.01),
        'g_pre': 1.0 + _normal(ks[11], (DEPTH, N_SUB, D_MODEL), 0.01),
        'g_post': 1.0 + _normal(ks[12], (DEPTH, N_SUB, D_MODEL), 0.01),
        'w_ffn_gu': _normal(ks[13], (DEPTH, 2, D_MODEL, 2 * D_FF), D_MODEL ** -0.5),
        'w_ffn_down': _normal(ks[14], (DEPTH, 2, D_FF, D_MODEL), D_FF ** -0.5),
        'w_in': _normal(ks[15], (DEPTH, D_MODEL, N_IN), D_MODEL ** -0.5),
        'w_out': _normal(ks[16], (DEPTH, D_MIX, D_MODEL), D_MIX ** -0.5),
        'attn_sinks': _normal(ks[17], (DEPTH, N_HEADS_B), 1.0),
        'conv_w': _normal(ks[18], (DEPTH, CONV_WIDTH, D_CONV), CONV_WIDTH ** -0.5),
        'conv_b': _normal(ks[19], (DEPTH, D_CONV), 0.01),
        'conv_ln_g': 1.0 + _normal(ks[20], (DEPTH, D_CONV), 0.01),
        'conv_ln_b': _normal(ks[21], (DEPTH, D_CONV), 0.01),
    }


def reference(x_prompt, x_sample, c_prompt, c_sample, cache_a_k, cache_a_v, cache_b_k, cache_b_v,
              state_c_conv, w_ada, b_ada, g_pre, g_post, w_ffn_gu, w_ffn_down, w_in, w_out,
              attn_sinks, conv_w, conv_b, conv_ln_g, conv_ln_b):
    pos_prompt = jnp.arange(x_prompt.shape[1], dtype=jnp.int32)
    pos_sample = PAST_LEN + jnp.arange(x_sample.shape[1], dtype=jnp.int32)
    xp = x_prompt
    xs = x_sample
    st_p = []
    st_s = []
    for l in range(DEPTH):
        p = {'w_ada': w_ada[l], 'b_ada': b_ada[l], 'g_pre': g_pre[l], 'g_post': g_post[l],
             'w_gu': w_ffn_gu[l], 'w_down': w_ffn_down[l], 'w_in': w_in[l], 'w_out': w_out[l],
             'sinks': attn_sinks[l], 'conv_w': conv_w[l], 'conv_b': conv_b[l],
             'conv_ln_g': conv_ln_g[l], 'conv_ln_b': conv_ln_b[l]}
        xp, new_p = apply_layer(xp, c_prompt, p, mixer_prompt, (pos_prompt, p))
        xs, new_s = apply_layer(xs, c_sample, p, mixer_sample,
                                (pos_sample, cache_a_k[l], cache_a_v[l], cache_b_k[l], cache_b_v[l],
                                 state_c_conv[l], p))
        st_p.append(new_p)
        st_s.append(new_s)
    new_a_k_prompt = jnp.stack([s[0] for s in st_p])
    new_a_v_prompt = jnp.stack([s[1] for s in st_p])
    new_b_k_prompt = jnp.stack([s[2] for s in st_p])
    new_b_v_prompt = jnp.stack([s[3] for s in st_p])
    new_c_prompt = jnp.stack([s[4] for s in st_p])
    new_a_k_sample = jnp.stack([s[0] for s in st_s])
    new_a_v_sample = jnp.stack([s[1] for s in st_s])
    new_b_k_sample = jnp.stack([s[2] for s in st_s])
    new_b_v_sample = jnp.stack([s[3] for s in st_s])
    new_c_sample = jnp.stack([s[4] for s in st_s])
    return (xp, xs, new_a_k_prompt, new_a_v_prompt, new_b_k_prompt, new_b_v_prompt, new_c_prompt,
            new_a_k_sample, new_a_v_sample, new_b_k_sample, new_b_v_sample, new_c_sample)
```

```python
import functools

import jax
import jax.numpy as jnp
from jax import lax
from jax.experimental import pallas as pl
from jax.experimental.pallas import tpu as pltpu

F32 = jnp.float32
BF16 = jnp.bfloat16

D_MODEL = 2048
DEPTH = 4
HEAD_DIM = 64
N_HEADS_A = 8
N_HEADS_B = 16
N_KV_B = 2
D_CONV = 512
DILATIONS_A = (1, 4, 16)
HOPS = 128
CONV_WIDTH = 31
ROT_HALF = 8
ROPE_THETA = 500000.0
PAST_LEN = 16384
RMS_EPS = 1e-6
LN_EPS = 1e-5
NEG_INF = -1e30
ATTN_SCALE = HEAD_DIM ** -0.5
N_SUB = 3

LANES = 128
MIB = 1 << 20

SLAB_QA, SLAB_KA, SLAB_VA = 0, 4, 8
SLAB_QB, SLAB_KB, SLAB_VB = 12, 20, 21
COL_VAL, COL_GATE = 2816, 3328
N_IN = 3840


def _cparams(n_axes, vmem_mib):
    return pltpu.CompilerParams(
        dimension_semantics=("arbitrary",) * n_axes,
        vmem_limit_bytes=vmem_mib * MIB,
    )


def _sigmoid(x):
    return jax.nn.sigmoid(x)


def _pre_norm(x, g, shift, scale):
    ms = jnp.mean(x * x, axis=-1, keepdims=True)
    return (x * lax.rsqrt(ms + RMS_EPS)) * g * (1.0 + scale) + shift


def _gated_post(y, g, gate, weight):
    ms = jnp.mean(y * y, axis=-1, keepdims=True)
    return (weight * gate) * ((y * lax.rsqrt(ms + RMS_EPS)) * g)


def _rope(x, cos, sin_lo, sin_hi):
    return x * cos + pltpu.roll(x, LANES - ROT_HALF, 1) * sin_lo + pltpu.roll(x, ROT_HALF, 1) * sin_hi


def _ada_kernel(c_ref, w_ref, b_ref, o_ref):
    c = c_ref[...]
    a = (c * _sigmoid(c)).astype(BF16)
    o_ref[...] = jnp.dot(a, w_ref[...].astype(BF16), preferred_element_type=F32) + b_ref[...]


def _ada_mod(c_all, w_ada, b_ada):
    depth, d, n = w_ada.shape
    rows = c_all.shape[0]
    tn = 1024
    return pl.pallas_call(
        _ada_kernel,
        grid=(depth, n // tn),
        in_specs=[
            pl.BlockSpec((rows, d), lambda l, j: (0, 0)),
            pl.BlockSpec((None, d, tn), lambda l, j: (l, 0, j)),
            pl.BlockSpec((None, 1, tn), lambda l, j: (l, 0, j)),
        ],
        out_specs=pl.BlockSpec((None, rows, tn), lambda l, j: (l, 0, j)),
        out_shape=jax.ShapeDtypeStruct((depth, rows, n), F32),
        compiler_params=_cparams(2, 40),
        name="ada_mod",
    )(c_all, w_ada, b_ada.reshape(depth, 1, n))


def _prenorm_kernel(x_ref, mod_ref, g_ref, h_ref):
    h_ref[...] = _pre_norm(x_ref[...], g_ref[...], mod_ref[0], mod_ref[1]).astype(BF16)


def _mod_spec(mod, rows_per_batch, tm):
    _, _, r, d = mod.shape
    if r == 1:
        per = rows_per_batch // tm
        return pl.BlockSpec((None, 3, 1, d), lambda i, *_: (i // per, 0, 0, 0))
    return pl.BlockSpec((None, 3, r, d), lambda i, *_: (0, 0, 0, 0))


def _prenorm(x, mod, g, *, tm, rows_per_batch):
    m, d = x.shape
    return pl.pallas_call(
        _prenorm_kernel,
        grid=(m // tm,),
        in_specs=[
            pl.BlockSpec((tm, d), lambda i: (i, 0)),
            _mod_spec(mod, rows_per_batch, tm),
            pl.BlockSpec((1, d), lambda i: (0, 0)),
        ],
        out_specs=pl.BlockSpec((tm, d), lambda i: (i, 0)),
        out_shape=jax.ShapeDtypeStruct((m, d), BF16),
        compiler_params=_cparams(1, 32),
        name="prenorm",
    )(x, mod, g)


W_BLK = 128
W_STREAMS = 4
TF = W_BLK * W_STREAMS


def _ffn_up_kernel(h_ref, *refs, f_valid):
    g_refs = refs[0:W_STREAMS]
    u_refs = refs[W_STREAMS:2 * W_STREAMS]
    a_ref, wg_s, wu_s = refs[2 * W_STREAMS:]
    j = pl.program_id(0)

    @pl.when(pl.program_id(1) == 0)
    def _():
        for q in range(W_STREAMS):
            wg_s[:, q * W_BLK:(q + 1) * W_BLK] = g_refs[q][...].astype(BF16)
            wu_s[:, q * W_BLK:(q + 1) * W_BLK] = u_refs[q][...].astype(BF16)

    h = h_ref[...]
    g = jnp.dot(h, wg_s[...], preferred_element_type=F32)
    u = jnp.dot(h, wu_s[...], preferred_element_type=F32)
    a = (g * _sigmoid(g)) * u
    col = lax.broadcasted_iota(jnp.int32, a.shape, 1)
    a_ref[...] = jnp.where(col < f_valid - j * TF, a, 0.0).astype(BF16)


def _ffn_up(h, w_gu, *, tm):
    m, d = h.shape
    f = w_gu.shape[1] // 2
    nb = f // W_BLK
    nj = pl.cdiv(f, TF)

    def wspec(base, q):
        return pl.BlockSpec(
            (d, W_BLK), lambda j, i: (0, base + jnp.minimum(W_STREAMS * j + q, nb - 1)))

    in_specs = [pl.BlockSpec((tm, d), lambda j, i: (i, 0))]
    in_specs += [wspec(0, q) for q in range(W_STREAMS)]
    in_specs += [wspec(nb, q) for q in range(W_STREAMS)]
    return pl.pallas_call(
        functools.partial(_ffn_up_kernel, f_valid=f),
        grid=(nj, m // tm),
        in_specs=in_specs,
        out_specs=pl.BlockSpec((tm, TF), lambda j, i: (i, j)),
        out_shape=jax.ShapeDtypeStruct((m, nj * TF), BF16),
        scratch_shapes=[pltpu.VMEM((d, TF), BF16), pltpu.VMEM((d, TF), BF16)],
        compiler_params=_cparams(2, 48),
        name="ffn_up",
    )(h, *([w_gu] * (2 * W_STREAMS)))


EPI_ROWS = 256


def _resid_kernel(*refs, seg_nk, nxc, weight, emit_h, tm, per_token):
    nseg = len(seg_nk)
    it = iter(refs)
    lhs_refs = [next(it) for _ in range(nseg)]
    w_refs = [next(it) for _ in range(W_STREAMS)]
    xin_ref, modc_ref, gpost_ref = next(it), next(it), next(it)
    modn_ref = gpre_ref = None
    if emit_h:
        modn_ref, gpre_ref = next(it), next(it)
    x_out = next(it)
    h_out = next(it) if emit_h else None
    acc, w_s = next(it), next(it)
    k = pl.program_id(1)
    nk = sum(seg_nk)

    for q in range(W_STREAMS):
        w_s[q * W_BLK:(q + 1) * W_BLK, :] = w_refs[q][...].astype(BF16)

    @pl.when(k == 0)
    def _():
        acc[...] = jnp.zeros_like(acc)

    if nxc > 1:
        xr = tm // nxc

        @pl.when(k < nxc)
        def _():
            x_out[pl.ds(pl.multiple_of(k * xr, xr), xr), :] = xin_ref[...]

    off = 0
    for s in range(nseg):
        lo, hi = off, off + seg_nk[s]

        @pl.when((k >= lo) & (k < hi))
        def _(s=s):
            acc[...] += jnp.dot(lhs_refs[s][...], w_s[...], preferred_element_type=F32)

        off = hi

    def epilogue(rows):
        y = acc[rows, :]
        x = xin_ref[rows, :] if nxc == 1 else x_out[rows, :]
        if per_token:
            gate = modc_ref[2, rows, :]
        else:
            gate = modc_ref[2]
        xn = x + _gated_post(y, gpost_ref[...], gate, weight)
        x_out[rows, :] = xn
        if emit_h:
            if per_token:
                shift, scale = modn_ref[0, rows, :], modn_ref[1, rows, :]
            else:
                shift, scale = modn_ref[0], modn_ref[1]
            h_out[rows, :] = _pre_norm(xn, gpre_ref[...], shift, scale).astype(BF16)

    @pl.when(k == nk - 1)
    def _():
        if tm <= EPI_ROWS:
            epilogue(slice(None))
        else:
            def body(c, carry):
                epilogue(pl.ds(pl.multiple_of(c * EPI_ROWS, EPI_ROWS), EPI_ROWS))
                return carry
            lax.fori_loop(0, tm // EPI_ROWS, body, 0)


def _resid_proj(lhs_list, w, x, modc, gpost, modn, gpre, *, tm, rows_per_batch, weight):
    m, d = x.shape
    emit_h = modn is not None
    kdim = w.shape[0]
    nb = kdim // W_BLK
    seg_nk = tuple(lhs.shape[1] // TF for lhs in lhs_list)
    nk = sum(seg_nk)
    per_token = modc.shape[2] != 1
    nxc = 1
    if tm >= 512:
        nxc = 8 if nk >= 8 else 4

    in_specs = []
    off = 0
    for lhs, n in zip(lhs_list, seg_nk):
        in_specs.append(pl.BlockSpec(
            (tm, TF), lambda i, k, off=off, n=n: (i, jnp.clip(k - off, 0, n - 1))))
        off += n
    for q in range(W_STREAMS):
        in_specs.append(pl.BlockSpec(
            (W_BLK, d), lambda i, k, q=q: (jnp.minimum(W_STREAMS * k + q, nb - 1), 0)))
    if nxc > 1:
        in_specs.append(pl.BlockSpec(
            (tm // nxc, d), lambda i, k: (i * nxc + jnp.minimum(k, nxc - 1), 0)))
    else:
        in_specs.append(pl.BlockSpec((tm, d), lambda i, k: (i, 0)))
    in_specs.append(_mod_spec(modc, rows_per_batch, tm))
    in_specs.append(pl.BlockSpec((1, d), lambda i, k: (0, 0)))
    args = list(lhs_list) + [w] * W_STREAMS + [x, modc, gpost]
    out_specs = [pl.BlockSpec((tm, d), lambda i, k: (i, 0))]
    out_shape = [jax.ShapeDtypeStruct((m, d), F32)]
    if emit_h:
        in_specs.append(_mod_spec(modn, rows_per_batch, tm))
        in_specs.append(pl.BlockSpec((1, d), lambda i, k: (0, 0)))
        args += [modn, gpre]
        out_specs.append(pl.BlockSpec((tm, d), lambda i, k: (i, 0)))
        out_shape.append(jax.ShapeDtypeStruct((m, d), BF16))
    outs = pl.pallas_call(
        functools.partial(_resid_kernel, seg_nk=seg_nk, nxc=nxc, weight=weight,
                          emit_h=emit_h, tm=tm, per_token=per_token),
        grid=(m // tm, nk),
        in_specs=in_specs,
        out_specs=out_specs,
        out_shape=out_shape,
        scratch_shapes=[pltpu.VMEM((tm, d), F32), pltpu.VMEM((TF, d), BF16)],
        compiler_params=_cparams(2, 56),
        name="resid_proj",
    )(*args)
    return (outs[0], outs[1]) if emit_h else (outs[0], None)


def _inproj_kernel(h_ref, w_ref, o_ref, w_s):
    @pl.when(pl.program_id(1) == 0)
    def _():
        w_s[...] = w_ref[...].astype(BF16)

    o_ref[...] = jnp.dot(h_ref[...], w_s[...], preferred_element_type=F32)


def _in_proj(h, w_in, *, tm):
    m, d = h.shape
    n = w_in.shape[1]
    tn = 768
    return pl.pallas_call(
        _inproj_kernel,
        grid=(n // tn, m // tm),
        in_specs=[
            pl.BlockSpec((tm, d), lambda j, i: (i, 0)),
            pl.BlockSpec((d, tn), lambda j, i: (0, j)),
        ],
        out_specs=pl.BlockSpec((tm, tn), lambda j, i: (i, j)),
        out_shape=jax.ShapeDtypeStruct((m, n), F32),
        scratch_shapes=[pltpu.VMEM((d, tn), BF16)],
        compiler_params=_cparams(2, 48),
        name="in_proj",
    )(h, w_in)


QBLK = 128
PREP_ROWS = 512


def _attn_kernel(*refs, dils, shared, t, tail):
    pad = QBLK * max(dils)
    it = iter(refs)
    q_ref, k_ref, v_ref, cos_ref, slo_ref, shi_ref = [next(it) for _ in range(6)]
    sink_ref = next(it) if shared else None
    o_ref, kt_ref = next(it), next(it)
    qs = next(it)
    nvar = 2 if shared else 1
    kps = [next(it) for _ in range(nvar)]
    vps = [next(it) for _ in range(nvar)]
    acc, m_s, l_s = next(it), next(it), next(it)

    kv_head = pl.program_id(1) // 4
    lane = lax.broadcasted_iota(jnp.int32, (1, LANES), 1)
    low = lane < HEAD_DIM

    for r in kps + vps:
        r[0:pad, :] = jnp.zeros((pad, LANES), F32)

    def prep(c, carry):
        r0 = pl.multiple_of(c * PREP_ROWS, PREP_ROWS)
        rows = pl.ds(r0, PREP_ROWS)
        cos, slo, shi = cos_ref[rows, :], slo_ref[rows, :], shi_ref[rows, :]
        qs[rows, :] = _rope(q_ref[rows, :], cos, slo, shi) * ATTN_SCALE
        kr = _rope(k_ref[rows, :], cos, slo, shi)
        v = v_ref[rows, :]
        prow = pl.ds(pad + r0, PREP_ROWS)
        if shared:
            kroll = pltpu.roll(kr, HEAD_DIM, 1)
            vroll = pltpu.roll(v, HEAD_DIM, 1)
            for hh in range(2):
                kps[hh][prow, :] = jnp.where(kv_head == hh, kr, kroll)
                vps[hh][prow, :] = jnp.where(kv_head == hh, v, vroll)
        else:
            kps[0][prow, :] = kr
            vps[0][prow, :] = v
        acc[rows, :] = jnp.zeros((PREP_ROWS, LANES), F32)
        m_s[rows, :] = jnp.full((PREP_ROWS, LANES), NEG_INF, F32)
        l_s[rows, :] = jnp.zeros((PREP_ROWS, LANES), F32)
        return carry

    lax.fori_loop(0, t // PREP_ROWS, prep, 0)

    trow = pl.ds(pad + t - tail, tail)
    if shared:
        kt_ref[...] = jnp.where(kv_head == 0, kps[0][trow, :], kps[1][trow, :])
    else:
        kt_ref[...] = kps[0][trow, :]

    row = lax.broadcasted_iota(jnp.int32, (QBLK, 2 * QBLK), 0)
    col = lax.broadcasted_iota(jnp.int32, (QBLK, 2 * QBLK), 1)
    band = (col >= row) & (col <= row + HOPS)
    current = col >= QBLK
    nt = (((1,), (1,)), ((), ()))

    for d in dils:
        shift = d.bit_length() - 1

        def block(b, carry, d=d, shift=shift):
            n = lax.shift_right_logical(b, jnp.int32(shift))
            q0 = n * (QBLK * d) + (b & (d - 1))
            k0 = q0 + pad - QBLK * d
            if d == 1:
                qsl = pl.ds(pl.multiple_of(q0, QBLK), QBLK)
                ksl = pl.ds(pl.multiple_of(k0, QBLK), 2 * QBLK)
            else:
                qsl = pl.ds(q0, QBLK, stride=d)
                ksl = pl.ds(k0, 2 * QBLK, stride=d)
            q = qs[qsl, :]
            valid = band & (current | (n > 0))
            m_old, l_old, a_old = m_s[qsl, :], l_s[qsl, :], acc[qsl, :]
            kb = [r[ksl, :].astype(BF16) for r in kps]
            vb = [r[ksl, :].astype(BF16) for r in vps]
            parts = []
            for hh in range(2):
                var = hh if shared else 0
                qh = jnp.where(low if hh == 0 else jnp.logical_not(low), q, 0.0).astype(BF16)
                s = lax.dot_general(qh, kb[var], nt, preferred_element_type=F32)
                s = jnp.where(valid, s, NEG_INF)
                mo = m_old[:, hh * HEAD_DIM:hh * HEAD_DIM + 1]
                mn = jnp.maximum(mo, jnp.max(s, axis=1, keepdims=True))
                p = jnp.exp(s - mn)
                parts.append((mn, jnp.exp(mo - mn), jnp.sum(p, axis=1, keepdims=True),
                              jnp.dot(p.astype(BF16), vb[var], preferred_element_type=F32)))
            mn, alpha, ls, pv = [jnp.where(low, parts[0][i], parts[1][i]) for i in range(4)]
            m_s[qsl, :] = mn
            l_s[qsl, :] = alpha * l_old + ls
            acc[qsl, :] = alpha * a_old + pv
            return carry

        lax.fori_loop(0, t // QBLK, block, 0)

    def finish(c, carry):
        rows = pl.ds(pl.multiple_of(c * PREP_ROWS, PREP_ROWS), PREP_ROWS)
        den = l_s[rows, :]
        if shared:
            den = den + jnp.exp(sink_ref[...] - m_s[rows, :])
        o_ref[rows, :] = (acc[rows, :] / den).astype(BF16)
        return carry

    lax.fori_loop(0, t // PREP_ROWS, finish, 0)


def _attention(u, tables, sinks, *, n_batch, t, dils, q_slab, k_slab, v_slab, n_slab, shared,
               tail):
    pad = QBLK * max(dils)
    cos, slo, shi = tables
    const = lambda b, s: (0, 0)
    once = pl.Buffered(1)
    in_specs = [
        pl.BlockSpec((t, LANES), lambda b, s: (b, q_slab + s)),
        pl.BlockSpec((t, LANES), (lambda b, s: (b, k_slab)) if shared
                     else (lambda b, s: (b, k_slab + s))),
        pl.BlockSpec((t, LANES), (lambda b, s: (b, v_slab)) if shared
                     else (lambda b, s: (b, v_slab + s))),
        pl.BlockSpec((t, LANES), const, pipeline_mode=once),
        pl.BlockSpec((t, LANES), const, pipeline_mode=once),
        pl.BlockSpec((t, LANES), const, pipeline_mode=once),
    ]
    args = [u, u, u, cos, slo, shi]
    if shared:
        in_specs.append(pl.BlockSpec((None, 1, LANES), lambda b, s: (s, 0, 0)))
        args.append(sinks)
    n_kslab = 1 if shared else n_slab
    nvar = 2 if shared else 1
    scratch = [pltpu.VMEM((t, LANES), F32)]
    scratch += [pltpu.VMEM((pad + t, LANES), F32) for _ in range(2 * nvar)]
    scratch += [pltpu.VMEM((t, LANES), F32) for _ in range(3)]
    return pl.pallas_call(
        functools.partial(_attn_kernel, dils=dils, shared=shared, t=t, tail=tail),
        grid=(n_batch, n_slab),
        in_specs=in_specs,
        out_specs=[
            pl.BlockSpec((t, LANES), lambda b, s: (b, s)),
            pl.BlockSpec((tail, LANES), (lambda b, s: (b, 0)) if shared
                         else (lambda b, s: (b, s))),
        ],
        out_shape=[
            jax.ShapeDtypeStruct((n_batch * t, n_slab * LANES), BF16),
            jax.ShapeDtypeStruct((n_batch * tail, n_kslab * LANES), F32),
        ],
        scratch_shapes=scratch,
        compiler_params=_cparams(2, 48),
        name="attn_shared" if shared else "attn_dilated",
    )(*args)


CONV_TB = 256
CONV_HALO = 32
CONV_RC = 32


def _layer_norm_swish(y, g, b):
    mu = jnp.mean(y, axis=-1, keepdims=True)
    yc = y - mu
    var = jnp.mean(yc * yc, axis=-1, keepdims=True)
    z = yc * lax.rsqrt(var + LN_EPS) * g + b
    return z * _sigmoid(z)


def _conv_kernel(v0, v1, g0, g1, cw_ref, cb_ref, lg_ref, lb_ref, o_ref, st_ref, ext):
    tt = pl.program_id(1)
    half = D_CONV // 2

    @pl.when(tt == 0)
    def _():
        ext[0:CONV_HALO, :] = jnp.zeros((CONV_HALO, D_CONV), F32)

    @pl.when(tt > 0)
    def _():
        ext[0:CONV_HALO, :] = ext[CONV_TB:CONV_TB + CONV_HALO, :]

    ext[CONV_HALO:CONV_HALO + CONV_TB, 0:half] = v0[...] * _sigmoid(g0[...])
    ext[CONV_HALO:CONV_HALO + CONV_TB, half:D_CONV] = v1[...] * _sigmoid(g1[...])
    st_ref[...] = ext[CONV_TB:CONV_TB + CONV_HALO, :]

    lead = CONV_HALO - (CONV_WIDTH - 1)
    for c in range(CONV_TB // CONV_RC):
        r0 = c * CONV_RC
        y = jnp.zeros((CONV_RC, D_CONV), F32) + cb_ref[...]
        for w in range(CONV_WIDTH):
            y = y + ext[r0 + lead + w:r0 + lead + w + CONV_RC, :] * cw_ref[w:w + 1, :]
        o_ref[r0:r0 + CONV_RC, :] = _layer_norm_swish(y, lg_ref[...], lb_ref[...]).astype(BF16)


def _conv_module(u, cw, cb, lg, lb, *, n_batch, t):
    nt = t // CONV_TB
    half = D_CONV // 2
    v_blk, g_blk = COL_VAL // half, COL_GATE // half
    row = lambda b, i: b * nt + i
    vec = pl.BlockSpec((1, D_CONV), lambda b, i: (0, 0))
    return pl.pallas_call(
        _conv_kernel,
        grid=(n_batch, nt),
        in_specs=[
            pl.BlockSpec((CONV_TB, half), lambda b, i: (row(b, i), v_blk)),
            pl.BlockSpec((CONV_TB, half), lambda b, i: (row(b, i), v_blk + 1)),
            pl.BlockSpec((CONV_TB, half), lambda b, i: (row(b, i), g_blk)),
            pl.BlockSpec((CONV_TB, half), lambda b, i: (row(b, i), g_blk + 1)),
            pl.BlockSpec((CONV_HALO, D_CONV), lambda b, i: (0, 0)),
            vec, vec, vec,
        ],
        out_specs=[
            pl.BlockSpec((CONV_TB, D_CONV), lambda b, i: (row(b, i), 0)),
            pl.BlockSpec((CONV_HALO, D_CONV), lambda b, i: (b, 0)),
        ],
        out_shape=[
            jax.ShapeDtypeStruct((n_batch * t, D_CONV), BF16),
            jax.ShapeDtypeStruct((n_batch * CONV_HALO, D_CONV), F32),
        ],
        scratch_shapes=[pltpu.VMEM((CONV_HALO + CONV_TB, D_CONV), F32)],
        compiler_params=_cparams(2, 32),
        name="conv_module",
    )(u, u, u, u, cw, cb, lg, lb)


ROPE_SLABS = tuple(range(SLAB_QA, SLAB_VA)) + tuple(range(SLAB_QB, SLAB_VB))
Q_SLABS = tuple(range(SLAB_QA, SLAB_KA)) + tuple(range(SLAB_QB, SLAB_KB))
VAL_SLABS = tuple(range(COL_VAL // LANES, COL_GATE // LANES))


def _sample_post_kernel(u_ref, cos_ref, slo_ref, shi_ref, o_ref):
    for s in range(N_IN // LANES):
        x = u_ref[:, s * LANES:(s + 1) * LANES]
        if s in ROPE_SLABS:
            x = _rope(x, cos_ref[...], slo_ref[...], shi_ref[...])
            if s in Q_SLABS:
                x = x * ATTN_SCALE
        elif s in VAL_SLABS:
            gs = s + D_CONV // LANES
            x = x * _sigmoid(u_ref[:, gs * LANES:(gs + 1) * LANES])
        o_ref[:, s * LANES:(s + 1) * LANES] = x


def _sample_post(u, tables):
    m, n = u.shape
    full = pl.BlockSpec((m, n), lambda i: (0, 0))
    vec = pl.BlockSpec((1, LANES), lambda i: (0, 0))
    return pl.pallas_call(
        _sample_post_kernel,
        grid=(1,),
        in_specs=[full, vec, vec, vec],
        out_specs=full,
        out_shape=jax.ShapeDtypeStruct((m, n), F32),
        compiler_params=_cparams(1, 16),
        name="sample_post",
    )(u, *tables)


def _sattn_a_kernel(k16, v16, k4, v4, k1, v1, q_ref, kn_ref, vn_ref, o_ref):
    nkeys = HOPS * N_HEADS_A
    nt = (((1,), (1,)), ((), ()))
    q = q_ref[...]
    qb = q.astype(BF16)
    head = lax.broadcasted_iota(jnp.int32, (N_HEADS_A, nkeys), 0)
    colh = lax.broadcasted_iota(jnp.int32, (N_HEADS_A, nkeys), 1) & (N_HEADS_A - 1)
    own = head == colh
    scores = []
    for kr in (k16, k4, k1):
        k2 = kr[...].reshape(nkeys, HEAD_DIM).astype(BF16)
        scores.append(jnp.where(own, lax.dot_general(qb, k2, nt, preferred_element_type=F32),
                                NEG_INF))
    m = jnp.sum(q * kn_ref[...], axis=1, keepdims=True)
    s_new = m
    for s in scores:
        m = jnp.maximum(m, jnp.max(s, axis=1, keepdims=True))
    den = jnp.exp(s_new - m) * float(len(DILATIONS_A))
    num = den * vn_ref[...]
    for s, vr in zip(scores, (v16, v4, v1)):
        p = jnp.exp(s - m)
        den = den + jnp.sum(p, axis=1, keepdims=True)
        num = num + jnp.dot(p.astype(BF16), vr[...].reshape(nkeys, HEAD_DIM).astype(BF16),
                            preferred_element_type=F32)
    o_ref[...] = num / den


def _sample_attn_a(cache_k, cache_v, layer, q, kn, vn):
    depth, nb, la, nh, e = cache_k.shape
    blk = (HOPS, nh, e)

    def views(c):
        return (c.reshape(depth, nb, la // 16, 16, nh, e),
                c.reshape(depth, nb, la // 4, 4, nh, e), c)

    specs16 = pl.BlockSpec((None, None, HOPS, None, nh, e), lambda b: (layer, b, 0, 0, 0, 0))
    specs4 = pl.BlockSpec((None, None, HOPS, None, nh, e),
                          lambda b: (layer, b, la // 4 // HOPS - 1, 0, 0, 0))
    specs1 = pl.BlockSpec((None, None) + blk, lambda b: (layer, b, la // HOPS - 1, 0, 0))
    tok = pl.BlockSpec((None, nh, e), lambda b: (b, 0, 0))
    k16, k4, k1 = views(cache_k)
    v16, v4, v1 = views(cache_v)
    return pl.pallas_call(
        _sattn_a_kernel,
        grid=(nb,),
        in_specs=[specs16, specs16, specs4, specs4, specs1, specs1, tok, tok, tok],
        out_specs=tok,
        out_shape=jax.ShapeDtypeStruct((nb, nh, e), F32),
        compiler_params=_cparams(1, 32),
        name="sample_attn_dilated",
    )(k16, v16, k4, v4, k1, v1, q, kn, vn)


def _sattn_b_kernel(kc_ref, vc_ref, q_ref, kn_ref, vn_ref, sink_ref, o_ref):
    nt = (((1,), (1,)), ((), ()))
    q = q_ref[...]
    s = lax.dot_general(q.astype(BF16), kc_ref[...].astype(BF16), nt, preferred_element_type=F32)
    s_new = jnp.sum(q * kn_ref[...], axis=1, keepdims=True)
    m = jnp.maximum(jnp.max(s, axis=1, keepdims=True), s_new)
    p = jnp.exp(s - m)
    p_new = jnp.exp(s_new - m)
    den = jnp.sum(p, axis=1, keepdims=True) + p_new + jnp.exp(sink_ref[...] - m)
    num = jnp.dot(p.astype(BF16), vc_ref[...].astype(BF16), preferred_element_type=F32)
    o_ref[...] = (num + p_new * vn_ref[...]) / den


def _sample_attn_b(kc, vc, layer, q_exp, kn, vn, sinks):
    _, nb, lb, w = kc.shape
    nh = q_exp.shape[1]
    cache = pl.BlockSpec((None, None, lb, w), lambda b: (layer, b, 0, 0))
    tokq = pl.BlockSpec((None, nh, w), lambda b: (b, 0, 0))
    tok1 = pl.BlockSpec((None, 1, w), lambda b: (b, 0, 0))
    return pl.pallas_call(
        _sattn_b_kernel,
        grid=(nb,),
        in_specs=[cache, cache, tokq, tok1, tok1, pl.BlockSpec((nh, 1), lambda b: (0, 0))],
        out_specs=tokq,
        out_shape=jax.ShapeDtypeStruct((nb, nh, w), F32),
        compiler_params=_cparams(1, 16),
        name="sample_attn_shared",
    )(kc, vc, q_exp, kn, vn, sinks)


def _sconv_kernel(st_ref, glu_ref, cw_ref, cb_ref, lg_ref, lb_ref, o_ref, ns_ref):
    hist = CONV_WIDTH - 1
    glu = glu_ref[...]
    y = glu * cw_ref[hist:hist + 1, :] + cb_ref[...]
    for w in range(hist):
        y = y + st_ref[:, w * D_CONV:(w + 1) * D_CONV] * cw_ref[w:w + 1, :]
    o_ref[...] = _layer_norm_swish(y, lg_ref[...], lb_ref[...]).astype(BF16)
    ns_ref[:, 0:(hist - 1) * D_CONV] = st_ref[:, D_CONV:hist * D_CONV]
    ns_ref[:, (hist - 1) * D_CONV:hist * D_CONV] = glu


def _sample_conv(state2d, layer, glu, cw, cb, lg, lb):
    _, nb, width = state2d.shape
    st = pl.BlockSpec((None, nb, width), lambda i: (layer, 0, 0))
    tok = pl.BlockSpec((nb, D_CONV), lambda i: (0, 0))
    vec = pl.BlockSpec((1, D_CONV), lambda i: (0, 0))
    return pl.pallas_call(
        _sconv_kernel,
        grid=(1,),
        in_specs=[st, tok, pl.BlockSpec((CONV_HALO, D_CONV), lambda i: (0, 0)), vec, vec, vec],
        out_specs=[tok, pl.BlockSpec((nb, width), lambda i: (0, 0))],
        out_shape=[jax.ShapeDtypeStruct((nb, D_CONV), BF16),
                   jax.ShapeDtypeStruct((nb, width), F32)],
        compiler_params=_cparams(1, 16),
        name="sample_conv",
    )(state2d, glu, cw, cb, lg, lb)


SHIFT_BATCH_CHUNK = 8


def _shift_copies(cache_refs, new_refs, out_refs, sem):
    copies = []
    for cache, new, out in zip(cache_refs, new_refs, out_refs):
        depth, nb, rows = cache.shape[:3]
        for l in range(depth):
            for b0 in range(0, nb, SHIFT_BATCH_CHUNK):
                bs = pl.ds(b0, SHIFT_BATCH_CHUNK)
                copies.append((cache.at[l, bs, pl.ds(1, rows - 1)],
                               out.at[l, bs, pl.ds(0, rows - 1)]))
            copies.append((new.at[l], out.at[l, :, pl.ds(rows - 1, 1)]))
    return [pltpu.make_async_copy(src, dst, sem.at[i]) for i, (src, dst) in enumerate(copies)]


def _shift_kernel(*refs, n_arr):
    cache_refs = refs[0:n_arr]
    new_refs = refs[n_arr:2 * n_arr]
    out_refs = refs[2 * n_arr:3 * n_arr]
    sem = refs[3 * n_arr]
    copies = _shift_copies(cache_refs, new_refs, out_refs, sem)
    for c in copies:
        c.start()
    for c in copies:
        c.wait()


def _shift_caches(caches, news):
    n_arr = len(caches)
    n_copies = sum(c.shape[0] * (c.shape[1] // SHIFT_BATCH_CHUNK + 1) for c in caches)
    any_spec = pl.BlockSpec(memory_space=pl.ANY)
    return pl.pallas_call(
        functools.partial(_shift_kernel, n_arr=n_arr),
        in_specs=[any_spec] * (2 * n_arr),
        out_specs=[any_spec] * n_arr,
        out_shape=[jax.ShapeDtypeStruct(c.shape, c.dtype) for c in caches],
        scratch_shapes=[pltpu.SemaphoreType.DMA((n_copies,))],
        name="cache_shift",
    )(*caches, *news)


def _rope_tables(pos):
    n = pos.shape[0]
    inv_freq = 1.0 / (ROPE_THETA ** (jnp.arange(ROT_HALF, dtype=F32) / ROT_HALF))
    ang = pos.astype(F32)[:, None] * inv_freq[None, :]
    cos, sin = jnp.cos(ang), jnp.sin(ang)
    rest = HEAD_DIM - 2 * ROT_HALF
    c = jnp.concatenate([cos, cos, jnp.ones((n, rest), F32)], axis=1)
    lo = jnp.concatenate([-sin, jnp.zeros((n, HEAD_DIM - ROT_HALF), F32)], axis=1)
    hi = jnp.concatenate([jnp.zeros((n, ROT_HALF), F32), sin, jnp.zeros((n, rest), F32)], axis=1)
    return tuple(jnp.tile(x, (1, LANES // HEAD_DIM)) for x in (c, lo, hi))


def kernel(x_prompt, x_sample, c_prompt, c_sample, cache_a_k, cache_a_v, cache_b_k, cache_b_v,
           state_c_conv, w_ada, b_ada, g_pre, g_post, w_ffn_gu, w_ffn_down, w_in, w_out,
           attn_sinks, conv_w, conv_b, conv_ln_g, conv_ln_b):
    nbp, t, d = x_prompt.shape
    nbs = x_sample.shape[0]
    mp = nbp * t
    depth = w_ada.shape[0]
    la, lb = cache_a_k.shape[2], cache_b_k.shape[2]
    hist = CONV_WIDTH - 1
    tm_p = 1024

    xp = x_prompt.reshape(mp, d)
    xs = x_sample.reshape(nbs, d)

    s_row = 32
    c_all = jnp.zeros((s_row + nbs, d), F32).at[:nbp].set(c_prompt).at[s_row:].set(c_sample)
    mod_all = _ada_mod(c_all, w_ada, b_ada)

    def mod_p(l, s):
        return mod_all[l, :nbp, 3 * s * d:3 * (s + 1) * d].reshape(nbp, 3, 1, d)

    def mod_s(l, s):
        m = mod_all[l, s_row:, 3 * s * d:3 * (s + 1) * d].reshape(nbs, 3, d)
        return jnp.transpose(m, (1, 0, 2))[None]

    def nxt(l, s):
        return (l, s + 1) if s + 1 < N_SUB else (l + 1, 0)

    tab_p = _rope_tables(jnp.arange(t, dtype=jnp.int32))
    tab_s = _rope_tables(PAST_LEN + jnp.arange(1, dtype=jnp.int32))
    sink_slabs = jnp.repeat(attn_sinks, HEAD_DIM, axis=1).reshape(depth, N_HEADS_B // 2, 1, LANES)
    sink_cols = attn_sinks.reshape(depth, N_HEADS_B, 1)
    cw_pad = jnp.pad(conv_w, ((0, 0), (0, CONV_HALO - CONV_WIDTH), (0, 0)))
    vec = lambda a, l: a[l].reshape(1, -1)

    cache_bk2 = cache_b_k.reshape(depth, nbs, lb, N_KV_B * HEAD_DIM)
    cache_bv2 = cache_b_v.reshape(depth, nbs, lb, N_KV_B * HEAD_DIM)
    state2d = state_c_conv.reshape(depth, nbs, hist * D_CONV)
    kv_of_head = (jnp.arange(N_HEADS_B) // (N_HEADS_B // N_KV_B))[:, None] == jnp.arange(N_KV_B)

    hp = _prenorm(xp, mod_p(0, 0), vec(g_pre[0], 0), tm=tm_p, rows_per_batch=t)
    hs = _prenorm(xs, mod_s(0, 0), vec(g_pre[0], 0), tm=nbs, rows_per_batch=nbs)

    st_p = [[] for _ in range(5)]
    new_rows = [[] for _ in range(4)]
    st_c_s = []

    def resid(lhs, w, x, l, s, weight, mod_fn, tm, rpb):
        ln, sn = nxt(l, s)
        last = ln >= depth
        return _resid_proj(
            lhs, w, x, mod_fn(l, s), vec(g_post[l], s),
            None if last else mod_fn(ln, sn), None if last else vec(g_pre[ln], sn),
            tm=tm, rows_per_batch=rpb, weight=weight)

    for l in range(depth):
        a = _ffn_up(hp, w_ffn_gu[l, 0], tm=tm_p)
        xp, hp = resid([a], w_ffn_down[l, 0], xp, l, 0, 0.5, mod_p, tm_p, t)
        u = _in_proj(hp, w_in[l], tm=tm_p)
        oa, kta = _attention(u, tab_p, None, n_batch=nbp, t=t, dils=DILATIONS_A,
                             q_slab=SLAB_QA, k_slab=SLAB_KA, v_slab=SLAB_VA,
                             n_slab=N_HEADS_A // 2, shared=False, tail=min(la, t))
        ob, ktb = _attention(u, tab_p, sink_slabs[l], n_batch=nbp, t=t, dils=(1,),
                             q_slab=SLAB_QB, k_slab=SLAB_KB, v_slab=SLAB_VB,
                             n_slab=N_HEADS_B // 2, shared=True, tail=min(lb, t))
        oc, cst = _conv_module(u, cw_pad[l], vec(conv_b, l), vec(conv_ln_g, l), vec(conv_ln_b, l),
                               n_batch=nbp, t=t)
        xp, hp = resid([oa, ob, oc], w_out[l], xp, l, 1, 1.0, mod_p, tm_p, t)
        a = _ffn_up(hp, w_ffn_gu[l, 1], tm=tm_p)
        xp, hp = resid([a], w_ffn_down[l, 1], xp, l, 2, 0.5, mod_p, tm_p, t)

        u3 = u.reshape(nbp, t, N_IN)
        ta, tb = min(la, t), min(lb, t)
        st_p[0].append(kta.reshape(nbp, ta, N_HEADS_A, HEAD_DIM))
        st_p[1].append(u3[:, t - ta:, SLAB_VA * LANES:SLAB_QB * LANES]
                       .reshape(nbp, ta, N_HEADS_A, HEAD_DIM))
        st_p[2].append(ktb.reshape(nbp, tb, N_KV_B, HEAD_DIM))
        st_p[3].append(u3[:, t - tb:, SLAB_VB * LANES:COL_VAL].reshape(nbp, tb, N_KV_B, HEAD_DIM))
        st_p[4].append(cst.reshape(nbp, CONV_HALO, D_CONV)[:, CONV_HALO - hist:])

        a = _ffn_up(hs, w_ffn_gu[l, 0], tm=nbs)
        xs, hs = resid([a], w_ffn_down[l, 0], xs, l, 0, 0.5, mod_s, nbs, nbs)
        ur = _sample_post(_in_proj(hs, w_in[l], tm=nbs), tab_s)
        seg = lambda lo, hi: ur[:, lo * LANES:hi * LANES]
        qa = seg(SLAB_QA, SLAB_KA).reshape(nbs, N_HEADS_A, HEAD_DIM)
        kna = seg(SLAB_KA, SLAB_VA).reshape(nbs, N_HEADS_A, HEAD_DIM)
        vna = seg(SLAB_VA, SLAB_QB).reshape(nbs, N_HEADS_A, HEAD_DIM)
        oa = _sample_attn_a(cache_a_k, cache_a_v, l, qa, kna, vna)
        qb = seg(SLAB_QB, SLAB_KB).reshape(nbs, N_HEADS_B, 1, HEAD_DIM)
        q_exp = jnp.where(kv_of_head[None, :, :, None], qb, 0.0).reshape(nbs, N_HEADS_B, LANES)
        knb = seg(SLAB_KB, SLAB_VB)
        vnb = seg(SLAB_VB, SLAB_VB + 1)
        ob2 = _sample_attn_b(cache_bk2, cache_bv2, l, q_exp, knb.reshape(nbs, 1, LANES),
                             vnb.reshape(nbs, 1, LANES), sink_cols[l])
        ob = jnp.sum(jnp.where(kv_of_head[None, :, :, None],
                               ob2.reshape(nbs, N_HEADS_B, N_KV_B, HEAD_DIM), 0.0), axis=2)
        glu = ur[:, COL_VAL:COL_GATE]
        oc, ns = _sample_conv(state2d, l, glu, cw_pad[l], vec(conv_b, l), vec(conv_ln_g, l),
                              vec(conv_ln_b, l))
        mix = [oa.reshape(nbs, -1).astype(BF16), ob.reshape(nbs, -1).astype(BF16), oc]
        xs, hs = resid(mix, w_out[l], xs, l, 1, 1.0, mod_s, nbs, nbs)
        a = _ffn_up(hs, w_ffn_gu[l, 1], tm=nbs)
        xs, hs = resid([a], w_ffn_down[l, 1], xs, l, 2, 0.5, mod_s, nbs, nbs)

        new_rows[0].append(kna.reshape(nbs, 1, N_HEADS_A, HEAD_DIM))
        new_rows[1].append(vna.reshape(nbs, 1, N_HEADS_A, HEAD_DIM))
        new_rows[2].append(knb.reshape(nbs, 1, N_KV_B, HEAD_DIM))
        new_rows[3].append(vnb.reshape(nbs, 1, N_KV_B, HEAD_DIM))
        st_c_s.append(ns.reshape(nbs, hist, D_CONV))

    shifted = _shift_caches([cache_a_k, cache_a_v, cache_b_k, cache_b_v],
                            [jnp.stack(r) for r in new_rows])
    return (xp.reshape(nbp, t, d), xs.reshape(nbs, 1, d),
            *[jnp.stack(s) for s in st_p], *shifted, jnp.stack(st_c_s))
```

```python
import functools

import jax
import jax.numpy as jnp
from jax import lax
from jax.experimental import pallas as pl
from jax.experimental.pallas import tpu as pltpu

F32 = jnp.float32
BF16 = jnp.bfloat16

D_MODEL = 2048
DEPTH = 4
HEAD_DIM = 64
N_HEADS_A = 8
N_HEADS_B = 16
N_KV_B = 2
D_CONV = 512
DILATIONS_A = (1, 4, 16)
HOPS = 128
CONV_WIDTH = 31
ROT_HALF = 8
ROPE_THETA = 500000.0
PAST_LEN = 16384
RMS_EPS = 1e-6
LN_EPS = 1e-5
NEG_INF = -1e30
ATTN_SCALE = HEAD_DIM ** -0.5
N_SUB = 3

LANES = 128
MIB = 1 << 20

SLAB_QA, SLAB_KA, SLAB_VA = 0, 4, 8
SLAB_QB, SLAB_KB, SLAB_VB = 12, 20, 21
COL_VAL, COL_GATE = 2816, 3328
N_IN = 3840


def _cparams(n_axes, vmem_mib):
    return pltpu.CompilerParams(
        dimension_semantics=("arbitrary",) * n_axes,
        vmem_limit_bytes=vmem_mib * MIB,
    )


def _sigmoid(x):
    return jax.nn.sigmoid(x)


def _pre_norm(x, g, shift, scale):
    ms = jnp.mean(x * x, axis=-1, keepdims=True)
    return (x * lax.rsqrt(ms + RMS_EPS)) * g * (1.0 + scale) + shift


def _gated_post(y, g, gate, weight):
    ms = jnp.mean(y * y, axis=-1, keepdims=True)
    return (weight * gate) * ((y * lax.rsqrt(ms + RMS_EPS)) * g)


def _rope(x, cos, sin_lo, sin_hi):
    return x * cos + pltpu.roll(x, LANES - ROT_HALF, 1) * sin_lo + pltpu.roll(x, ROT_HALF, 1) * sin_hi


def _ada_kernel(c_ref, w_ref, b_ref, o_ref):
    c = c_ref[...]
    a = (c * _sigmoid(c)).astype(BF16)
    o_ref[...] = jnp.dot(a, w_ref[...].astype(BF16), preferred_element_type=F32) + b_ref[...]


def _ada_mod(c_all, w_ada, b_ada):
    depth, d, n = w_ada.shape
    rows = c_all.shape[0]
    tn = 1024
    return pl.pallas_call(
        _ada_kernel,
        grid=(depth, n // tn),
        in_specs=[
            pl.BlockSpec((rows, d), lambda l, j: (0, 0)),
            pl.BlockSpec((None, d, tn), lambda l, j: (l, 0, j)),
            pl.BlockSpec((None, 1, tn), lambda l, j: (l, 0, j)),
        ],
        out_specs=pl.BlockSpec((None, rows, tn), lambda l, j: (l, 0, j)),
        out_shape=jax.ShapeDtypeStruct((depth, rows, n), F32),
        compiler_params=_cparams(2, 40),
        name="ada_mod",
    )(c_all, w_ada, b_ada.reshape(depth, 1, n))


def _prenorm_kernel(x_ref, mod_ref, g_ref, h_ref):
    h_ref[...] = _pre_norm(x_ref[...], g_ref[...], mod_ref[0], mod_ref[1]).astype(BF16)


def _mod_spec(mod, rows_per_batch, tm):
    _, _, r, d = mod.shape
    if r == 1:
        per = rows_per_batch // tm
        return pl.BlockSpec((None, 3, 1, d), lambda i, *_: (i // per, 0, 0, 0))
    return pl.BlockSpec((None, 3, r, d), lambda i, *_: (0, 0, 0, 0))


def _prenorm(x, mod, g, *, tm, rows_per_batch):
    m, d = x.shape
    return pl.pallas_call(
        _prenorm_kernel,
        grid=(m // tm,),
        in_specs=[
            pl.BlockSpec((tm, d), lambda i: (i, 0)),
            _mod_spec(mod, rows_per_batch, tm),
            pl.BlockSpec((1, d), lambda i: (0, 0)),
        ],
        out_specs=pl.BlockSpec((tm, d), lambda i: (i, 0)),
        out_shape=jax.ShapeDtypeStruct((m, d), BF16),
        compiler_params=_cparams(1, 32),
        name="prenorm",
    )(x, mod, g)


W_BLK = 128
W_STREAMS = 4
TF = W_BLK * W_STREAMS


def _ffn_up_kernel(h_ref, *refs, f_valid):
    g_refs = refs[0:W_STREAMS]
    u_refs = refs[W_STREAMS:2 * W_STREAMS]
    a_ref, wg_s, wu_s = refs[2 * W_STREAMS:]
    j = pl.program_id(0)

    @pl.when(pl.program_id(1) == 0)
    def _():
        for q in range(W_STREAMS):
            wg_s[:, q * W_BLK:(q + 1) * W_BLK] = g_refs[q][...].astype(BF16)
            wu_s[:, q * W_BLK:(q + 1) * W_BLK] = u_refs[q][...].astype(BF16)

    h = h_ref[...]
    g = jnp.dot(h, wg_s[...], preferred_element_type=F32)
    u = jnp.dot(h, wu_s[...], preferred_element_type=F32)
    a = (g * _sigmoid(g)) * u
    col = lax.broadcasted_iota(jnp.int32, a.shape, 1)
    a_ref[...] = jnp.where(col < f_valid - j * TF, a, 0.0).astype(BF16)


def _ffn_up(h, w_gu, *, tm):
    m, d = h.shape
    f = w_gu.shape[1] // 2
    nb = f // W_BLK
    nj = pl.cdiv(f, TF)

    def wspec(base, q):
        return pl.BlockSpec(
            (d, W_BLK), lambda j, i: (0, base + jnp.minimum(W_STREAMS * j + q, nb - 1)))

    in_specs = [pl.BlockSpec((tm, d), lambda j, i: (i, 0))]
    in_specs += [wspec(0, q) for q in range(W_STREAMS)]
    in_specs += [wspec(nb, q) for q in range(W_STREAMS)]
    return pl.pallas_call(
        functools.partial(_ffn_up_kernel, f_valid=f),
        grid=(nj, m // tm),
        in_specs=in_specs,
        out_specs=pl.BlockSpec((tm, TF), lambda j, i: (i, j)),
        out_shape=jax.ShapeDtypeStruct((m, nj * TF), BF16),
        scratch_shapes=[pltpu.VMEM((d, TF), BF16), pltpu.VMEM((d, TF), BF16)],
        compiler_params=_cparams(2, 48),
        name="ffn_up",
    )(h, *([w_gu] * (2 * W_STREAMS)))


EPI_ROWS = 256


def _resid_kernel(*refs, seg_nk, nxc, weight, emit_h, tm, per_token):
    nseg = len(seg_nk)
    it = iter(refs)
    lhs_refs = [next(it) for _ in range(nseg)]
    w_refs = [next(it) for _ in range(W_STREAMS)]
    xin_ref, modc_ref, gpost_ref = next(it), next(it), next(it)
    modn_ref = gpre_ref = None
    if emit_h:
        modn_ref, gpre_ref = next(it), next(it)
    x_out = next(it)
    h_out = next(it) if emit_h else None
    acc = next(it)
    k = pl.program_id(1)
    nk = sum(seg_nk)

    w = jnp.concatenate([r[...].astype(BF16) for r in w_refs], axis=0)

    @pl.when(k == 0)
    def _():
        acc[...] = jnp.zeros_like(acc)

    if nxc > 1:
        xr = tm // nxc

        @pl.when(k < nxc)
        def _():
            x_out[pl.ds(pl.multiple_of(k * xr, xr), xr), :] = xin_ref[...]

    if nseg == 1:
        acc[...] += jnp.dot(lhs_refs[0][...], w, preferred_element_type=F32)
    else:
        off = 0
        for s in range(nseg):
            lo, hi = off, off + seg_nk[s]

            @pl.when((k >= lo) & (k < hi))
            def _(s=s):
                acc[...] += jnp.dot(lhs_refs[s][...], w, preferred_element_type=F32)

            off = hi

    def epilogue(rows):
        y = acc[rows, :]
        x = xin_ref[rows, :] if nxc == 1 else x_out[rows, :]
        if per_token:
            gate = modc_ref[2, rows, :]
        else:
            gate = modc_ref[2]
        xn = x + _gated_post(y, gpost_ref[...], gate, weight)
        x_out[rows, :] = xn
        if emit_h:
            if per_token:
                shift, scale = modn_ref[0, rows, :], modn_ref[1, rows, :]
            else:
                shift, scale = modn_ref[0], modn_ref[1]
            h_out[rows, :] = _pre_norm(xn, gpre_ref[...], shift, scale).astype(BF16)

    @pl.when(k == nk - 1)
    def _():
        if tm <= EPI_ROWS:
            epilogue(slice(None))
        else:
            def body(c, carry):
                epilogue(pl.ds(pl.multiple_of(c * EPI_ROWS, EPI_ROWS), EPI_ROWS))
                return carry
            lax.fori_loop(0, tm // EPI_ROWS, body, 0)


def _resid_proj(lhs_list, w, x, modc, gpost, modn, gpre, *, tm, rows_per_batch, weight):
    m, d = x.shape
    emit_h = modn is not None
    kdim = w.shape[0]
    nb = kdim // W_BLK
    seg_nk = tuple(lhs.shape[1] // TF for lhs in lhs_list)
    nk = sum(seg_nk)
    per_token = modc.shape[2] != 1
    nxc = 1
    if tm >= 512:
        nxc = 8 if nk >= 8 else 4

    in_specs = []
    off = 0
    for lhs, n in zip(lhs_list, seg_nk):
        in_specs.append(pl.BlockSpec(
            (tm, TF), lambda i, k, off=off, n=n: (i, jnp.clip(k - off, 0, n - 1))))
        off += n
    for q in range(W_STREAMS):
        in_specs.append(pl.BlockSpec(
            (W_BLK, d), lambda i, k, q=q: (jnp.minimum(W_STREAMS * k + q, nb - 1), 0)))
    if nxc > 1:
        in_specs.append(pl.BlockSpec(
            (tm // nxc, d), lambda i, k: (i * nxc + jnp.minimum(k, nxc - 1), 0)))
    else:
        in_specs.append(pl.BlockSpec((tm, d), lambda i, k: (i, 0)))
    in_specs.append(_mod_spec(modc, rows_per_batch, tm))
    in_specs.append(pl.BlockSpec((1, d), lambda i, k: (0, 0)))
    args = list(lhs_list) + [w] * W_STREAMS + [x, modc, gpost]
    out_specs = [pl.BlockSpec((tm, d), lambda i, k: (i, 0))]
    out_shape = [jax.ShapeDtypeStruct((m, d), F32)]
    if emit_h:
        in_specs.append(_mod_spec(modn, rows_per_batch, tm))
        in_specs.append(pl.BlockSpec((1, d), lambda i, k: (0, 0)))
        args += [modn, gpre]
        out_specs.append(pl.BlockSpec((tm, d), lambda i, k: (i, 0)))
        out_shape.append(jax.ShapeDtypeStruct((m, d), BF16))
    outs = pl.pallas_call(
        functools.partial(_resid_kernel, seg_nk=seg_nk, nxc=nxc, weight=weight,
                          emit_h=emit_h, tm=tm, per_token=per_token),
        grid=(m // tm, nk),
        in_specs=in_specs,
        out_specs=out_specs,
        out_shape=out_shape,
        scratch_shapes=[pltpu.VMEM((tm, d), F32)],
        compiler_params=_cparams(2, 56),
        name="resid_proj",
    )(*args)
    return (outs[0], outs[1]) if emit_h else (outs[0], None)


def _inproj_kernel(h_ref, w_ref, o_ref, w_s):
    @pl.when(pl.program_id(1) == 0)
    def _():
        w_s[...] = w_ref[...].astype(BF16)

    o_ref[...] = jnp.dot(h_ref[...], w_s[...], preferred_element_type=F32)


def _in_proj(h, w_in, *, tm):
    m, d = h.shape
    n = w_in.shape[1]
    tn = 768
    return pl.pallas_call(
        _inproj_kernel,
        grid=(n // tn, m // tm),
        in_specs=[
            pl.BlockSpec((tm, d), lambda j, i: (i, 0)),
            pl.BlockSpec((d, tn), lambda j, i: (0, j)),
        ],
        out_specs=pl.BlockSpec((tm, tn), lambda j, i: (i, j)),
        out_shape=jax.ShapeDtypeStruct((m, n), F32),
        scratch_shapes=[pltpu.VMEM((d, tn), BF16)],
        compiler_params=_cparams(2, 48),
        name="in_proj",
    )(h, w_in)


QBLK = 128
PREP_ROWS = 512
BLOCK_UNROLL_DILATED = 4
BLOCK_UNROLL_SHARED = 2
STAT_LANES = LANES // 4


def _attn_kernel(*refs, dils, shared, t, tail):
    pad = QBLK * max(dils)
    n_pat = len(dils)
    it = iter(refs)
    q_ref, k_ref, v_ref, cos_ref, slo_ref, shi_ref = [next(it) for _ in range(6)]
    sink_ref = next(it) if shared else None
    o_ref, kt_ref = next(it), next(it)
    qs = next(it)
    nvar = 2 if shared else 1
    kps = [next(it) for _ in range(nvar)]
    vps = [next(it) for _ in range(nvar)]
    num_refs = [next(it) for _ in range(n_pat)] if n_pat > 1 else []
    stat_refs = [next(it) for _ in range(n_pat)] if n_pat > 1 else []

    kv_head = pl.program_id(1) // 4
    lane = lax.broadcasted_iota(jnp.int32, (1, LANES), 1)
    low = lane < HEAD_DIM

    for r in kps + vps:
        r[0:pad, :] = jnp.zeros((pad, LANES), F32)

    def prep(c, carry):
        r0 = pl.multiple_of(c * PREP_ROWS, PREP_ROWS)
        rows = pl.ds(r0, PREP_ROWS)
        cos, slo, shi = cos_ref[rows, :], slo_ref[rows, :], shi_ref[rows, :]
        qs[rows, :] = _rope(q_ref[rows, :], cos, slo, shi) * ATTN_SCALE
        kr = _rope(k_ref[rows, :], cos, slo, shi)
        v = v_ref[rows, :]
        prow = pl.ds(pad + r0, PREP_ROWS)
        if shared:
            kroll = pltpu.roll(kr, HEAD_DIM, 1)
            vroll = pltpu.roll(v, HEAD_DIM, 1)
            for hh in range(2):
                kps[hh][prow, :] = jnp.where(kv_head == hh, kr, kroll)
                vps[hh][prow, :] = jnp.where(kv_head == hh, v, vroll)
        else:
            kps[0][prow, :] = kr
            vps[0][prow, :] = v
        return carry

    lax.fori_loop(0, t // PREP_ROWS, prep, 0)

    trow = pl.ds(pad + t - tail, tail)
    if shared:
        kt_ref[...] = jnp.where(kv_head == 0, kps[0][trow, :], kps[1][trow, :])
    else:
        kt_ref[...] = kps[0][trow, :]

    row = lax.broadcasted_iota(jnp.int32, (QBLK, 2 * QBLK), 0)
    col = lax.broadcasted_iota(jnp.int32, (QBLK, 2 * QBLK), 1)
    band = (col >= row) & (col <= row + HOPS)
    current = col >= QBLK
    nt = (((1,), (1,)), ((), ()))

    for pi, d in enumerate(dils):
        shift = d.bit_length() - 1

        def block(b, carry, d=d, shift=shift, pi=pi):
            n = lax.shift_right_logical(b, jnp.int32(shift))
            q0 = n * (QBLK * d) + (b & (d - 1))
            k0 = q0 + pad - QBLK * d
            if d == 1:
                qsl = pl.ds(pl.multiple_of(q0, QBLK), QBLK)
                ksl = pl.ds(pl.multiple_of(k0, QBLK), 2 * QBLK)
            else:
                qsl = pl.ds(q0, QBLK, stride=d)
                ksl = pl.ds(k0, 2 * QBLK, stride=d)
            q = qs[qsl, :]
            valid = band & (current | (n > 0))
            kb = [r[ksl, :].astype(BF16) for r in kps]
            vb = [r[ksl, :].astype(BF16) for r in vps]
            parts = []
            for hh in range(2):
                var = hh if shared else 0
                qh = jnp.where(low if hh == 0 else jnp.logical_not(low), q, 0.0).astype(BF16)
                s = lax.dot_general(qh, kb[var], nt, preferred_element_type=F32)
                s = jnp.where(valid, s, NEG_INF)
                mb = jnp.max(s, axis=1, keepdims=True)
                p = jnp.exp(s - mb)
                den = jnp.sum(p, axis=1, keepdims=True)
                if n_pat == 1 and shared:
                    den = den + jnp.exp(sink_ref[:, hh * HEAD_DIM:hh * HEAD_DIM + 1] - mb)
                parts.append((mb, den,
                              jnp.dot(p.astype(BF16), vb[var], preferred_element_type=F32)))
            pv = jnp.where(low, parts[0][2], parts[1][2])
            if n_pat == 1:
                o_ref[qsl, :] = (pv / jnp.where(low, parts[0][1], parts[1][1])).astype(BF16)
            else:
                num_refs[pi][qsl, :] = pv
                stat_refs[pi][qsl, :] = jnp.where(
                    lane < STAT_LANES, parts[0][0],
                    jnp.where(low, parts[0][1],
                              jnp.where(lane < 3 * STAT_LANES, parts[1][0], parts[1][1])))
            return carry

        lax.fori_loop(0, t // QBLK, block, 0,
                      unroll=BLOCK_UNROLL_SHARED if shared else BLOCK_UNROLL_DILATED)

    if n_pat > 1:
        is_max = (lane & (HEAD_DIM - 1)) < STAT_LANES

        def finish(c, carry):
            rows = pl.ds(pl.multiple_of(c * PREP_ROWS, PREP_ROWS), PREP_ROWS)
            ms, ls = [], []
            for r in stat_refs:
                st = r[rows, :]
                ms.append(jnp.where(is_max, st, pltpu.roll(st, STAT_LANES, 1)))
                ls.append(jnp.where(is_max, pltpu.roll(st, LANES - STAT_LANES, 1), st))
            m = functools.reduce(jnp.maximum, ms)
            den = num = None
            for mi, li, r in zip(ms, ls, num_refs):
                w = jnp.exp(mi - m)
                den = w * li if den is None else den + w * li
                num = w * r[rows, :] if num is None else num + w * r[rows, :]
            o_ref[rows, :] = (num / den).astype(BF16)
            return carry

        lax.fori_loop(0, t // PREP_ROWS, finish, 0)


def _attention(u, tables, sinks, *, n_batch, t, dils, q_slab, k_slab, v_slab, n_slab, shared,
               tail):
    pad = QBLK * max(dils)
    cos, slo, shi = tables
    const = lambda b, s: (0, 0)
    once = pl.Buffered(1)
    in_specs = [
        pl.BlockSpec((t, LANES), lambda b, s: (b, q_slab + s)),
        pl.BlockSpec((t, LANES), (lambda b, s: (b, k_slab)) if shared
                     else (lambda b, s: (b, k_slab + s))),
        pl.BlockSpec((t, LANES), (lambda b, s: (b, v_slab)) if shared
                     else (lambda b, s: (b, v_slab + s))),
        pl.BlockSpec((t, LANES), const, pipeline_mode=once),
        pl.BlockSpec((t, LANES), const, pipeline_mode=once),
        pl.BlockSpec((t, LANES), const, pipeline_mode=once),
    ]
    args = [u, u, u, cos, slo, shi]
    if shared:
        in_specs.append(pl.BlockSpec((None, 1, LANES), lambda b, s: (s, 0, 0)))
        args.append(sinks)
    n_kslab = 1 if shared else n_slab
    nvar = 2 if shared else 1
    scratch = [pltpu.VMEM((t, LANES), F32)]
    scratch += [pltpu.VMEM((pad + t, LANES), F32) for _ in range(2 * nvar)]
    if len(dils) > 1:
        scratch += [pltpu.VMEM((t, LANES), F32) for _ in range(2 * len(dils))]
    return pl.pallas_call(
        functools.partial(_attn_kernel, dils=dils, shared=shared, t=t, tail=tail),
        grid=(n_batch, n_slab),
        in_specs=in_specs,
        out_specs=[
            pl.BlockSpec((t, LANES), lambda b, s: (b, s)),
            pl.BlockSpec((tail, LANES), (lambda b, s: (b, 0)) if shared
                         else (lambda b, s: (b, s))),
        ],
        out_shape=[
            jax.ShapeDtypeStruct((n_batch * t, n_slab * LANES), BF16),
            jax.ShapeDtypeStruct((n_batch * tail, n_kslab * LANES), F32),
        ],
        scratch_shapes=scratch,
        compiler_params=_cparams(2, 48),
        name="attn_shared" if shared else "attn_dilated",
    )(*args)


CONV_TB = 256
CONV_HALO = 32
CONV_RC = 32


def _layer_norm_swish(y, g, b):
    mu = jnp.mean(y, axis=-1, keepdims=True)
    yc = y - mu
    var = jnp.mean(yc * yc, axis=-1, keepdims=True)
    z = yc * lax.rsqrt(var + LN_EPS) * g + b
    return z * _sigmoid(z)


def _conv_kernel(v0, v1, g0, g1, cw_ref, cb_ref, lg_ref, lb_ref, o_ref, st_ref, ext):
    tt = pl.program_id(1)
    half = D_CONV // 2

    @pl.when(tt == 0)
    def _():
        ext[0:CONV_HALO, :] = jnp.zeros((CONV_HALO, D_CONV), F32)

    @pl.when(tt > 0)
    def _():
        ext[0:CONV_HALO, :] = ext[CONV_TB:CONV_TB + CONV_HALO, :]

    ext[CONV_HALO:CONV_HALO + CONV_TB, 0:half] = v0[...] * _sigmoid(g0[...])
    ext[CONV_HALO:CONV_HALO + CONV_TB, half:D_CONV] = v1[...] * _sigmoid(g1[...])
    st_ref[...] = ext[CONV_TB:CONV_TB + CONV_HALO, :]

    lead = CONV_HALO - (CONV_WIDTH - 1)
    for c in range(CONV_TB // CONV_RC):
        r0 = c * CONV_RC
        y = jnp.zeros((CONV_RC, D_CONV), F32) + cb_ref[...]
        for w in range(CONV_WIDTH):
            y = y + ext[r0 + lead + w:r0 + lead + w + CONV_RC, :] * cw_ref[w:w + 1, :]
        o_ref[r0:r0 + CONV_RC, :] = _layer_norm_swish(y, lg_ref[...], lb_ref[...]).astype(BF16)


def _conv_module(u, cw, cb, lg, lb, *, n_batch, t):
    nt = t // CONV_TB
    half = D_CONV // 2
    v_blk, g_blk = COL_VAL // half, COL_GATE // half
    row = lambda b, i: b * nt + i
    vec = pl.BlockSpec((1, D_CONV), lambda b, i: (0, 0))
    return pl.pallas_call(
        _conv_kernel,
        grid=(n_batch, nt),
        in_specs=[
            pl.BlockSpec((CONV_TB, half), lambda b, i: (row(b, i), v_blk)),
            pl.BlockSpec((CONV_TB, half), lambda b, i: (row(b, i), v_blk + 1)),
            pl.BlockSpec((CONV_TB, half), lambda b, i: (row(b, i), g_blk)),
            pl.BlockSpec((CONV_TB, half), lambda b, i: (row(b, i), g_blk + 1)),
            pl.BlockSpec((CONV_HALO, D_CONV), lambda b, i: (0, 0)),
            vec, vec, vec,
        ],
        out_specs=[
            pl.BlockSpec((CONV_TB, D_CONV), lambda b, i: (row(b, i), 0)),
            pl.BlockSpec((CONV_HALO, D_CONV), lambda b, i: (b, 0)),
        ],
        out_shape=[
            jax.ShapeDtypeStruct((n_batch * t, D_CONV), BF16),
            jax.ShapeDtypeStruct((n_batch * CONV_HALO, D_CONV), F32),
        ],
        scratch_shapes=[pltpu.VMEM((CONV_HALO + CONV_TB, D_CONV), F32)],
        compiler_params=_cparams(2, 32),
        name="conv_module",
    )(u, u, u, u, cw, cb, lg, lb)


ROPE_SLABS = tuple(range(SLAB_QA, SLAB_VA)) + tuple(range(SLAB_QB, SLAB_VB))
Q_SLABS = tuple(range(SLAB_QA, SLAB_KA)) + tuple(range(SLAB_QB, SLAB_KB))
VAL_SLABS = tuple(range(COL_VAL // LANES, COL_GATE // LANES))


def _sample_post_kernel(u_ref, cos_ref, slo_ref, shi_ref, o_ref):
    for s in range(N_IN // LANES):
        x = u_ref[:, s * LANES:(s + 1) * LANES]
        if s in ROPE_SLABS:
            x = _rope(x, cos_ref[...], slo_ref[...], shi_ref[...])
            if s in Q_SLABS:
                x = x * ATTN_SCALE
        elif s in VAL_SLABS:
            gs = s + D_CONV // LANES
            x = x * _sigmoid(u_ref[:, gs * LANES:(gs + 1) * LANES])
        o_ref[:, s * LANES:(s + 1) * LANES] = x


def _sample_post(u, tables):
    m, n = u.shape
    full = pl.BlockSpec((m, n), lambda i: (0, 0))
    vec = pl.BlockSpec((1, LANES), lambda i: (0, 0))
    return pl.pallas_call(
        _sample_post_kernel,
        grid=(1,),
        in_specs=[full, vec, vec, vec],
        out_specs=full,
        out_shape=jax.ShapeDtypeStruct((m, n), F32),
        compiler_params=_cparams(1, 16),
        name="sample_post",
    )(u, *tables)


def _sattn_a_kernel(k16, v16, k4, v4, k1, v1, q_ref, kn_ref, vn_ref, o_ref):
    nkeys = HOPS * N_HEADS_A
    nt = (((1,), (1,)), ((), ()))
    q = q_ref[...]
    qb = q.astype(BF16)
    head = lax.broadcasted_iota(jnp.int32, (N_HEADS_A, nkeys), 0)
    colh = lax.broadcasted_iota(jnp.int32, (N_HEADS_A, nkeys), 1) & (N_HEADS_A - 1)
    own = head == colh
    scores = []
    for kr in (k16, k4, k1):
        k2 = kr[...].reshape(nkeys, HEAD_DIM).astype(BF16)
        scores.append(jnp.where(own, lax.dot_general(qb, k2, nt, preferred_element_type=F32),
                                NEG_INF))
    m = jnp.sum(q * kn_ref[...], axis=1, keepdims=True)
    s_new = m
    for s in scores:
        m = jnp.maximum(m, jnp.max(s, axis=1, keepdims=True))
    den = jnp.exp(s_new - m) * float(len(DILATIONS_A))
    num = den * vn_ref[...]
    for s, vr in zip(scores, (v16, v4, v1)):
        p = jnp.exp(s - m)
        den = den + jnp.sum(p, axis=1, keepdims=True)
        num = num + jnp.dot(p.astype(BF16), vr[...].reshape(nkeys, HEAD_DIM).astype(BF16),
                            preferred_element_type=F32)
    o_ref[...] = num / den


def _sample_attn_a(cache_k, cache_v, layer, q, kn, vn):
    depth, nb, la, nh, e = cache_k.shape
    blk = (HOPS, nh, e)

    def views(c):
        return (c.reshape(depth, nb, la // 16, 16, nh, e),
                c.reshape(depth, nb, la // 4, 4, nh, e), c)

    specs16 = pl.BlockSpec((None, None, HOPS, None, nh, e), lambda b: (layer, b, 0, 0, 0, 0))
    specs4 = pl.BlockSpec((None, None, HOPS, None, nh, e),
                          lambda b: (layer, b, la // 4 // HOPS - 1, 0, 0, 0))
    specs1 = pl.BlockSpec((None, None) + blk, lambda b: (layer, b, la // HOPS - 1, 0, 0))
    tok = pl.BlockSpec((None, nh, e), lambda b: (b, 0, 0))
    k16, k4, k1 = views(cache_k)
    v16, v4, v1 = views(cache_v)
    return pl.pallas_call(
        _sattn_a_kernel,
        grid=(nb,),
        in_specs=[specs16, specs16, specs4, specs4, specs1, specs1, tok, tok, tok],
        out_specs=tok,
        out_shape=jax.ShapeDtypeStruct((nb, nh, e), F32),
        compiler_params=_cparams(1, 32),
        name="sample_attn_dilated",
    )(k16, v16, k4, v4, k1, v1, q, kn, vn)


def _sattn_b_kernel(kc_ref, vc_ref, q_ref, kn_ref, vn_ref, sink_ref, o_ref):
    nt = (((1,), (1,)), ((), ()))
    q = q_ref[...]
    s = lax.dot_general(q.astype(BF16), kc_ref[...].astype(BF16), nt, preferred_element_type=F32)
    s_new = jnp.sum(q * kn_ref[...], axis=1, keepdims=True)
    m = jnp.maximum(jnp.max(s, axis=1, keepdims=True), s_new)
    p = jnp.exp(s - m)
    p_new = jnp.exp(s_new - m)
    den = jnp.sum(p, axis=1, keepdims=True) + p_new + jnp.exp(sink_ref[...] - m)
    num = jnp.dot(p.astype(BF16), vc_ref[...].astype(BF16), preferred_element_type=F32)
    o_ref[...] = (num + p_new * vn_ref[...]) / den


def _sample_attn_b(kc, vc, layer, q_exp, kn, vn, sinks):
    _, nb, lb, w = kc.shape
    nh = q_exp.shape[1]
    cache = pl.BlockSpec((None, None, lb, w), lambda b: (layer, b, 0, 0))
    tokq = pl.BlockSpec((None, nh, w), lambda b: (b, 0, 0))
    tok1 = pl.BlockSpec((None, 1, w), lambda b: (b, 0, 0))
    return pl.pallas_call(
        _sattn_b_kernel,
        grid=(nb,),
        in_specs=[cache, cache, tokq, tok1, tok1, pl.BlockSpec((nh, 1), lambda b: (0, 0))],
        out_specs=tokq,
        out_shape=jax.ShapeDtypeStruct((nb, nh, w), F32),
        compiler_params=_cparams(1, 16),
        name="sample_attn_shared",
    )(kc, vc, q_exp, kn, vn, sinks)


def _sconv_kernel(st_ref, glu_ref, cw_ref, cb_ref, lg_ref, lb_ref, o_ref, ns_ref):
    hist = CONV_WIDTH - 1
    glu = glu_ref[...]
    y = glu * cw_ref[hist:hist + 1, :] + cb_ref[...]
    for w in range(hist):
        y = y + st_ref[:, w * D_CONV:(w + 1) * D_CONV] * cw_ref[w:w + 1, :]
    o_ref[...] = _layer_norm_swish(y, lg_ref[...], lb_ref[...]).astype(BF16)
    ns_ref[:, 0:(hist - 1) * D_CONV] = st_ref[:, D_CONV:hist * D_CONV]
    ns_ref[:, (hist - 1) * D_CONV:hist * D_CONV] = glu


def _sample_conv(state2d, layer, glu, cw, cb, lg, lb):
    _, nb, width = state2d.shape
    st = pl.BlockSpec((None, nb, width), lambda i: (layer, 0, 0))
    tok = pl.BlockSpec((nb, D_CONV), lambda i: (0, 0))
    vec = pl.BlockSpec((1, D_CONV), lambda i: (0, 0))
    return pl.pallas_call(
        _sconv_kernel,
        grid=(1,),
        in_specs=[st, tok, pl.BlockSpec((CONV_HALO, D_CONV), lambda i: (0, 0)), vec, vec, vec],
        out_specs=[tok, pl.BlockSpec((nb, width), lambda i: (0, 0))],
        out_shape=[jax.ShapeDtypeStruct((nb, D_CONV), BF16),
                   jax.ShapeDtypeStruct((nb, width), F32)],
        compiler_params=_cparams(1, 16),
        name="sample_conv",
    )(state2d, glu, cw, cb, lg, lb)


def _shift_kernel(km, kx, kn, vm, vx, vn, ko, vo):
    last = pl.program_id(2) == pl.num_programs(2) - 1
    for main, nxt, new, out in ((km, kx, kn, ko), (vm, vx, vn, vo)):
        rb = main.shape[1]
        out[:, 0:rb - 1] = main[:, 1:rb]
        out[:, rb - 1:rb] = jnp.where(last, new[...], nxt[...])


def _shift_caches(cache_k, cache_v, new_k, new_v, *, batch_chunk, row_block):
    depth, nb, rows, nh, e = cache_k.shape
    main = pl.BlockSpec((None, batch_chunk, row_block, nh, e), lambda l, b, r: (l, b, r, 0, 0))
    nxt = pl.BlockSpec((None, batch_chunk, 1, nh, e),
                       lambda l, b, r: (l, b, jnp.minimum((r + 1) * row_block, rows - 1), 0, 0))
    new = pl.BlockSpec((None, batch_chunk, 1, nh, e), lambda l, b, r: (l, b, 0, 0, 0))
    shape = jax.ShapeDtypeStruct(cache_k.shape, cache_k.dtype)
    return pl.pallas_call(
        _shift_kernel,
        grid=(depth, nb // batch_chunk, rows // row_block),
        in_specs=[main, nxt, new, main, nxt, new],
        out_specs=[main, main],
        out_shape=[shape, shape],
        compiler_params=_cparams(3, 48),
        name="cache_shift",
    )(cache_k, cache_k, new_k, cache_v, cache_v, new_v)


def _rope_tables(pos):
    n = pos.shape[0]
    inv_freq = 1.0 / (ROPE_THETA ** (jnp.arange(ROT_HALF, dtype=F32) / ROT_HALF))
    ang = pos.astype(F32)[:, None] * inv_freq[None, :]
    cos, sin = jnp.cos(ang), jnp.sin(ang)
    rest = HEAD_DIM - 2 * ROT_HALF
    c = jnp.concatenate([cos, cos, jnp.ones((n, rest), F32)], axis=1)
    lo = jnp.concatenate([-sin, jnp.zeros((n, HEAD_DIM - ROT_HALF), F32)], axis=1)
    hi = jnp.concatenate([jnp.zeros((n, ROT_HALF), F32), sin, jnp.zeros((n, rest), F32)], axis=1)
    return tuple(jnp.tile(x, (1, LANES // HEAD_DIM)) for x in (c, lo, hi))


def kernel(x_prompt, x_sample, c_prompt, c_sample, cache_a_k, cache_a_v, cache_b_k, cache_b_v,
           state_c_conv, w_ada, b_ada, g_pre, g_post, w_ffn_gu, w_ffn_down, w_in, w_out,
           attn_sinks, conv_w, conv_b, conv_ln_g, conv_ln_b):
    nbp, t, d = x_prompt.shape
    nbs = x_sample.shape[0]
    mp = nbp * t
    depth = w_ada.shape[0]
    la, lb = cache_a_k.shape[2], cache_b_k.shape[2]
    hist = CONV_WIDTH - 1
    tm_p = 1024

    xp = x_prompt.reshape(mp, d)
    xs = x_sample.reshape(nbs, d)

    s_row = 32
    c_all = jnp.zeros((s_row + nbs, d), F32).at[:nbp].set(c_prompt).at[s_row:].set(c_sample)
    mod_all = _ada_mod(c_all, w_ada, b_ada)

    def mod_p(l, s):
        return mod_all[l, :nbp, 3 * s * d:3 * (s + 1) * d].reshape(nbp, 3, 1, d)

    def mod_s(l, s):
        m = mod_all[l, s_row:, 3 * s * d:3 * (s + 1) * d].reshape(nbs, 3, d)
        return jnp.transpose(m, (1, 0, 2))[None]

    def nxt(l, s):
        return (l, s + 1) if s + 1 < N_SUB else (l + 1, 0)

    tab_p = _rope_tables(jnp.arange(t, dtype=jnp.int32))
    tab_s = _rope_tables(PAST_LEN + jnp.arange(1, dtype=jnp.int32))
    sink_slabs = jnp.repeat(attn_sinks, HEAD_DIM, axis=1).reshape(depth, N_HEADS_B // 2, 1, LANES)
    sink_cols = attn_sinks.reshape(depth, N_HEADS_B, 1)
    cw_pad = jnp.pad(conv_w, ((0, 0), (0, CONV_HALO - CONV_WIDTH), (0, 0)))
    vec = lambda a, l: a[l].reshape(1, -1)

    cache_bk2 = cache_b_k.reshape(depth, nbs, lb, N_KV_B * HEAD_DIM)
    cache_bv2 = cache_b_v.reshape(depth, nbs, lb, N_KV_B * HEAD_DIM)
    state2d = state_c_conv.reshape(depth, nbs, hist * D_CONV)
    kv_of_head = (jnp.arange(N_HEADS_B) // (N_HEADS_B // N_KV_B))[:, None] == jnp.arange(N_KV_B)

    hp = _prenorm(xp, mod_p(0, 0), vec(g_pre[0], 0), tm=tm_p, rows_per_batch=t)
    hs = _prenorm(xs, mod_s(0, 0), vec(g_pre[0], 0), tm=nbs, rows_per_batch=nbs)

    st_p = [[] for _ in range(5)]
    new_rows = [[] for _ in range(4)]
    st_c_s = []

    def resid(lhs, w, x, l, s, weight, mod_fn, tm, rpb):
        ln, sn = nxt(l, s)
        last = ln >= depth
        return _resid_proj(
            lhs, w, x, mod_fn(l, s), vec(g_post[l], s),
            None if last else mod_fn(ln, sn), None if last else vec(g_pre[ln], sn),
            tm=tm, rows_per_batch=rpb, weight=weight)

    for l in range(depth):
        a = _ffn_up(hp, w_ffn_gu[l, 0], tm=tm_p)
        xp, hp = resid([a], w_ffn_down[l, 0], xp, l, 0, 0.5, mod_p, tm_p, t)
        u = _in_proj(hp, w_in[l], tm=tm_p)
        oa, kta = _attention(u, tab_p, None, n_batch=nbp, t=t, dils=DILATIONS_A,
                             q_slab=SLAB_QA, k_slab=SLAB_KA, v_slab=SLAB_VA,
                             n_slab=N_HEADS_A // 2, shared=False, tail=min(la, t))
        ob, ktb = _attention(u, tab_p, sink_slabs[l], n_batch=nbp, t=t, dils=(1,),
                             q_slab=SLAB_QB, k_slab=SLAB_KB, v_slab=SLAB_VB,
                             n_slab=N_HEADS_B // 2, shared=True, tail=min(lb, t))
        oc, cst = _conv_module(u, cw_pad[l], vec(conv_b, l), vec(conv_ln_g, l), vec(conv_ln_b, l),
                               n_batch=nbp, t=t)
        xp, hp = resid([oa, ob, oc], w_out[l], xp, l, 1, 1.0, mod_p, tm_p, t)
        a = _ffn_up(hp, w_ffn_gu[l, 1], tm=tm_p)
        xp, hp = resid([a], w_ffn_down[l, 1], xp, l, 2, 0.5, mod_p, tm_p, t)

        u3 = u.reshape(nbp, t, N_IN)
        ta, tb = min(la, t), min(lb, t)
        st_p[0].append(kta.reshape(nbp, ta, N_HEADS_A, HEAD_DIM))
        st_p[1].append(u3[:, t - ta:, SLAB_VA * LANES:SLAB_QB * LANES]
                       .reshape(nbp, ta, N_HEADS_A, HEAD_DIM))
        st_p[2].append(ktb.reshape(nbp, tb, N_KV_B, HEAD_DIM))
        st_p[3].append(u3[:, t - tb:, SLAB_VB * LANES:COL_VAL].reshape(nbp, tb, N_KV_B, HEAD_DIM))
        st_p[4].append(cst.reshape(nbp, CONV_HALO, D_CONV)[:, CONV_HALO - hist:])

        a = _ffn_up(hs, w_ffn_gu[l, 0], tm=nbs)
        xs, hs = resid([a], w_ffn_down[l, 0], xs, l, 0, 0.5, mod_s, nbs, nbs)
        ur = _sample_post(_in_proj(hs, w_in[l], tm=nbs), tab_s)
        seg = lambda lo, hi: ur[:, lo * LANES:hi * LANES]
        qa = seg(SLAB_QA, SLAB_KA).reshape(nbs, N_HEADS_A, HEAD_DIM)
        kna = seg(SLAB_KA, SLAB_VA).reshape(nbs, N_HEADS_A, HEAD_DIM)
        vna = seg(SLAB_VA, SLAB_QB).reshape(nbs, N_HEADS_A, HEAD_DIM)
        oa = _sample_attn_a(cache_a_k, cache_a_v, l, qa, kna, vna)
        qb = seg(SLAB_QB, SLAB_KB).reshape(nbs, N_HEADS_B, 1, HEAD_DIM)
        q_exp = jnp.where(kv_of_head[None, :, :, None], qb, 0.0).reshape(nbs, N_HEADS_B, LANES)
        knb = seg(SLAB_KB, SLAB_VB)
        vnb = seg(SLAB_VB, SLAB_VB + 1)
        ob2 = _sample_attn_b(cache_bk2, cache_bv2, l, q_exp, knb.reshape(nbs, 1, LANES),
                             vnb.reshape(nbs, 1, LANES), sink_cols[l])
        ob = jnp.sum(jnp.where(kv_of_head[None, :, :, None],
                               ob2.reshape(nbs, N_HEADS_B, N_KV_B, HEAD_DIM), 0.0), axis=2)
        glu = ur[:, COL_VAL:COL_GATE]
        oc, ns = _sample_conv(state2d, l, glu, cw_pad[l], vec(conv_b, l), vec(conv_ln_g, l),
                              vec(conv_ln_b, l))
        mix = [oa.reshape(nbs, -1).astype(BF16), ob.reshape(nbs, -1).astype(BF16), oc]
        xs, hs = resid(mix, w_out[l], xs, l, 1, 1.0, mod_s, nbs, nbs)
        a = _ffn_up(hs, w_ffn_gu[l, 1], tm=nbs)
        xs, hs = resid([a], w_ffn_down[l, 1], xs, l, 2, 0.5, mod_s, nbs, nbs)

        new_rows[0].append(kna.reshape(nbs, 1, N_HEADS_A, HEAD_DIM))
        new_rows[1].append(vna.reshape(nbs, 1, N_HEADS_A, HEAD_DIM))
        new_rows[2].append(knb.reshape(nbs, 1, N_KV_B, HEAD_DIM))
        new_rows[3].append(vnb.reshape(nbs, 1, N_KV_B, HEAD_DIM))
        st_c_s.append(ns.reshape(nbs, hist, D_CONV))

    new_rows = [jnp.stack(r) for r in new_rows]
    shift_a = _shift_caches(cache_a_k, cache_a_v, new_rows[0], new_rows[1],
                            batch_chunk=1, row_block=la // 2)
    shift_b = _shift_caches(cache_b_k, cache_b_v, new_rows[2], new_rows[3],
                            batch_chunk=8, row_block=lb)
    return (xp.reshape(nbp, t, d), xs.reshape(nbs, 1, d),
            *[jnp.stack(s) for s in st_p], *shift_a, *shift_b, jnp.stack(st_c_s))
```

```python
import functools

import jax
import jax.numpy as jnp
from jax import lax
from jax.experimental import pallas as pl
from jax.experimental.pallas import tpu as pltpu

F32 = jnp.float32
BF16 = jnp.bfloat16

D_MODEL = 2048
DEPTH = 4
HEAD_DIM = 64
N_HEADS_A = 8
N_HEADS_B = 16
N_KV_B = 2
D_CONV = 512
DILATIONS_A = (1, 4, 16)
HOPS = 128
CONV_WIDTH = 31
ROT_HALF = 8
ROPE_THETA = 500000.0
PAST_LEN = 16384
RMS_EPS = 1e-6
LN_EPS = 1e-5
NEG_INF = -1e30
ATTN_SCALE = HEAD_DIM ** -0.5
N_SUB = 3

LANES = 128
MIB = 1 << 20

SLAB_QA, SLAB_KA, SLAB_VA = 0, 4, 8
SLAB_QB, SLAB_KB, SLAB_VB = 12, 20, 21
COL_VAL, COL_GATE = 2816, 3328
N_IN = 3840


def _cparams(n_axes, vmem_mib):
    return pltpu.CompilerParams(
        dimension_semantics=("arbitrary",) * n_axes,
        vmem_limit_bytes=vmem_mib * MIB,
    )


def _sigmoid(x):
    return jax.nn.sigmoid(x)


def _pre_norm(x, g, shift, scale):
    ms = jnp.mean(x * x, axis=-1, keepdims=True)
    return (x * lax.rsqrt(ms + RMS_EPS)) * g * (1.0 + scale) + shift


def _gated_post(y, g, gate, weight):
    ms = jnp.mean(y * y, axis=-1, keepdims=True)
    return (weight * gate) * ((y * lax.rsqrt(ms + RMS_EPS)) * g)


def _rope(x, cos, sin_lo, sin_hi):
    return x * cos + pltpu.roll(x, LANES - ROT_HALF, 1) * sin_lo + pltpu.roll(x, ROT_HALF, 1) * sin_hi


def _ada_kernel(c_ref, w_ref, b_ref, o_ref):
    c = c_ref[...]
    a = (c * _sigmoid(c)).astype(BF16)
    o_ref[...] = jnp.dot(a, w_ref[...].astype(BF16), preferred_element_type=F32) + b_ref[...]


def _ada_mod(c_all, w_ada, b_ada):
    depth, d, n = w_ada.shape
    rows = c_all.shape[0]
    tn = 1024
    return pl.pallas_call(
        _ada_kernel,
        grid=(depth, n // tn),
        in_specs=[
            pl.BlockSpec((rows, d), lambda l, j: (0, 0)),
            pl.BlockSpec((None, d, tn), lambda l, j: (l, 0, j)),
            pl.BlockSpec((None, 1, tn), lambda l, j: (l, 0, j)),
        ],
        out_specs=pl.BlockSpec((None, rows, tn), lambda l, j: (l, 0, j)),
        out_shape=jax.ShapeDtypeStruct((depth, rows, n), F32),
        compiler_params=_cparams(2, 40),
        name="ada_mod",
    )(c_all, w_ada, b_ada.reshape(depth, 1, n))


def _prenorm_kernel(x_ref, mod_ref, g_ref, h_ref):
    h_ref[...] = _pre_norm(x_ref[...], g_ref[...], mod_ref[0], mod_ref[1]).astype(BF16)


def _mod_spec(mod, rows_per_batch, tm):
    _, _, r, d = mod.shape
    if r == 1:
        per = rows_per_batch // tm
        return pl.BlockSpec((None, 3, 1, d), lambda i, *_: (i // per, 0, 0, 0))
    return pl.BlockSpec((None, 3, r, d), lambda i, *_: (0, 0, 0, 0))


def _prenorm(x, mod, g, *, tm, rows_per_batch):
    m, d = x.shape
    return pl.pallas_call(
        _prenorm_kernel,
        grid=(m // tm,),
        in_specs=[
            pl.BlockSpec((tm, d), lambda i: (i, 0)),
            _mod_spec(mod, rows_per_batch, tm),
            pl.BlockSpec((1, d), lambda i: (0, 0)),
        ],
        out_specs=pl.BlockSpec((tm, d), lambda i: (i, 0)),
        out_shape=jax.ShapeDtypeStruct((m, d), BF16),
        compiler_params=_cparams(1, 32),
        name="prenorm",
    )(x, mod, g)


W_BLK = 128
W_STREAMS = 4
TF = W_BLK * W_STREAMS


def _ffn_up_kernel(h_ref, *refs, f_valid):
    g_refs = refs[0:W_STREAMS]
    u_refs = refs[W_STREAMS:2 * W_STREAMS]
    a_ref, wg_s, wu_s = refs[2 * W_STREAMS:]
    j = pl.program_id(0)

    @pl.when(pl.program_id(1) == 0)
    def _():
        for q in range(W_STREAMS):
            wg_s[:, q * W_BLK:(q + 1) * W_BLK] = g_refs[q][...].astype(BF16)
            wu_s[:, q * W_BLK:(q + 1) * W_BLK] = u_refs[q][...].astype(BF16)

    h = h_ref[...]
    g = jnp.dot(h, wg_s[...], preferred_element_type=F32)
    u = jnp.dot(h, wu_s[...], preferred_element_type=F32)
    a = (g * _sigmoid(g)) * u
    col = lax.broadcasted_iota(jnp.int32, a.shape, 1)
    a_ref[...] = jnp.where(col < f_valid - j * TF, a, 0.0).astype(BF16)


def _ffn_up(h, w_gu, *, tm):
    m, d = h.shape
    f = w_gu.shape[1] // 2
    nb = f // W_BLK
    nj = pl.cdiv(f, TF)

    def wspec(base, q):
        return pl.BlockSpec(
            (d, W_BLK), lambda j, i: (0, base + jnp.minimum(W_STREAMS * j + q, nb - 1)))

    in_specs = [pl.BlockSpec((tm, d), lambda j, i: (i, 0))]
    in_specs += [wspec(0, q) for q in range(W_STREAMS)]
    in_specs += [wspec(nb, q) for q in range(W_STREAMS)]
    return pl.pallas_call(
        functools.partial(_ffn_up_kernel, f_valid=f),
        grid=(nj, m // tm),
        in_specs=in_specs,
        out_specs=pl.BlockSpec((tm, TF), lambda j, i: (i, j)),
        out_shape=jax.ShapeDtypeStruct((m, nj * TF), BF16),
        scratch_shapes=[pltpu.VMEM((d, TF), BF16), pltpu.VMEM((d, TF), BF16)],
        compiler_params=_cparams(2, 48),
        name="ffn_up",
    )(h, *([w_gu] * (2 * W_STREAMS)))


EPI_ROWS = 256


def _resid_kernel(*refs, seg_nk, nxc, weight, emit_h, tm, per_token):
    nseg = len(seg_nk)
    it = iter(refs)
    lhs_refs = [next(it) for _ in range(nseg)]
    w_refs = [next(it) for _ in range(W_STREAMS)]
    xin_ref, modc_ref, gpost_ref = next(it), next(it), next(it)
    modn_ref = gpre_ref = None
    if emit_h:
        modn_ref, gpre_ref = next(it), next(it)
    x_out = next(it)
    h_out = next(it) if emit_h else None
    acc = next(it)
    k = pl.program_id(1)
    nk = sum(seg_nk)

    w = jnp.concatenate([r[...].astype(BF16) for r in w_refs], axis=0)

    @pl.when(k == 0)
    def _():
        acc[...] = jnp.zeros_like(acc)

    if nxc > 1:
        xr = tm // nxc

        @pl.when(k < nxc)
        def _():
            x_out[pl.ds(pl.multiple_of(k * xr, xr), xr), :] = xin_ref[...]

    if nseg == 1:
        acc[...] += jnp.dot(lhs_refs[0][...], w, preferred_element_type=F32)
    else:
        off = 0
        for s in range(nseg):
            lo, hi = off, off + seg_nk[s]

            @pl.when((k >= lo) & (k < hi))
            def _(s=s):
                acc[...] += jnp.dot(lhs_refs[s][...], w, preferred_element_type=F32)

            off = hi

    def epilogue(rows):
        y = acc[rows, :]
        x = xin_ref[rows, :] if nxc == 1 else x_out[rows, :]
        if per_token:
            gate = modc_ref[2, rows, :]
        else:
            gate = modc_ref[2]
        xn = x + _gated_post(y, gpost_ref[...], gate, weight)
        x_out[rows, :] = xn
        if emit_h:
            if per_token:
                shift, scale = modn_ref[0, rows, :], modn_ref[1, rows, :]
            else:
                shift, scale = modn_ref[0], modn_ref[1]
            h_out[rows, :] = _pre_norm(xn, gpre_ref[...], shift, scale).astype(BF16)

    @pl.when(k == nk - 1)
    def _():
        if tm <= EPI_ROWS:
            epilogue(slice(None))
        else:
            def body(c, carry):
                epilogue(pl.ds(pl.multiple_of(c * EPI_ROWS, EPI_ROWS), EPI_ROWS))
                return carry
            lax.fori_loop(0, tm // EPI_ROWS, body, 0)


def _resid_proj(lhs_list, w, x, modc, gpost, modn, gpre, *, tm, rows_per_batch, weight):
    m, d = x.shape
    emit_h = modn is not None
    kdim = w.shape[0]
    nb = kdim // W_BLK
    seg_nk = tuple(lhs.shape[1] // TF for lhs in lhs_list)
    nk = sum(seg_nk)
    per_token = modc.shape[2] != 1
    nxc = 1
    if tm >= 512:
        nxc = 8 if nk >= 8 else 4

    in_specs = []
    off = 0
    for lhs, n in zip(lhs_list, seg_nk):
        in_specs.append(pl.BlockSpec(
            (tm, TF), lambda i, k, off=off, n=n: (i, jnp.clip(k - off, 0, n - 1))))
        off += n
    for q in range(W_STREAMS):
        in_specs.append(pl.BlockSpec(
            (W_BLK, d), lambda i, k, q=q: (jnp.minimum(W_STREAMS * k + q, nb - 1), 0)))
    if nxc > 1:
        in_specs.append(pl.BlockSpec(
            (tm // nxc, d), lambda i, k: (i * nxc + jnp.minimum(k, nxc - 1), 0)))
    else:
        in_specs.append(pl.BlockSpec((tm, d), lambda i, k: (i, 0)))
    in_specs.append(_mod_spec(modc, rows_per_batch, tm))
    in_specs.append(pl.BlockSpec((1, d), lambda i, k: (0, 0)))
    args = list(lhs_list) + [w] * W_STREAMS + [x, modc, gpost]
    out_specs = [pl.BlockSpec((tm, d), lambda i, k: (i, 0))]
    out_shape = [jax.ShapeDtypeStruct((m, d), F32)]
    if emit_h:
        in_specs.append(_mod_spec(modn, rows_per_batch, tm))
        in_specs.append(pl.BlockSpec((1, d), lambda i, k: (0, 0)))
        args += [modn, gpre]
        out_specs.append(pl.BlockSpec((tm, d), lambda i, k: (i, 0)))
        out_shape.append(jax.ShapeDtypeStruct((m, d), BF16))
    outs = pl.pallas_call(
        functools.partial(_resid_kernel, seg_nk=seg_nk, nxc=nxc, weight=weight,
                          emit_h=emit_h, tm=tm, per_token=per_token),
        grid=(m // tm, nk),
        in_specs=in_specs,
        out_specs=out_specs,
        out_shape=out_shape,
        scratch_shapes=[pltpu.VMEM((tm, d), F32)],
        compiler_params=_cparams(2, 56),
        name="resid_proj",
    )(*args)
    return (outs[0], outs[1]) if emit_h else (outs[0], None)


def _inproj_kernel(h_ref, w_ref, o_ref, w_s):
    @pl.when(pl.program_id(1) == 0)
    def _():
        w_s[...] = w_ref[...].astype(BF16)

    o_ref[...] = jnp.dot(h_ref[...], w_s[...], preferred_element_type=F32)


def _in_proj(h, w_in, *, tm):
    m, d = h.shape
    n = w_in.shape[1]
    tn = 768
    return pl.pallas_call(
        _inproj_kernel,
        grid=(n // tn, m // tm),
        in_specs=[
            pl.BlockSpec((tm, d), lambda j, i: (i, 0)),
            pl.BlockSpec((d, tn), lambda j, i: (0, j)),
        ],
        out_specs=pl.BlockSpec((tm, tn), lambda j, i: (i, j)),
        out_shape=jax.ShapeDtypeStruct((m, n), F32),
        scratch_shapes=[pltpu.VMEM((d, tn), BF16)],
        compiler_params=_cparams(2, 48),
        name="in_proj",
    )(h, w_in)


QBLK = 128
PREP_ROWS = 512
BLOCK_UNROLL_DILATED = 4
BLOCK_UNROLL_SHARED = 2
STAT_LANES = LANES // 4


def _attn_kernel(*refs, dils, shared, t, tail):
    pad = QBLK * max(dils)
    n_pat = len(dils)
    it = iter(refs)
    q_ref, k_ref, v_ref, cos_ref, slo_ref, shi_ref = [next(it) for _ in range(6)]
    sink_ref = next(it) if shared else None
    o_ref, kt_ref, vt_ref = next(it), next(it), next(it)
    qs = next(it)
    nvar = 2 if shared else 1
    kps = [next(it) for _ in range(nvar)]
    vps = [next(it) for _ in range(nvar)]
    num_refs = [next(it) for _ in range(n_pat)] if n_pat > 1 else []
    stat_refs = [next(it) for _ in range(n_pat)] if n_pat > 1 else []

    kv_head = pl.program_id(1) // 4
    lane = lax.broadcasted_iota(jnp.int32, (1, LANES), 1)
    low = lane < HEAD_DIM

    for r in kps + vps:
        r[0:pad, :] = jnp.zeros((pad, LANES), F32)

    def prep(c, carry):
        r0 = pl.multiple_of(c * PREP_ROWS, PREP_ROWS)
        rows = pl.ds(r0, PREP_ROWS)
        cos, slo, shi = cos_ref[rows, :], slo_ref[rows, :], shi_ref[rows, :]
        qs[rows, :] = _rope(q_ref[rows, :], cos, slo, shi) * ATTN_SCALE
        kr = _rope(k_ref[rows, :], cos, slo, shi)
        v = v_ref[rows, :]
        prow = pl.ds(pad + r0, PREP_ROWS)
        if shared:
            kroll = pltpu.roll(kr, HEAD_DIM, 1)
            vroll = pltpu.roll(v, HEAD_DIM, 1)
            for hh in range(2):
                kps[hh][prow, :] = jnp.where(kv_head == hh, kr, kroll)
                vps[hh][prow, :] = jnp.where(kv_head == hh, v, vroll)
        else:
            kps[0][prow, :] = kr
            vps[0][prow, :] = v
        return carry

    lax.fori_loop(0, t // PREP_ROWS, prep, 0)

    for c in range(tail // QBLK):
        crow = pl.ds(pad + t - tail + c * QBLK, QBLK)
        vrow = pl.ds(t - tail + c * QBLK, QBLK)
        if shared:
            kc = jnp.where(kv_head == 0, kps[0][crow, :], kps[1][crow, :])
        else:
            kc = kps[0][crow, :]
        kt_ref[:, c * QBLK:(c + 1) * QBLK] = kc.T
        vt_ref[:, c * QBLK:(c + 1) * QBLK] = v_ref[vrow, :].T

    row = lax.broadcasted_iota(jnp.int32, (QBLK, 2 * QBLK), 0)
    col = lax.broadcasted_iota(jnp.int32, (QBLK, 2 * QBLK), 1)
    band = (col >= row) & (col <= row + HOPS)
    current = col >= QBLK
    nt = (((1,), (1,)), ((), ()))

    for pi, d in enumerate(dils):
        shift = d.bit_length() - 1

        def block(b, carry, d=d, shift=shift, pi=pi):
            n = lax.shift_right_logical(b, jnp.int32(shift))
            q0 = n * (QBLK * d) + (b & (d - 1))
            k0 = q0 + pad - QBLK * d
            if d == 1:
                qsl = pl.ds(pl.multiple_of(q0, QBLK), QBLK)
                ksl = pl.ds(pl.multiple_of(k0, QBLK), 2 * QBLK)
            else:
                qsl = pl.ds(q0, QBLK, stride=d)
                ksl = pl.ds(k0, 2 * QBLK, stride=d)
            q = qs[qsl, :]
            valid = band & (current | (n > 0))
            kb = [r[ksl, :].astype(BF16) for r in kps]
            vb = [r[ksl, :].astype(BF16) for r in vps]
            parts = []
            for hh in range(2):
                var = hh if shared else 0
                qh = jnp.where(low if hh == 0 else jnp.logical_not(low), q, 0.0).astype(BF16)
                s = lax.dot_general(qh, kb[var], nt, preferred_element_type=F32)
                s = jnp.where(valid, s, NEG_INF)
                mb = jnp.max(s, axis=1, keepdims=True)
                p = jnp.exp(s - mb)
                den = jnp.sum(p, axis=1, keepdims=True)
                if n_pat == 1 and shared:
                    den = den + jnp.exp(sink_ref[:, hh * HEAD_DIM:hh * HEAD_DIM + 1] - mb)
                parts.append((mb, den,
                              jnp.dot(p.astype(BF16), vb[var], preferred_element_type=F32)))
            pv = jnp.where(low, parts[0][2], parts[1][2])
            if n_pat == 1:
                o_ref[qsl, :] = (pv / jnp.where(low, parts[0][1], parts[1][1])).astype(BF16)
            else:
                num_refs[pi][qsl, :] = pv
                stat_refs[pi][qsl, :] = jnp.where(
                    lane < STAT_LANES, parts[0][0],
                    jnp.where(low, parts[0][1],
                              jnp.where(lane < 3 * STAT_LANES, parts[1][0], parts[1][1])))
            return carry

        lax.fori_loop(0, t // QBLK, block, 0,
                      unroll=BLOCK_UNROLL_SHARED if shared else BLOCK_UNROLL_DILATED)

    if n_pat > 1:
        is_max = (lane & (HEAD_DIM - 1)) < STAT_LANES

        def finish(c, carry):
            rows = pl.ds(pl.multiple_of(c * PREP_ROWS, PREP_ROWS), PREP_ROWS)
            ms, ls = [], []
            for r in stat_refs:
                st = r[rows, :]
                ms.append(jnp.where(is_max, st, pltpu.roll(st, STAT_LANES, 1)))
                ls.append(jnp.where(is_max, pltpu.roll(st, LANES - STAT_LANES, 1), st))
            m = functools.reduce(jnp.maximum, ms)
            den = num = None
            for mi, li, r in zip(ms, ls, num_refs):
                w = jnp.exp(mi - m)
                den = w * li if den is None else den + w * li
                num = w * r[rows, :] if num is None else num + w * r[rows, :]
            o_ref[rows, :] = (num / den).astype(BF16)
            return carry

        lax.fori_loop(0, t // PREP_ROWS, finish, 0)


def _attention(u, tables, sinks, *, n_batch, t, dils, q_slab, k_slab, v_slab, n_slab, shared,
               tail):
    pad = QBLK * max(dils)
    cos, slo, shi = tables
    const = lambda b, s: (0, 0)
    once = pl.Buffered(1)
    in_specs = [
        pl.BlockSpec((t, LANES), lambda b, s: (b, q_slab + s)),
        pl.BlockSpec((t, LANES), (lambda b, s: (b, k_slab)) if shared
                     else (lambda b, s: (b, k_slab + s))),
        pl.BlockSpec((t, LANES), (lambda b, s: (b, v_slab)) if shared
                     else (lambda b, s: (b, v_slab + s))),
        pl.BlockSpec((t, LANES), const, pipeline_mode=once),
        pl.BlockSpec((t, LANES), const, pipeline_mode=once),
        pl.BlockSpec((t, LANES), const, pipeline_mode=once),
    ]
    args = [u, u, u, cos, slo, shi]
    if shared:
        in_specs.append(pl.BlockSpec((None, 1, LANES), lambda b, s: (s, 0, 0)))
        args.append(sinks)
    n_kslab = 1 if shared else n_slab
    tail_spec = pl.BlockSpec((None, LANES, tail), (lambda b, s: (b, 0, 0)) if shared
                             else (lambda b, s: (b, s, 0)))
    nvar = 2 if shared else 1
    scratch = [pltpu.VMEM((t, LANES), F32)]
    scratch += [pltpu.VMEM((pad + t, LANES), F32) for _ in range(2 * nvar)]
    if len(dils) > 1:
        scratch += [pltpu.VMEM((t, LANES), F32) for _ in range(2 * len(dils))]
    return pl.pallas_call(
        functools.partial(_attn_kernel, dils=dils, shared=shared, t=t, tail=tail),
        grid=(n_batch, n_slab),
        in_specs=in_specs,
        out_specs=[
            pl.BlockSpec((t, LANES), lambda b, s: (b, s)),
            tail_spec, tail_spec,
        ],
        out_shape=[
            jax.ShapeDtypeStruct((n_batch * t, n_slab * LANES), BF16),
            jax.ShapeDtypeStruct((n_batch, n_kslab * LANES, tail), F32),
            jax.ShapeDtypeStruct((n_batch, n_kslab * LANES, tail), F32),
        ],
        scratch_shapes=scratch,
        compiler_params=_cparams(2, 48),
        name="attn_shared" if shared else "attn_dilated",
    )(*args)


CONV_TB = 256
CONV_HALO = 32
CONV_RC = 32


def _layer_norm_swish(y, g, b):
    mu = jnp.mean(y, axis=-1, keepdims=True)
    yc = y - mu
    var = jnp.mean(yc * yc, axis=-1, keepdims=True)
    z = yc * lax.rsqrt(var + LN_EPS) * g + b
    return z * _sigmoid(z)


def _conv_kernel(v0, v1, g0, g1, cw_ref, cb_ref, lg_ref, lb_ref, o_ref, st_ref, ext):
    tt = pl.program_id(1)
    half = D_CONV // 2

    @pl.when(tt == 0)
    def _():
        ext[0:CONV_HALO, :] = jnp.zeros((CONV_HALO, D_CONV), F32)

    @pl.when(tt > 0)
    def _():
        ext[0:CONV_HALO, :] = ext[CONV_TB:CONV_TB + CONV_HALO, :]

    ext[CONV_HALO:CONV_HALO + CONV_TB, 0:half] = v0[...] * _sigmoid(g0[...])
    ext[CONV_HALO:CONV_HALO + CONV_TB, half:D_CONV] = v1[...] * _sigmoid(g1[...])
    st_ref[...] = ext[CONV_TB:CONV_TB + CONV_HALO, :]

    lead = CONV_HALO - (CONV_WIDTH - 1)
    for c in range(CONV_TB // CONV_RC):
        r0 = c * CONV_RC
        y = jnp.zeros((CONV_RC, D_CONV), F32) + cb_ref[...]
        for w in range(CONV_WIDTH):
            y = y + ext[r0 + lead + w:r0 + lead + w + CONV_RC, :] * cw_ref[w:w + 1, :]
        o_ref[r0:r0 + CONV_RC, :] = _layer_norm_swish(y, lg_ref[...], lb_ref[...]).astype(BF16)


def _conv_module(u, cw, cb, lg, lb, *, n_batch, t):
    nt = t // CONV_TB
    half = D_CONV // 2
    v_blk, g_blk = COL_VAL // half, COL_GATE // half
    row = lambda b, i: b * nt + i
    vec = pl.BlockSpec((1, D_CONV), lambda b, i: (0, 0))
    return pl.pallas_call(
        _conv_kernel,
        grid=(n_batch, nt),
        in_specs=[
            pl.BlockSpec((CONV_TB, half), lambda b, i: (row(b, i), v_blk)),
            pl.BlockSpec((CONV_TB, half), lambda b, i: (row(b, i), v_blk + 1)),
            pl.BlockSpec((CONV_TB, half), lambda b, i: (row(b, i), g_blk)),
            pl.BlockSpec((CONV_TB, half), lambda b, i: (row(b, i), g_blk + 1)),
            pl.BlockSpec((CONV_HALO, D_CONV), lambda b, i: (0, 0)),
            vec, vec, vec,
        ],
        out_specs=[
            pl.BlockSpec((CONV_TB, D_CONV), lambda b, i: (row(b, i), 0)),
            pl.BlockSpec((CONV_HALO, D_CONV), lambda b, i: (b, 0)),
        ],
        out_shape=[
            jax.ShapeDtypeStruct((n_batch * t, D_CONV), BF16),
            jax.ShapeDtypeStruct((n_batch * CONV_HALO, D_CONV), F32),
        ],
        scratch_shapes=[pltpu.VMEM((CONV_HALO + CONV_TB, D_CONV), F32)],
        compiler_params=_cparams(2, 32),
        name="conv_module",
    )(u, u, u, u, cw, cb, lg, lb)


ROPE_SLABS = tuple(range(SLAB_QA, SLAB_VA)) + tuple(range(SLAB_QB, SLAB_VB))
Q_SLABS = tuple(range(SLAB_QA, SLAB_KA)) + tuple(range(SLAB_QB, SLAB_KB))
VAL_SLABS = tuple(range(COL_VAL // LANES, COL_GATE // LANES))


def _sample_post_kernel(u_ref, cos_ref, slo_ref, shi_ref, o_ref):
    for s in range(N_IN // LANES):
        x = u_ref[:, s * LANES:(s + 1) * LANES]
        if s in ROPE_SLABS:
            x = _rope(x, cos_ref[...], slo_ref[...], shi_ref[...])
            if s in Q_SLABS:
                x = x * ATTN_SCALE
        elif s in VAL_SLABS:
            gs = s + D_CONV // LANES
            x = x * _sigmoid(u_ref[:, gs * LANES:(gs + 1) * LANES])
        o_ref[:, s * LANES:(s + 1) * LANES] = x


def _sample_post(u, tables):
    m, n = u.shape
    full = pl.BlockSpec((m, n), lambda i: (0, 0))
    vec = pl.BlockSpec((1, LANES), lambda i: (0, 0))
    return pl.pallas_call(
        _sample_post_kernel,
        grid=(1,),
        in_specs=[full, vec, vec, vec],
        out_specs=full,
        out_shape=jax.ShapeDtypeStruct((m, n), F32),
        compiler_params=_cparams(1, 16),
        name="sample_post",
    )(u, *tables)


def _pattern_multiplicity(la):
    dist = la - jnp.arange(la, dtype=jnp.int32)
    mult = jnp.zeros((la,), F32)
    for d in DILATIONS_A:
        mult = mult + ((dist % d == 0) & (dist <= HOPS * d)).astype(F32)
    return mult.reshape(1, la)


def _sattn_shift_kernel(kt_ref, vt_ref, q_ref, knr_ref, vnr_ref, knc_ref, vnc_ref, mult_ref,
                        *rest):
    o_ref, ko_ref, vo_ref = rest[-3:]
    nh, e, la = kt_ref.shape
    w = nh * e
    nt = (((1,), (1,)), ((), ()))
    k = kt_ref[...].reshape(w, la)
    v = vt_ref[...].reshape(w, la)
    q = q_ref[...]
    mult = mult_ref[...]
    s = jnp.dot(q.astype(BF16), k.astype(BF16), preferred_element_type=F32)
    s = jnp.where(mult > 0.0, s, NEG_INF)
    s_new = jnp.sum(q * knr_ref[...], axis=1, keepdims=True)
    m = jnp.maximum(jnp.max(s, axis=1, keepdims=True), s_new)
    p = jnp.exp(s - m) * mult
    p_new = jnp.exp(s_new - m) * float(len(DILATIONS_A))
    den = jnp.sum(p, axis=1, keepdims=True) + p_new
    num = lax.dot_general(p.astype(BF16), v.astype(BF16), nt, preferred_element_type=F32)
    o_ref[...] = (num + p_new * vnr_ref[...]) / den
    newest = lax.broadcasted_iota(jnp.int32, (1, la), 1) == la - 1
    ko_ref[...] = jnp.where(newest, knc_ref[...], pltpu.roll(k, la - 1, 1)).reshape(nh, e, la)
    vo_ref[...] = jnp.where(newest, vnc_ref[...], pltpu.roll(v, la - 1, 1)).reshape(nh, e, la)


def _sample_attn_shift(kt, vt, prev, layer, q_bd, kn, vn, mult):
    depth, nb, nh, e, la = kt.shape
    w = nh * e
    slab = pl.BlockSpec((None, None, nh, e, la), lambda b: (layer, b, 0, 0, 0))
    tokq = pl.BlockSpec((None, nh, w), lambda b: (b, 0, 0))
    row = pl.BlockSpec((None, 1, w), lambda b: (b, 0, 0))
    col = pl.BlockSpec((None, w, 1), lambda b: (b, 0, 0))
    in_specs = [slab, slab, tokq, row, row, col, col, pl.BlockSpec((1, la), lambda b: (0, 0))]
    args = [kt, vt, q_bd, kn.reshape(nb, 1, w), vn.reshape(nb, 1, w),
            kn.reshape(nb, w, 1), vn.reshape(nb, w, 1), mult]
    aliases = {}
    if prev is not None:
        aliases = {len(args): 1, len(args) + 1: 2}
        in_specs += [pl.BlockSpec(memory_space=pl.ANY)] * 2
        args += list(prev)
    return pl.pallas_call(
        _sattn_shift_kernel,
        grid=(nb,),
        in_specs=in_specs,
        out_specs=[tokq, slab, slab],
        out_shape=[jax.ShapeDtypeStruct((nb, nh, w), F32),
                   jax.ShapeDtypeStruct(kt.shape, F32), jax.ShapeDtypeStruct(vt.shape, F32)],
        input_output_aliases=aliases,
        compiler_params=_cparams(1, 56),
        name="sample_attn_shift",
    )(*args)


def _sattn_b_kernel(kc_ref, vc_ref, q_ref, kn_ref, vn_ref, sink_ref, o_ref):
    nt = (((1,), (1,)), ((), ()))
    q = q_ref[...]
    s = lax.dot_general(q.astype(BF16), kc_ref[...].astype(BF16), nt, preferred_element_type=F32)
    s_new = jnp.sum(q * kn_ref[...], axis=1, keepdims=True)
    m = jnp.maximum(jnp.max(s, axis=1, keepdims=True), s_new)
    p = jnp.exp(s - m)
    p_new = jnp.exp(s_new - m)
    den = jnp.sum(p, axis=1, keepdims=True) + p_new + jnp.exp(sink_ref[...] - m)
    num = jnp.dot(p.astype(BF16), vc_ref[...].astype(BF16), preferred_element_type=F32)
    o_ref[...] = (num + p_new * vn_ref[...]) / den


def _sample_attn_b(kc, vc, layer, q_exp, kn, vn, sinks):
    _, nb, lb, w = kc.shape
    nh = q_exp.shape[1]
    cache = pl.BlockSpec((None, None, lb, w), lambda b: (layer, b, 0, 0))
    tokq = pl.BlockSpec((None, nh, w), lambda b: (b, 0, 0))
    tok1 = pl.BlockSpec((None, 1, w), lambda b: (b, 0, 0))
    return pl.pallas_call(
        _sattn_b_kernel,
        grid=(nb,),
        in_specs=[cache, cache, tokq, tok1, tok1, pl.BlockSpec((nh, 1), lambda b: (0, 0))],
        out_specs=tokq,
        out_shape=jax.ShapeDtypeStruct((nb, nh, w), F32),
        compiler_params=_cparams(1, 16),
        name="sample_attn_shared",
    )(kc, vc, q_exp, kn, vn, sinks)


def _sconv_kernel(st_ref, glu_ref, cw_ref, cb_ref, lg_ref, lb_ref, o_ref, ns_ref):
    hist = CONV_WIDTH - 1
    glu = glu_ref[...]
    y = glu * cw_ref[hist:hist + 1, :] + cb_ref[...]
    for w in range(hist):
        y = y + st_ref[:, w * D_CONV:(w + 1) * D_CONV] * cw_ref[w:w + 1, :]
    o_ref[...] = _layer_norm_swish(y, lg_ref[...], lb_ref[...]).astype(BF16)
    ns_ref[:, 0:(hist - 1) * D_CONV] = st_ref[:, D_CONV:hist * D_CONV]
    ns_ref[:, (hist - 1) * D_CONV:hist * D_CONV] = glu


def _sample_conv(state2d, layer, glu, cw, cb, lg, lb):
    _, nb, width = state2d.shape
    st = pl.BlockSpec((None, nb, width), lambda i: (layer, 0, 0))
    tok = pl.BlockSpec((nb, D_CONV), lambda i: (0, 0))
    vec = pl.BlockSpec((1, D_CONV), lambda i: (0, 0))
    return pl.pallas_call(
        _sconv_kernel,
        grid=(1,),
        in_specs=[st, tok, pl.BlockSpec((CONV_HALO, D_CONV), lambda i: (0, 0)), vec, vec, vec],
        out_specs=[tok, pl.BlockSpec((nb, width), lambda i: (0, 0))],
        out_shape=[jax.ShapeDtypeStruct((nb, D_CONV), BF16),
                   jax.ShapeDtypeStruct((nb, width), F32)],
        compiler_params=_cparams(1, 16),
        name="sample_conv",
    )(state2d, glu, cw, cb, lg, lb)


def _shift_kernel(km, kx, kn, vm, vx, vn, ko, vo):
    last = pl.program_id(2) == pl.num_programs(2) - 1
    for main, nxt, new, out in ((km, kx, kn, ko), (vm, vx, vn, vo)):
        rb = main.shape[1]
        out[:, 0:rb - 1] = main[:, 1:rb]
        out[:, rb - 1:rb] = jnp.where(last, new[...], nxt[...])


def _shift_caches(cache_k, cache_v, new_k, new_v, *, batch_chunk, row_block):
    depth, nb, rows, nh, e = cache_k.shape
    main = pl.BlockSpec((None, batch_chunk, row_block, nh, e), lambda l, b, r: (l, b, r, 0, 0))
    nxt = pl.BlockSpec((None, batch_chunk, 1, nh, e),
                       lambda l, b, r: (l, b, jnp.minimum((r + 1) * row_block, rows - 1), 0, 0))
    new = pl.BlockSpec((None, batch_chunk, 1, nh, e), lambda l, b, r: (l, b, 0, 0, 0))
    shape = jax.ShapeDtypeStruct(cache_k.shape, cache_k.dtype)
    return pl.pallas_call(
        _shift_kernel,
        grid=(depth, nb // batch_chunk, rows // row_block),
        in_specs=[main, nxt, new, main, nxt, new],
        out_specs=[main, main],
        out_shape=[shape, shape],
        compiler_params=_cparams(3, 48),
        name="cache_shift",
    )(cache_k, cache_k, new_k, cache_v, cache_v, new_v)


def _rope_tables(pos):
    n = pos.shape[0]
    inv_freq = 1.0 / (ROPE_THETA ** (jnp.arange(ROT_HALF, dtype=F32) / ROT_HALF))
    ang = pos.astype(F32)[:, None] * inv_freq[None, :]
    cos, sin = jnp.cos(ang), jnp.sin(ang)
    rest = HEAD_DIM - 2 * ROT_HALF
    c = jnp.concatenate([cos, cos, jnp.ones((n, rest), F32)], axis=1)
    lo = jnp.concatenate([-sin, jnp.zeros((n, HEAD_DIM - ROT_HALF), F32)], axis=1)
    hi = jnp.concatenate([jnp.zeros((n, ROT_HALF), F32), sin, jnp.zeros((n, rest), F32)], axis=1)
    return tuple(jnp.tile(x, (1, LANES // HEAD_DIM)) for x in (c, lo, hi))


def kernel(x_prompt, x_sample, c_prompt, c_sample, cache_a_k, cache_a_v, cache_b_k, cache_b_v,
           state_c_conv, w_ada, b_ada, g_pre, g_post, w_ffn_gu, w_ffn_down, w_in, w_out,
           attn_sinks, conv_w, conv_b, conv_ln_g, conv_ln_b):
    nbp, t, d = x_prompt.shape
    nbs = x_sample.shape[0]
    mp = nbp * t
    depth = w_ada.shape[0]
    la, lb = cache_a_k.shape[2], cache_b_k.shape[2]
    hist = CONV_WIDTH - 1
    tm_p = 1024

    xp = x_prompt.reshape(mp, d)
    xs = x_sample.reshape(nbs, d)

    s_row = 32
    c_all = jnp.zeros((s_row + nbs, d), F32).at[:nbp].set(c_prompt).at[s_row:].set(c_sample)
    mod_all = _ada_mod(c_all, w_ada, b_ada)

    def mod_p(l, s):
        return mod_all[l, :nbp, 3 * s * d:3 * (s + 1) * d].reshape(nbp, 3, 1, d)

    def mod_s(l, s):
        m = mod_all[l, s_row:, 3 * s * d:3 * (s + 1) * d].reshape(nbs, 3, d)
        return jnp.transpose(m, (1, 0, 2))[None]

    def nxt(l, s):
        return (l, s + 1) if s + 1 < N_SUB else (l + 1, 0)

    tab_p = _rope_tables(jnp.arange(t, dtype=jnp.int32))
    tab_s = _rope_tables(PAST_LEN + jnp.arange(1, dtype=jnp.int32))
    sink_slabs = jnp.repeat(attn_sinks, HEAD_DIM, axis=1).reshape(depth, N_HEADS_B // 2, 1, LANES)
    sink_cols = attn_sinks.reshape(depth, N_HEADS_B, 1)
    cw_pad = jnp.pad(conv_w, ((0, 0), (0, CONV_HALO - CONV_WIDTH), (0, 0)))
    vec = lambda a, l: a[l].reshape(1, -1)

    cache_bk2 = cache_b_k.reshape(depth, nbs, lb, N_KV_B * HEAD_DIM)
    cache_bv2 = cache_b_v.reshape(depth, nbs, lb, N_KV_B * HEAD_DIM)
    state2d = state_c_conv.reshape(depth, nbs, hist * D_CONV)
    kv_of_head = (jnp.arange(N_HEADS_B) // (N_HEADS_B // N_KV_B))[:, None] == jnp.arange(N_KV_B)
    cache_akt = jnp.transpose(cache_a_k, (0, 1, 3, 4, 2))
    cache_avt = jnp.transpose(cache_a_v, (0, 1, 3, 4, 2))
    mult_a = _pattern_multiplicity(la)
    head_eye = jnp.eye(N_HEADS_A, dtype=bool)[None, :, :, None]

    hp = _prenorm(xp, mod_p(0, 0), vec(g_pre[0], 0), tm=tm_p, rows_per_batch=t)
    hs = _prenorm(xs, mod_s(0, 0), vec(g_pre[0], 0), tm=nbs, rows_per_batch=nbs)

    st_p = [[] for _ in range(5)]
    new_rows = [[], []]
    shift_a = None
    st_c_s = []

    def resid(lhs, w, x, l, s, weight, mod_fn, tm, rpb):
        ln, sn = nxt(l, s)
        last = ln >= depth
        return _resid_proj(
            lhs, w, x, mod_fn(l, s), vec(g_post[l], s),
            None if last else mod_fn(ln, sn), None if last else vec(g_pre[ln], sn),
            tm=tm, rows_per_batch=rpb, weight=weight)

    for l in range(depth):
        a = _ffn_up(hp, w_ffn_gu[l, 0], tm=tm_p)
        xp, hp = resid([a], w_ffn_down[l, 0], xp, l, 0, 0.5, mod_p, tm_p, t)
        u = _in_proj(hp, w_in[l], tm=tm_p)
        oa, kta, vta = _attention(u, tab_p, None, n_batch=nbp, t=t, dils=DILATIONS_A,
                             q_slab=SLAB_QA, k_slab=SLAB_KA, v_slab=SLAB_VA,
                             n_slab=N_HEADS_A // 2, shared=False, tail=min(la, t))
        ob, ktb, vtb = _attention(u, tab_p, sink_slabs[l], n_batch=nbp, t=t, dils=(1,),
                             q_slab=SLAB_QB, k_slab=SLAB_KB, v_slab=SLAB_VB,
                             n_slab=N_HEADS_B // 2, shared=True, tail=min(lb, t))
        oc, cst = _conv_module(u, cw_pad[l], vec(conv_b, l), vec(conv_ln_g, l), vec(conv_ln_b, l),
                               n_batch=nbp, t=t)
        xp, hp = resid([oa, ob, oc], w_out[l], xp, l, 1, 1.0, mod_p, tm_p, t)
        a = _ffn_up(hp, w_ffn_gu[l, 1], tm=tm_p)
        xp, hp = resid([a], w_ffn_down[l, 1], xp, l, 2, 0.5, mod_p, tm_p, t)

        for dst, tail_t, heads in ((st_p[0], kta, N_HEADS_A), (st_p[1], vta, N_HEADS_A),
                                   (st_p[2], ktb, N_KV_B), (st_p[3], vtb, N_KV_B)):
            dst.append(jnp.transpose(tail_t.reshape(nbp, heads, HEAD_DIM, -1), (0, 3, 1, 2)))
        st_p[4].append(cst.reshape(nbp, CONV_HALO, D_CONV)[:, CONV_HALO - hist:])

        a = _ffn_up(hs, w_ffn_gu[l, 0], tm=nbs)
        xs, hs = resid([a], w_ffn_down[l, 0], xs, l, 0, 0.5, mod_s, nbs, nbs)
        ur = _sample_post(_in_proj(hs, w_in[l], tm=nbs), tab_s)
        seg = lambda lo, hi: ur[:, lo * LANES:hi * LANES]
        qa = seg(SLAB_QA, SLAB_KA).reshape(nbs, 1, N_HEADS_A, HEAD_DIM)
        q_bd = jnp.where(head_eye, qa, 0.0).reshape(nbs, N_HEADS_A, N_HEADS_A * HEAD_DIM)
        oa3, *shift_a = _sample_attn_shift(cache_akt, cache_avt, shift_a, l, q_bd,
                                           seg(SLAB_KA, SLAB_VA), seg(SLAB_VA, SLAB_QB), mult_a)
        oa = jnp.sum(jnp.where(head_eye, oa3.reshape(nbs, N_HEADS_A, N_HEADS_A, HEAD_DIM), 0.0),
                     axis=1)
        qb = seg(SLAB_QB, SLAB_KB).reshape(nbs, N_HEADS_B, 1, HEAD_DIM)
        q_exp = jnp.where(kv_of_head[None, :, :, None], qb, 0.0).reshape(nbs, N_HEADS_B, LANES)
        knb = seg(SLAB_KB, SLAB_VB)
        vnb = seg(SLAB_VB, SLAB_VB + 1)
        ob2 = _sample_attn_b(cache_bk2, cache_bv2, l, q_exp, knb.reshape(nbs, 1, LANES),
                             vnb.reshape(nbs, 1, LANES), sink_cols[l])
        ob = jnp.sum(jnp.where(kv_of_head[None, :, :, None],
                               ob2.reshape(nbs, N_HEADS_B, N_KV_B, HEAD_DIM), 0.0), axis=2)
        glu = ur[:, COL_VAL:COL_GATE]
        oc, ns = _sample_conv(state2d, l, glu, cw_pad[l], vec(conv_b, l), vec(conv_ln_g, l),
                              vec(conv_ln_b, l))
        mix = [oa.reshape(nbs, -1).astype(BF16), ob.reshape(nbs, -1).astype(BF16), oc]
        xs, hs = resid(mix, w_out[l], xs, l, 1, 1.0, mod_s, nbs, nbs)
        a = _ffn_up(hs, w_ffn_gu[l, 1], tm=nbs)
        xs, hs = resid([a], w_ffn_down[l, 1], xs, l, 2, 0.5, mod_s, nbs, nbs)

        new_rows[0].append(knb.reshape(nbs, 1, N_KV_B, HEAD_DIM))
        new_rows[1].append(vnb.reshape(nbs, 1, N_KV_B, HEAD_DIM))
        st_c_s.append(ns.reshape(nbs, hist, D_CONV))

    new_rows = [jnp.stack(r) for r in new_rows]
    shift_a = [jnp.transpose(c, (0, 1, 4, 2, 3)) for c in shift_a]
    shift_b = _shift_caches(cache_b_k, cache_b_v, new_rows[0], new_rows[1],
                            batch_chunk=8, row_block=lb)
    return (xp.reshape(nbp, t, d), xs.reshape(nbs, 1, d),
            *[jnp.stack(s) for s in st_p], *shift_a, *shift_b, jnp.stack(st_c_s))
```

```python
import functools

import jax
import jax.numpy as jnp
from jax import lax
from jax.experimental import pallas as pl
from jax.experimental.pallas import tpu as pltpu

F32 = jnp.float32
BF16 = jnp.bfloat16

D_MODEL = 2048
DEPTH = 4
HEAD_DIM = 64
N_HEADS_A = 8
N_HEADS_B = 16
N_KV_B = 2
D_CONV = 512
DILATIONS_A = (1, 4, 16)
HOPS = 128
CONV_WIDTH = 31
ROT_HALF = 8
ROPE_THETA = 500000.0
PAST_LEN = 16384
RMS_EPS = 1e-6
LN_EPS = 1e-5
NEG_INF = -1e30
ATTN_SCALE = HEAD_DIM ** -0.5
N_SUB = 3

LANES = 128
MIB = 1 << 20

SLAB_QA, SLAB_KA, SLAB_VA = 0, 4, 8
SLAB_QB, SLAB_KB, SLAB_VB = 12, 20, 21
COL_VAL, COL_GATE = 2816, 3328
N_IN = 3840


def _cparams(n_axes, vmem_mib):
    return pltpu.CompilerParams(
        dimension_semantics=("arbitrary",) * n_axes,
        vmem_limit_bytes=vmem_mib * MIB,
    )


def _sigmoid(x):
    return jax.nn.sigmoid(x)


def _pre_norm(x, g, shift, scale):
    ms = jnp.mean(x * x, axis=-1, keepdims=True)
    return (x * lax.rsqrt(ms + RMS_EPS)) * g * (1.0 + scale) + shift


def _gated_post(y, g, gate, weight):
    ms = jnp.mean(y * y, axis=-1, keepdims=True)
    return (weight * gate) * ((y * lax.rsqrt(ms + RMS_EPS)) * g)


def _rope(x, cos, sin_lo, sin_hi):
    return x * cos + pltpu.roll(x, LANES - ROT_HALF, 1) * sin_lo + pltpu.roll(x, ROT_HALF, 1) * sin_hi


def _ada_kernel(c_ref, w_ref, b_ref, o_ref):
    c = c_ref[...]
    a = (c * _sigmoid(c)).astype(BF16)
    o_ref[...] = jnp.dot(a, w_ref[...].astype(BF16), preferred_element_type=F32) + b_ref[...]


def _ada_mod(c_all, w_ada, b_ada):
    depth, d, n = w_ada.shape
    rows = c_all.shape[0]
    tn = 1024
    return pl.pallas_call(
        _ada_kernel,
        grid=(depth, n // tn),
        in_specs=[
            pl.BlockSpec((rows, d), lambda l, j: (0, 0)),
            pl.BlockSpec((None, d, tn), lambda l, j: (l, 0, j)),
            pl.BlockSpec((None, 1, tn), lambda l, j: (l, 0, j)),
        ],
        out_specs=pl.BlockSpec((None, rows, tn), lambda l, j: (l, 0, j)),
        out_shape=jax.ShapeDtypeStruct((depth, rows, n), F32),
        compiler_params=_cparams(2, 40),
        name="ada_mod",
    )(c_all, w_ada, b_ada.reshape(depth, 1, n))


def _prenorm_kernel(x_ref, mod_ref, g_ref, h_ref):
    h_ref[...] = _pre_norm(x_ref[...], g_ref[...], mod_ref[0], mod_ref[1]).astype(BF16)


def _mod_spec(mod, rows_per_batch, tm):
    _, _, r, d = mod.shape
    if r == 1:
        per = rows_per_batch // tm
        return pl.BlockSpec((None, 3, 1, d), lambda i, *_: (i // per, 0, 0, 0))
    return pl.BlockSpec((None, 3, r, d), lambda i, *_: (0, 0, 0, 0))


def _prenorm(x, mod, g, *, tm, rows_per_batch):
    m, d = x.shape
    return pl.pallas_call(
        _prenorm_kernel,
        grid=(m // tm,),
        in_specs=[
            pl.BlockSpec((tm, d), lambda i: (i, 0)),
            _mod_spec(mod, rows_per_batch, tm),
            pl.BlockSpec((1, d), lambda i: (0, 0)),
        ],
        out_specs=pl.BlockSpec((tm, d), lambda i: (i, 0)),
        out_shape=jax.ShapeDtypeStruct((m, d), BF16),
        compiler_params=_cparams(1, 32),
        name="prenorm",
    )(x, mod, g)


W_BLK = 128
W_STREAMS = 4
TF = W_BLK * W_STREAMS


def _ffn_up_kernel(h_ref, *refs, f_valid):
    g_refs = refs[0:W_STREAMS]
    u_refs = refs[W_STREAMS:2 * W_STREAMS]
    a_ref, wg_s, wu_s = refs[2 * W_STREAMS:]
    j = pl.program_id(0)

    @pl.when(pl.program_id(1) == 0)
    def _():
        for q in range(W_STREAMS):
            wg_s[:, q * W_BLK:(q + 1) * W_BLK] = g_refs[q][...].astype(BF16)
            wu_s[:, q * W_BLK:(q + 1) * W_BLK] = u_refs[q][...].astype(BF16)

    h = h_ref[...]
    g = jnp.dot(h, wg_s[...], preferred_element_type=F32)
    u = jnp.dot(h, wu_s[...], preferred_element_type=F32)
    a = (g * _sigmoid(g)) * u
    col = lax.broadcasted_iota(jnp.int32, a.shape, 1)
    a_ref[...] = jnp.where(col < f_valid - j * TF, a, 0.0).astype(BF16)


def _ffn_up(h, w_gu, lead, *, tm):
    m, d = h.shape
    f = w_gu.shape[-1] // 2
    squeezed = (None,) * len(lead)
    nb = f // W_BLK
    nj = pl.cdiv(f, TF)

    def wspec(base, q):
        return pl.BlockSpec(
            squeezed + (d, W_BLK),
            lambda j, i: lead + (0, base + jnp.minimum(W_STREAMS * j + q, nb - 1)))

    in_specs = [pl.BlockSpec((tm, d), lambda j, i: (i, 0))]
    in_specs += [wspec(0, q) for q in range(W_STREAMS)]
    in_specs += [wspec(nb, q) for q in range(W_STREAMS)]
    return pl.pallas_call(
        functools.partial(_ffn_up_kernel, f_valid=f),
        grid=(nj, m // tm),
        in_specs=in_specs,
        out_specs=pl.BlockSpec((tm, TF), lambda j, i: (i, j)),
        out_shape=jax.ShapeDtypeStruct((m, nj * TF), BF16),
        scratch_shapes=[pltpu.VMEM((d, TF), BF16), pltpu.VMEM((d, TF), BF16)],
        compiler_params=_cparams(2, 48),
        name="ffn_up",
    )(h, *([w_gu] * (2 * W_STREAMS)))


EPI_ROWS = 256


def _resid_kernel(*refs, seg_nk, nxc, weight, emit_h, tm, per_token):
    nseg = len(seg_nk)
    it = iter(refs)
    lhs_refs = [next(it) for _ in range(nseg)]
    w_refs = [next(it) for _ in range(W_STREAMS)]
    xin_ref, modc_ref, gpost_ref = next(it), next(it), next(it)
    modn_ref = gpre_ref = None
    if emit_h:
        modn_ref, gpre_ref = next(it), next(it)
    x_out = next(it)
    h_out = next(it) if emit_h else None
    acc = next(it)
    k = pl.program_id(1)
    nk = sum(seg_nk)

    w = jnp.concatenate([r[...].astype(BF16) for r in w_refs], axis=0)

    @pl.when(k == 0)
    def _():
        acc[...] = jnp.zeros_like(acc)

    if nxc > 1:
        xr = tm // nxc

        @pl.when(k < nxc)
        def _():
            x_out[pl.ds(pl.multiple_of(k * xr, xr), xr), :] = xin_ref[...]

    if nseg == 1:
        acc[...] += jnp.dot(lhs_refs[0][...], w, preferred_element_type=F32)
    else:
        off = 0
        for s in range(nseg):
            lo, hi = off, off + seg_nk[s]

            @pl.when((k >= lo) & (k < hi))
            def _(s=s):
                acc[...] += jnp.dot(lhs_refs[s][...], w, preferred_element_type=F32)

            off = hi

    def epilogue(rows):
        y = acc[rows, :]
        x = xin_ref[rows, :] if nxc == 1 else x_out[rows, :]
        if per_token:
            gate = modc_ref[2, rows, :]
        else:
            gate = modc_ref[2]
        xn = x + _gated_post(y, gpost_ref[...], gate, weight)
        x_out[rows, :] = xn
        if emit_h:
            if per_token:
                shift, scale = modn_ref[0, rows, :], modn_ref[1, rows, :]
            else:
                shift, scale = modn_ref[0], modn_ref[1]
            h_out[rows, :] = _pre_norm(xn, gpre_ref[...], shift, scale).astype(BF16)

    @pl.when(k == nk - 1)
    def _():
        if tm <= EPI_ROWS:
            epilogue(slice(None))
        else:
            def body(c, carry):
                epilogue(pl.ds(pl.multiple_of(c * EPI_ROWS, EPI_ROWS), EPI_ROWS))
                return carry
            lax.fori_loop(0, tm // EPI_ROWS, body, 0)


def _resid_proj(lhs_list, w, lead, x, modc, gpost, modn, gpre, *, tm, rows_per_batch, weight):
    m, d = x.shape
    emit_h = modn is not None
    kdim = w.shape[-2]
    squeezed = (None,) * len(lead)
    nb = kdim // W_BLK
    seg_nk = tuple(lhs.shape[1] // TF for lhs in lhs_list)
    nk = sum(seg_nk)
    per_token = modc.shape[2] != 1
    nxc = 1
    if tm >= 512:
        nxc = 8 if nk >= 8 else 4

    in_specs = []
    off = 0
    for lhs, n in zip(lhs_list, seg_nk):
        in_specs.append(pl.BlockSpec(
            (tm, TF), lambda i, k, off=off, n=n: (i, jnp.clip(k - off, 0, n - 1))))
        off += n
    for q in range(W_STREAMS):
        in_specs.append(pl.BlockSpec(
            squeezed + (W_BLK, d),
            lambda i, k, q=q: lead + (jnp.minimum(W_STREAMS * k + q, nb - 1), 0)))
    if nxc > 1:
        in_specs.append(pl.BlockSpec(
            (tm // nxc, d), lambda i, k: (i * nxc + jnp.minimum(k, nxc - 1), 0)))
    else:
        in_specs.append(pl.BlockSpec((tm, d), lambda i, k: (i, 0)))
    in_specs.append(_mod_spec(modc, rows_per_batch, tm))
    in_specs.append(pl.BlockSpec((1, d), lambda i, k: (0, 0)))
    args = list(lhs_list) + [w] * W_STREAMS + [x, modc, gpost]
    out_specs = [pl.BlockSpec((tm, d), lambda i, k: (i, 0))]
    out_shape = [jax.ShapeDtypeStruct((m, d), F32)]
    if emit_h:
        in_specs.append(_mod_spec(modn, rows_per_batch, tm))
        in_specs.append(pl.BlockSpec((1, d), lambda i, k: (0, 0)))
        args += [modn, gpre]
        out_specs.append(pl.BlockSpec((tm, d), lambda i, k: (i, 0)))
        out_shape.append(jax.ShapeDtypeStruct((m, d), BF16))
    outs = pl.pallas_call(
        functools.partial(_resid_kernel, seg_nk=seg_nk, nxc=nxc, weight=weight,
                          emit_h=emit_h, tm=tm, per_token=per_token),
        grid=(m // tm, nk),
        in_specs=in_specs,
        out_specs=out_specs,
        out_shape=out_shape,
        scratch_shapes=[pltpu.VMEM((tm, d), F32)],
        compiler_params=_cparams(2, 56),
        name="resid_proj",
    )(*args)
    return (outs[0], outs[1]) if emit_h else (outs[0], None)


def _inproj_kernel(h_ref, w_ref, o_ref, w_s):
    @pl.when(pl.program_id(1) == 0)
    def _():
        w_s[...] = w_ref[...].astype(BF16)

    o_ref[...] = jnp.dot(h_ref[...], w_s[...], preferred_element_type=F32)


def _in_proj(h, w_in, layer, *, tm):
    m, d = h.shape
    n = w_in.shape[-1]
    tn = 768
    return pl.pallas_call(
        _inproj_kernel,
        grid=(n // tn, m // tm),
        in_specs=[
            pl.BlockSpec((tm, d), lambda j, i: (i, 0)),
            pl.BlockSpec((None, d, tn), lambda j, i: (layer, 0, j)),
        ],
        out_specs=pl.BlockSpec((tm, tn), lambda j, i: (i, j)),
        out_shape=jax.ShapeDtypeStruct((m, n), F32),
        scratch_shapes=[pltpu.VMEM((d, tn), BF16)],
        compiler_params=_cparams(2, 48),
        name="in_proj",
    )(h, w_in)


QBLK = 128
PREP_ROWS = 512
BLOCK_UNROLL_DILATED = 8
BLOCK_UNROLL_GQA = 1
STAT_LANES = LANES // 4
NT_DIMS = (((1,), (1,)), ((), ()))


def _band_mask():
    row = lax.broadcasted_iota(jnp.int32, (QBLK, 2 * QBLK), 0)
    col = lax.broadcasted_iota(jnp.int32, (QBLK, 2 * QBLK), 1)
    return (col >= row) & (col <= row + HOPS), col >= QBLK


def _block_softmax(q_pieces, kb, vb, valid, fuse_den):
    s = lax.dot_general(jnp.concatenate(q_pieces, axis=0), kb, NT_DIMS,
                        preferred_element_type=F32)
    maxima, probs = [], []
    for i in range(len(q_pieces)):
        sp = jnp.where(valid, s[i * QBLK:(i + 1) * QBLK, :], NEG_INF)
        mb = jnp.max(sp, axis=1, keepdims=True)
        maxima.append(mb)
        probs.append(jnp.exp(sp - mb).astype(BF16))
    pb = jnp.concatenate(probs, axis=0)
    if fuse_den:
        both = jnp.dot(pb, jnp.concatenate([vb, jnp.ones_like(vb)], axis=1),
                       preferred_element_type=F32)
        return maxima, both[:, 0:LANES], both[:, LANES:2 * LANES]
    num = jnp.dot(pb, vb, preferred_element_type=F32)
    den = jnp.dot(pb, jnp.ones_like(vb), preferred_element_type=F32)
    return maxima, num, den


def _write_tails(kt_ref, vt_ref, k_tail, v_tail):
    for c in range(k_tail.shape[0] // QBLK):
        kt_ref[:, c * QBLK:(c + 1) * QBLK] = k_tail[c * QBLK:(c + 1) * QBLK, :].T
        vt_ref[:, c * QBLK:(c + 1) * QBLK] = v_tail[c * QBLK:(c + 1) * QBLK, :].T


def _dilated_kernel(q_ref, k_ref, v_ref, cos_ref, slo_ref, shi_ref, o_ref, kt_ref, vt_ref,
                    qs, kp, vp, *stat_scratch, dils, t, tail):
    pad = QBLK * max(dils)
    n_pat = len(dils)
    num_refs, stat_refs = stat_scratch[:n_pat], stat_scratch[n_pat:]
    lane = lax.broadcasted_iota(jnp.int32, (1, LANES), 1)
    low = lane < HEAD_DIM

    kp[0:pad, :] = jnp.zeros((pad, LANES), F32)
    vp[0:pad, :] = jnp.zeros((pad, LANES), F32)

    def prep(c, carry):
        r0 = pl.multiple_of(c * PREP_ROWS, PREP_ROWS)
        rows = pl.ds(r0, PREP_ROWS)
        cos, slo, shi = cos_ref[rows, :], slo_ref[rows, :], shi_ref[rows, :]
        qs[rows, :] = _rope(q_ref[rows, :], cos, slo, shi) * ATTN_SCALE
        kp[pl.ds(pad + r0, PREP_ROWS), :] = _rope(k_ref[rows, :], cos, slo, shi)
        vp[pl.ds(pad + r0, PREP_ROWS), :] = v_ref[rows, :]
        return carry

    lax.fori_loop(0, t // PREP_ROWS, prep, 0)
    _write_tails(kt_ref, vt_ref, kp[pad + t - tail:pad + t, :], v_ref[t - tail:t, :])

    band, current = _band_mask()
    for pi, d in enumerate(dils):
        shift = d.bit_length() - 1

        def block(b, carry, d=d, shift=shift, pi=pi):
            n = lax.shift_right_logical(b, jnp.int32(shift))
            q0 = n * (QBLK * d) + (b & (d - 1))
            k0 = q0 + pad - QBLK * d
            if d == 1:
                qsl = pl.ds(pl.multiple_of(q0, QBLK), QBLK)
                ksl = pl.ds(pl.multiple_of(k0, QBLK), 2 * QBLK)
            else:
                qsl = pl.ds(q0, QBLK, stride=d)
                ksl = pl.ds(k0, 2 * QBLK, stride=d)
            q = qs[qsl, :]
            pieces = [jnp.where(low, q, 0.0).astype(BF16),
                      jnp.where(low, 0.0, q).astype(BF16)]
            (m0, m1), num, den = _block_softmax(
                pieces, kp[ksl, :].astype(BF16), vp[ksl, :].astype(BF16),
                band & (current | (n > 0)), fuse_den=True)
            num_refs[pi][qsl, :] = jnp.where(low, num[0:QBLK, :], num[QBLK:2 * QBLK, :])
            stat_refs[pi][qsl, :] = jnp.where(
                lane < STAT_LANES, m0,
                jnp.where(low, den[0:QBLK, :],
                          jnp.where(lane < 3 * STAT_LANES, m1, den[QBLK:2 * QBLK, :])))
            return carry

        lax.fori_loop(0, t // QBLK, block, 0, unroll=BLOCK_UNROLL_DILATED)

    is_max = (lane & (HEAD_DIM - 1)) < STAT_LANES

    def finish(c, carry):
        rows = pl.ds(pl.multiple_of(c * PREP_ROWS, PREP_ROWS), PREP_ROWS)
        ms, ls = [], []
        for r in stat_refs:
            st = r[rows, :]
            ms.append(jnp.where(is_max, st, pltpu.roll(st, STAT_LANES, 1)))
            ls.append(jnp.where(is_max, pltpu.roll(st, LANES - STAT_LANES, 1), st))
        m = functools.reduce(jnp.maximum, ms)
        den = num = None
        for mi, li, r in zip(ms, ls, num_refs):
            w = jnp.exp(mi - m)
            den = w * li if den is None else den + w * li
            num = w * r[rows, :] if num is None else num + w * r[rows, :]
        o_ref[rows, :] = (num / den).astype(BF16)
        return carry

    lax.fori_loop(0, t // PREP_ROWS, finish, 0)


def _attention_dilated(u, tables, *, n_batch, t, dils, q_slab, k_slab, v_slab, n_slab, tail):
    pad = QBLK * max(dils)
    once = pl.Buffered(1)
    table = pl.BlockSpec((t, LANES), lambda b, s: (0, 0), pipeline_mode=once)
    tail_spec = pl.BlockSpec((None, LANES, tail), lambda b, s: (b, s, 0))
    scratch = [pltpu.VMEM((t, LANES), F32)]
    scratch += [pltpu.VMEM((pad + t, LANES), F32) for _ in range(2)]
    scratch += [pltpu.VMEM((t, LANES), F32) for _ in range(2 * len(dils))]
    return pl.pallas_call(
        functools.partial(_dilated_kernel, dils=dils, t=t, tail=tail),
        grid=(n_batch, n_slab),
        in_specs=[
            pl.BlockSpec((t, LANES), lambda b, s: (b, q_slab + s)),
            pl.BlockSpec((t, LANES), lambda b, s: (b, k_slab + s)),
            pl.BlockSpec((t, LANES), lambda b, s: (b, v_slab + s)),
            table, table, table,
        ],
        out_specs=[pl.BlockSpec((t, LANES), lambda b, s: (b, s)), tail_spec, tail_spec],
        out_shape=[
            jax.ShapeDtypeStruct((n_batch * t, n_slab * LANES), BF16),
            jax.ShapeDtypeStruct((n_batch, n_slab * LANES, tail), F32),
            jax.ShapeDtypeStruct((n_batch, n_slab * LANES, tail), F32),
        ],
        scratch_shapes=scratch,
        compiler_params=_cparams(2, 48),
        name="attn_dilated",
    )(u, u, u, *tables)


GQA_SLABS = 4


def _gqa_kernel(q_ref, k_ref, v_ref, cos_ref, slo_ref, shi_ref, sink_ref, o_ref, kt_ref, vt_ref,
                kp, vp, *, t, tail):
    pad = QBLK
    kv_head = pl.program_id(1)
    lane = lax.broadcasted_iota(jnp.int32, (1, LANES), 1)
    low = lane < HEAD_DIM
    keep = lax.shift_right_logical(lane, jnp.int32(6)) == kv_head

    kp[0:pad, :] = jnp.zeros((pad, LANES), BF16)
    vp[0:pad, :] = jnp.zeros((pad, LANES), BF16)

    def prep(c, carry):
        r0 = pl.multiple_of(c * PREP_ROWS, PREP_ROWS)
        rows = pl.ds(r0, PREP_ROWS)
        kr = _rope(k_ref[rows, :], cos_ref[rows, :], slo_ref[rows, :], shi_ref[rows, :])
        v = v_ref[rows, :]
        prow = pl.ds(pad + r0, PREP_ROWS)
        kp[prow, :] = jnp.where(keep, kr, pltpu.roll(kr, HEAD_DIM, 1)).astype(BF16)
        vp[prow, :] = jnp.where(keep, v, pltpu.roll(v, HEAD_DIM, 1)).astype(BF16)
        return carry

    lax.fori_loop(0, t // PREP_ROWS, prep, 0)
    trows = slice(t - tail, t)
    _write_tails(kt_ref, vt_ref,
                 _rope(k_ref[trows, :], cos_ref[trows, :], slo_ref[trows, :], shi_ref[trows, :]),
                 v_ref[trows, :])

    band, current = _band_mask()

    def block(b, carry):
        q0 = pl.multiple_of(b * QBLK, QBLK)
        rows = pl.ds(q0, QBLK)
        cos, slo, shi = cos_ref[rows, :], slo_ref[rows, :], shi_ref[rows, :]
        pieces = []
        for s in range(GQA_SLABS):
            q = _rope(q_ref[rows, s * LANES:(s + 1) * LANES], cos, slo, shi) * ATTN_SCALE
            pieces += [jnp.where(low, q, 0.0).astype(BF16), jnp.where(low, 0.0, q).astype(BF16)]
        ksl = pl.ds(q0, 2 * QBLK)
        maxima, num, den = _block_softmax(pieces, kp[ksl, :], vp[ksl, :],
                                          band & (current | (b > 0)), fuse_den=False)
        for s in range(GQA_SLABS):
            outs = []
            for hh in range(2):
                i = 2 * s + hh
                piece = slice(i * QBLK, (i + 1) * QBLK)
                sink = sink_ref[:, s * LANES + hh * HEAD_DIM:s * LANES + hh * HEAD_DIM + 1]
                outs.append(num[piece, :] / (den[piece, :] + jnp.exp(sink - maxima[i])))
            o_ref[rows, s * LANES:(s + 1) * LANES] = jnp.where(low, outs[0], outs[1]).astype(BF16)
        return carry

    lax.fori_loop(0, t // QBLK, block, 0, unroll=BLOCK_UNROLL_GQA)


def _attention_gqa(u, tables, sinks, *, n_batch, t, q_slab, k_slab, v_slab, tail):
    width = GQA_SLABS * LANES
    once = pl.Buffered(1)
    table = pl.BlockSpec((t, LANES), lambda b, g: (0, 0), pipeline_mode=once)
    tail_spec = pl.BlockSpec((None, LANES, tail), lambda b, g: (b, 0, 0))
    q_blk = q_slab // GQA_SLABS
    return pl.pallas_call(
        functools.partial(_gqa_kernel, t=t, tail=tail),
        grid=(n_batch, N_KV_B),
        in_specs=[
            pl.BlockSpec((t, width), lambda b, g: (b, q_blk + g)),
            pl.BlockSpec((t, LANES), lambda b, g: (b, k_slab)),
            pl.BlockSpec((t, LANES), lambda b, g: (b, v_slab)),
            table, table, table,
            pl.BlockSpec((None, 1, width), lambda b, g: (g, 0, 0)),
        ],
        out_specs=[pl.BlockSpec((t, width), lambda b, g: (b, g)), tail_spec, tail_spec],
        out_shape=[
            jax.ShapeDtypeStruct((n_batch * t, N_KV_B * width), BF16),
            jax.ShapeDtypeStruct((n_batch, LANES, tail), F32),
            jax.ShapeDtypeStruct((n_batch, LANES, tail), F32),
        ],
        scratch_shapes=[pltpu.VMEM((QBLK + t, LANES), BF16), pltpu.VMEM((QBLK + t, LANES), BF16)],
        compiler_params=_cparams(2, 48),
        name="attn_gqa",
    )(u, u, u, *tables, sinks)


CONV_TB = 256
CONV_HALO = 32
CONV_RC = 32


def _layer_norm_swish(y, g, b):
    mu = jnp.mean(y, axis=-1, keepdims=True)
    yc = y - mu
    var = jnp.mean(yc * yc, axis=-1, keepdims=True)
    z = yc * lax.rsqrt(var + LN_EPS) * g + b
    return z * _sigmoid(z)


def _conv_kernel(v0, v1, g0, g1, cw_ref, cb_ref, lg_ref, lb_ref, o_ref, st_ref, ext):
    tt = pl.program_id(1)
    half = D_CONV // 2

    @pl.when(tt == 0)
    def _():
        ext[0:CONV_HALO, :] = jnp.zeros((CONV_HALO, D_CONV), F32)

    @pl.when(tt > 0)
    def _():
        ext[0:CONV_HALO, :] = ext[CONV_TB:CONV_TB + CONV_HALO, :]

    ext[CONV_HALO:CONV_HALO + CONV_TB, 0:half] = v0[...] * _sigmoid(g0[...])
    ext[CONV_HALO:CONV_HALO + CONV_TB, half:D_CONV] = v1[...] * _sigmoid(g1[...])
    st_ref[...] = ext[CONV_TB:CONV_TB + CONV_HALO, :]

    lead = CONV_HALO - (CONV_WIDTH - 1)
    for c in range(CONV_TB // CONV_RC):
        r0 = c * CONV_RC
        y = jnp.zeros((CONV_RC, D_CONV), F32) + cb_ref[...]
        for w in range(CONV_WIDTH):
            y = y + ext[r0 + lead + w:r0 + lead + w + CONV_RC, :] * cw_ref[w:w + 1, :]
        o_ref[r0:r0 + CONV_RC, :] = _layer_norm_swish(y, lg_ref[...], lb_ref[...]).astype(BF16)


def _conv_module(u, cw, cb, lg, lb, *, n_batch, t):
    nt = t // CONV_TB
    half = D_CONV // 2
    v_blk, g_blk = COL_VAL // half, COL_GATE // half
    row = lambda b, i: b * nt + i
    vec = pl.BlockSpec((1, D_CONV), lambda b, i: (0, 0))
    return pl.pallas_call(
        _conv_kernel,
        grid=(n_batch, nt),
        in_specs=[
            pl.BlockSpec((CONV_TB, half), lambda b, i: (row(b, i), v_blk)),
            pl.BlockSpec((CONV_TB, half), lambda b, i: (row(b, i), v_blk + 1)),
            pl.BlockSpec((CONV_TB, half), lambda b, i: (row(b, i), g_blk)),
            pl.BlockSpec((CONV_TB, half), lambda b, i: (row(b, i), g_blk + 1)),
            pl.BlockSpec((CONV_HALO, D_CONV), lambda b, i: (0, 0)),
            vec, vec, vec,
        ],
        out_specs=[
            pl.BlockSpec((CONV_TB, D_CONV), lambda b, i: (row(b, i), 0)),
            pl.BlockSpec((CONV_HALO, D_CONV), lambda b, i: (b, 0)),
        ],
        out_shape=[
            jax.ShapeDtypeStruct((n_batch * t, D_CONV), BF16),
            jax.ShapeDtypeStruct((n_batch * CONV_HALO, D_CONV), F32),
        ],
        scratch_shapes=[pltpu.VMEM((CONV_HALO + CONV_TB, D_CONV), F32)],
        compiler_params=_cparams(2, 32),
        name="conv_module",
    )(u, u, u, u, cw, cb, lg, lb)


ROPE_SLABS = tuple(range(SLAB_QA, SLAB_VA)) + tuple(range(SLAB_QB, SLAB_VB))
Q_SLABS = tuple(range(SLAB_QA, SLAB_KA)) + tuple(range(SLAB_QB, SLAB_KB))
VAL_SLABS = tuple(range(COL_VAL // LANES, COL_GATE // LANES))


def _sample_post_kernel(u_ref, cos_ref, slo_ref, shi_ref, o_ref):
    for s in range(N_IN // LANES):
        x = u_ref[:, s * LANES:(s + 1) * LANES]
        if s in ROPE_SLABS:
            x = _rope(x, cos_ref[...], slo_ref[...], shi_ref[...])
            if s in Q_SLABS:
                x = x * ATTN_SCALE
        elif s in VAL_SLABS:
            gs = s + D_CONV // LANES
            x = x * _sigmoid(u_ref[:, gs * LANES:(gs + 1) * LANES])
        o_ref[:, s * LANES:(s + 1) * LANES] = x


def _sample_post(u, tables):
    m, n = u.shape
    full = pl.BlockSpec((m, n), lambda i: (0, 0))
    vec = pl.BlockSpec((1, LANES), lambda i: (0, 0))
    return pl.pallas_call(
        _sample_post_kernel,
        grid=(1,),
        in_specs=[full, vec, vec, vec],
        out_specs=full,
        out_shape=jax.ShapeDtypeStruct((m, n), F32),
        compiler_params=_cparams(1, 16),
        name="sample_post",
    )(u, *tables)


def _pattern_multiplicity(la):
    dist = la - jnp.arange(la, dtype=jnp.int32)
    mult = jnp.zeros((la,), F32)
    for d in DILATIONS_A:
        mult = mult + ((dist % d == 0) & (dist <= HOPS * d)).astype(F32)
    return mult.reshape(1, la)


def _sattn_shift_kernel(kt_ref, vt_ref, q_ref, knr_ref, vnr_ref, knc_ref, vnc_ref, mult_ref,
                        *rest):
    o_ref, ko_ref, vo_ref = rest[-3:]
    nh, e, la = kt_ref.shape
    w = nh * e
    k = kt_ref[...].reshape(w, la)
    v = vt_ref[...].reshape(w, la)
    q = q_ref[...]
    mult = mult_ref[...]
    s = jnp.dot(q.astype(BF16), k.astype(BF16), preferred_element_type=F32)
    s = jnp.where(mult > 0.0, s, NEG_INF)
    s_new = jnp.sum(q * knr_ref[...], axis=1, keepdims=True)
    m = jnp.maximum(jnp.max(s, axis=1, keepdims=True), s_new)
    p = jnp.exp(s - m) * mult
    p_new = jnp.exp(s_new - m) * float(len(DILATIONS_A))
    den = jnp.sum(p, axis=1, keepdims=True) + p_new
    num = lax.dot_general(p.astype(BF16), v.astype(BF16), NT_DIMS, preferred_element_type=F32)
    o_ref[...] = (num + p_new * vnr_ref[...]) / den
    newest = lax.broadcasted_iota(jnp.int32, (1, la), 1) == la - 1
    ko_ref[...] = jnp.where(newest, knc_ref[...], pltpu.roll(k, la - 1, 1)).reshape(nh, e, la)
    vo_ref[...] = jnp.where(newest, vnc_ref[...], pltpu.roll(v, la - 1, 1)).reshape(nh, e, la)


def _sample_attn_shift(kt, vt, prev, layer, q_bd, kn, vn, mult):
    depth, nb, nh, e, la = kt.shape
    w = nh * e
    slab = pl.BlockSpec((None, None, nh, e, la), lambda b: (layer, b, 0, 0, 0))
    tokq = pl.BlockSpec((None, nh, w), lambda b: (b, 0, 0))
    row = pl.BlockSpec((None, 1, w), lambda b: (b, 0, 0))
    col = pl.BlockSpec((None, w, 1), lambda b: (b, 0, 0))
    in_specs = [slab, slab, tokq, row, row, col, col, pl.BlockSpec((1, la), lambda b: (0, 0))]
    args = [kt, vt, q_bd, kn.reshape(nb, 1, w), vn.reshape(nb, 1, w),
            kn.reshape(nb, w, 1), vn.reshape(nb, w, 1), mult]
    aliases = {}
    if prev is not None:
        aliases = {len(args): 1, len(args) + 1: 2}
        in_specs += [pl.BlockSpec(memory_space=pl.ANY)] * 2
        args += list(prev)
    return pl.pallas_call(
        _sattn_shift_kernel,
        grid=(nb,),
        in_specs=in_specs,
        out_specs=[tokq, slab, slab],
        out_shape=[jax.ShapeDtypeStruct((nb, nh, w), F32),
                   jax.ShapeDtypeStruct(kt.shape, F32), jax.ShapeDtypeStruct(vt.shape, F32)],
        input_output_aliases=aliases,
        compiler_params=_cparams(1, 56),
        name="sample_attn_shift",
    )(*args)


def _sattn_b_kernel(kc_ref, vc_ref, q_ref, kn_ref, vn_ref, sink_ref, o_ref):
    q = q_ref[...]
    s = lax.dot_general(q.astype(BF16), kc_ref[...].astype(BF16), NT_DIMS,
                        preferred_element_type=F32)
    s_new = jnp.sum(q * kn_ref[...], axis=1, keepdims=True)
    m = jnp.maximum(jnp.max(s, axis=1, keepdims=True), s_new)
    p = jnp.exp(s - m)
    p_new = jnp.exp(s_new - m)
    den = jnp.sum(p, axis=1, keepdims=True) + p_new + jnp.exp(sink_ref[...] - m)
    num = jnp.dot(p.astype(BF16), vc_ref[...].astype(BF16), preferred_element_type=F32)
    o_ref[...] = (num + p_new * vn_ref[...]) / den


def _sample_attn_b(kc, vc, layer, q_exp, kn, vn, sinks):
    _, nb, lb, w = kc.shape
    nh = q_exp.shape[1]
    cache = pl.BlockSpec((None, None, lb, w), lambda b: (layer, b, 0, 0))
    tokq = pl.BlockSpec((None, nh, w), lambda b: (b, 0, 0))
    tok1 = pl.BlockSpec((None, 1, w), lambda b: (b, 0, 0))
    return pl.pallas_call(
        _sattn_b_kernel,
        grid=(nb,),
        in_specs=[cache, cache, tokq, tok1, tok1, pl.BlockSpec((nh, 1), lambda b: (0, 0))],
        out_specs=tokq,
        out_shape=jax.ShapeDtypeStruct((nb, nh, w), F32),
        compiler_params=_cparams(1, 16),
        name="sample_attn_shared",
    )(kc, vc, q_exp, kn, vn, sinks)


def _sconv_kernel(st_ref, glu_ref, cw_ref, cb_ref, lg_ref, lb_ref, o_ref, ns_ref):
    hist = CONV_WIDTH - 1
    glu = glu_ref[...]
    y = glu * cw_ref[hist:hist + 1, :] + cb_ref[...]
    for w in range(hist):
        y = y + st_ref[:, w * D_CONV:(w + 1) * D_CONV] * cw_ref[w:w + 1, :]
    o_ref[...] = _layer_norm_swish(y, lg_ref[...], lb_ref[...]).astype(BF16)
    ns_ref[:, 0:(hist - 1) * D_CONV] = st_ref[:, D_CONV:hist * D_CONV]
    ns_ref[:, (hist - 1) * D_CONV:hist * D_CONV] = glu


def _sample_conv(state2d, layer, glu, cw, cb, lg, lb):
    _, nb, width = state2d.shape
    st = pl.BlockSpec((None, nb, width), lambda i: (layer, 0, 0))
    tok = pl.BlockSpec((nb, D_CONV), lambda i: (0, 0))
    vec = pl.BlockSpec((1, D_CONV), lambda i: (0, 0))
    return pl.pallas_call(
        _sconv_kernel,
        grid=(1,),
        in_specs=[st, tok, pl.BlockSpec((CONV_HALO, D_CONV), lambda i: (0, 0)), vec, vec, vec],
        out_specs=[tok, pl.BlockSpec((nb, width), lambda i: (0, 0))],
        out_shape=[jax.ShapeDtypeStruct((nb, D_CONV), BF16),
                   jax.ShapeDtypeStruct((nb, width), F32)],
        compiler_params=_cparams(1, 16),
        name="sample_conv",
    )(state2d, glu, cw, cb, lg, lb)


def _shift_kernel(km, kx, kn, vm, vx, vn, ko, vo):
    last = pl.program_id(2) == pl.num_programs(2) - 1
    for main, nxt, new, out in ((km, kx, kn, ko), (vm, vx, vn, vo)):
        rb = main.shape[1]
        out[:, 0:rb - 1] = main[:, 1:rb]
        out[:, rb - 1:rb] = jnp.where(last, new[...], nxt[...])


def _shift_caches(cache_k, cache_v, new_k, new_v, *, batch_chunk, row_block):
    depth, nb, rows, nh, e = cache_k.shape
    main = pl.BlockSpec((None, batch_chunk, row_block, nh, e), lambda l, b, r: (l, b, r, 0, 0))
    nxt = pl.BlockSpec((None, batch_chunk, 1, nh, e),
                       lambda l, b, r: (l, b, jnp.minimum((r + 1) * row_block, rows - 1), 0, 0))
    new = pl.BlockSpec((None, batch_chunk, 1, nh, e), lambda l, b, r: (l, b, 0, 0, 0))
    shape = jax.ShapeDtypeStruct(cache_k.shape, cache_k.dtype)
    return pl.pallas_call(
        _shift_kernel,
        grid=(depth, nb // batch_chunk, rows // row_block),
        in_specs=[main, nxt, new, main, nxt, new],
        out_specs=[main, main],
        out_shape=[shape, shape],
        compiler_params=_cparams(3, 48),
        name="cache_shift",
    )(cache_k, cache_k, new_k, cache_v, cache_v, new_v)


def _rope_tables(pos):
    n = pos.shape[0]
    inv_freq = 1.0 / (ROPE_THETA ** (jnp.arange(ROT_HALF, dtype=F32) / ROT_HALF))
    ang = pos.astype(F32)[:, None] * inv_freq[None, :]
    cos, sin = jnp.cos(ang), jnp.sin(ang)
    rest = HEAD_DIM - 2 * ROT_HALF
    c = jnp.concatenate([cos, cos, jnp.ones((n, rest), F32)], axis=1)
    lo = jnp.concatenate([-sin, jnp.zeros((n, HEAD_DIM - ROT_HALF), F32)], axis=1)
    hi = jnp.concatenate([jnp.zeros((n, ROT_HALF), F32), sin, jnp.zeros((n, rest), F32)], axis=1)
    return tuple(jnp.tile(x, (1, LANES // HEAD_DIM)) for x in (c, lo, hi))


def kernel(x_prompt, x_sample, c_prompt, c_sample, cache_a_k, cache_a_v, cache_b_k, cache_b_v,
           state_c_conv, w_ada, b_ada, g_pre, g_post, w_ffn_gu, w_ffn_down, w_in, w_out,
           attn_sinks, conv_w, conv_b, conv_ln_g, conv_ln_b):
    nbp, t, d = x_prompt.shape
    nbs = x_sample.shape[0]
    mp = nbp * t
    depth = w_ada.shape[0]
    la, lb = cache_a_k.shape[2], cache_b_k.shape[2]
    hist = CONV_WIDTH - 1
    tm_p = 1024

    xp = x_prompt.reshape(mp, d)
    xs = x_sample.reshape(nbs, d)

    s_row = 32
    c_all = jnp.zeros((s_row + nbs, d), F32).at[:nbp].set(c_prompt).at[s_row:].set(c_sample)
    mod_all = _ada_mod(c_all, w_ada, b_ada)

    def mod_p(l, s):
        return mod_all[l, :nbp, 3 * s * d:3 * (s + 1) * d].reshape(nbp, 3, 1, d)

    def mod_s(l, s):
        m = mod_all[l, s_row:, 3 * s * d:3 * (s + 1) * d].reshape(nbs, 3, d)
        return jnp.transpose(m, (1, 0, 2))[None]

    def nxt(l, s):
        return (l, s + 1) if s + 1 < N_SUB else (l + 1, 0)

    tab_p = _rope_tables(jnp.arange(t, dtype=jnp.int32))
    tab_s = _rope_tables(PAST_LEN + jnp.arange(1, dtype=jnp.int32))
    sink_groups = jnp.repeat(attn_sinks, HEAD_DIM, axis=1).reshape(depth, N_KV_B, 1, -1)
    sink_cols = attn_sinks.reshape(depth, N_HEADS_B, 1)
    cw_pad = jnp.pad(conv_w, ((0, 0), (0, CONV_HALO - CONV_WIDTH), (0, 0)))
    vec = lambda a, l: a[l].reshape(1, -1)

    cache_bk2 = cache_b_k.reshape(depth, nbs, lb, N_KV_B * HEAD_DIM)
    cache_bv2 = cache_b_v.reshape(depth, nbs, lb, N_KV_B * HEAD_DIM)
    state2d = state_c_conv.reshape(depth, nbs, hist * D_CONV)
    kv_of_head = (jnp.arange(N_HEADS_B) // (N_HEADS_B // N_KV_B))[:, None] == jnp.arange(N_KV_B)
    cache_akt = jnp.transpose(cache_a_k, (0, 1, 3, 4, 2))
    cache_avt = jnp.transpose(cache_a_v, (0, 1, 3, 4, 2))
    mult_a = _pattern_multiplicity(la)
    head_eye = jnp.eye(N_HEADS_A, dtype=bool)[None, :, :, None]

    hp = _prenorm(xp, mod_p(0, 0), vec(g_pre[0], 0), tm=tm_p, rows_per_batch=t)
    hs = _prenorm(xs, mod_s(0, 0), vec(g_pre[0], 0), tm=nbs, rows_per_batch=nbs)

    st_p = [[] for _ in range(5)]
    new_rows = [[], []]
    shift_a = None
    st_c_s = []

    def resid(lhs, w, lead, x, l, s, weight, mod_fn, tm, rpb):
        ln, sn = nxt(l, s)
        last = ln >= depth
        return _resid_proj(
            lhs, w, lead, x, mod_fn(l, s), vec(g_post[l], s),
            None if last else mod_fn(ln, sn), None if last else vec(g_pre[ln], sn),
            tm=tm, rows_per_batch=rpb, weight=weight)

    for l in range(depth):
        a = _ffn_up(hp, w_ffn_gu, (l, 0), tm=tm_p)
        xp, hp = resid([a], w_ffn_down, (l, 0), xp, l, 0, 0.5, mod_p, tm_p, t)
        u = _in_proj(hp, w_in, l, tm=tm_p)
        oa, kta, vta = _attention_dilated(
            u, tab_p, n_batch=nbp, t=t, dils=DILATIONS_A, q_slab=SLAB_QA, k_slab=SLAB_KA,
            v_slab=SLAB_VA, n_slab=N_HEADS_A // 2, tail=min(la, t))
        ob, ktb, vtb = _attention_gqa(
            u, tab_p, sink_groups[l], n_batch=nbp, t=t, q_slab=SLAB_QB, k_slab=SLAB_KB,
            v_slab=SLAB_VB, tail=min(lb, t))
        oc, cst = _conv_module(u, cw_pad[l], vec(conv_b, l), vec(conv_ln_g, l), vec(conv_ln_b, l),
                               n_batch=nbp, t=t)
        xp, hp = resid([oa, ob, oc], w_out, (l,), xp, l, 1, 1.0, mod_p, tm_p, t)
        a = _ffn_up(hp, w_ffn_gu, (l, 1), tm=tm_p)
        xp, hp = resid([a], w_ffn_down, (l, 1), xp, l, 2, 0.5, mod_p, tm_p, t)

        for dst, tail_t, heads in ((st_p[0], kta, N_HEADS_A), (st_p[1], vta, N_HEADS_A),
                                   (st_p[2], ktb, N_KV_B), (st_p[3], vtb, N_KV_B)):
            dst.append(jnp.transpose(tail_t.reshape(nbp, heads, HEAD_DIM, -1), (0, 3, 1, 2)))
        st_p[4].append(cst.reshape(nbp, CONV_HALO, D_CONV)[:, CONV_HALO - hist:])

        a = _ffn_up(hs, w_ffn_gu, (l, 0), tm=nbs)
        xs, hs = resid([a], w_ffn_down, (l, 0), xs, l, 0, 0.5, mod_s, nbs, nbs)
        ur = _sample_post(_in_proj(hs, w_in, l, tm=nbs), tab_s)
        seg = lambda lo, hi: ur[:, lo * LANES:hi * LANES]
        qa = seg(SLAB_QA, SLAB_KA).reshape(nbs, 1, N_HEADS_A, HEAD_DIM)
        q_bd = jnp.where(head_eye, qa, 0.0).reshape(nbs, N_HEADS_A, N_HEADS_A * HEAD_DIM)
        oa3, *shift_a = _sample_attn_shift(cache_akt, cache_avt, shift_a, l, q_bd,
                                           seg(SLAB_KA, SLAB_VA), seg(SLAB_VA, SLAB_QB), mult_a)
        oa = jnp.sum(jnp.where(head_eye, oa3.reshape(nbs, N_HEADS_A, N_HEADS_A, HEAD_DIM), 0.0),
                     axis=1)
        qb = seg(SLAB_QB, SLAB_KB).reshape(nbs, N_HEADS_B, 1, HEAD_DIM)
        q_exp = jnp.where(kv_of_head[None, :, :, None], qb, 0.0).reshape(nbs, N_HEADS_B, LANES)
        knb = seg(SLAB_KB, SLAB_VB)
        vnb = seg(SLAB_VB, SLAB_VB + 1)
        ob2 = _sample_attn_b(cache_bk2, cache_bv2, l, q_exp, knb.reshape(nbs, 1, LANES),
                             vnb.reshape(nbs, 1, LANES), sink_cols[l])
        ob = jnp.sum(jnp.where(kv_of_head[None, :, :, None],
                               ob2.reshape(nbs, N_HEADS_B, N_KV_B, HEAD_DIM), 0.0), axis=2)
        glu = ur[:, COL_VAL:COL_GATE]
        oc, ns = _sample_conv(state2d, l, glu, cw_pad[l], vec(conv_b, l), vec(conv_ln_g, l),
                              vec(conv_ln_b, l))
        mix = [oa.reshape(nbs, -1).astype(BF16), ob.reshape(nbs, -1).astype(BF16), oc]
        xs, hs = resid(mix, w_out, (l,), xs, l, 1, 1.0, mod_s, nbs, nbs)
        a = _ffn_up(hs, w_ffn_gu, (l, 1), tm=nbs)
        xs, hs = resid([a], w_ffn_down, (l, 1), xs, l, 2, 0.5, mod_s, nbs, nbs)

        new_rows[0].append(knb.reshape(nbs, 1, N_KV_B, HEAD_DIM))
        new_rows[1].append(vnb.reshape(nbs, 1, N_KV_B, HEAD_DIM))
        st_c_s.append(ns.reshape(nbs, hist, D_CONV))

    new_rows = [jnp.stack(r) for r in new_rows]
    shift_a = [jnp.transpose(c, (0, 1, 4, 2, 3)) for c in shift_a]
    shift_b = _shift_caches(cache_b_k, cache_b_v, new_rows[0], new_rows[1],
                            batch_chunk=8, row_block=lb)
    return (xp.reshape(nbp, t, d), xs.reshape(nbs, 1, d),
            *[jnp.stack(s) for s in st_p], *shift_a, *shift_b, jnp.stack(st_c_s))
```

```python
import functools

import jax
import jax.numpy as jnp
from jax import lax
from jax.experimental import pallas as pl
from jax.experimental.pallas import tpu as pltpu

F32 = jnp.float32
BF16 = jnp.bfloat16

D_MODEL = 2048
DEPTH = 4
HEAD_DIM = 64
N_HEADS_A = 8
N_HEADS_B = 16
N_KV_B = 2
D_CONV = 512
DILATIONS_A = (1, 4, 16)
HOPS = 128
CONV_WIDTH = 31
ROT_HALF = 8
ROPE_THETA = 500000.0
PAST_LEN = 16384
RMS_EPS = 1e-6
LN_EPS = 1e-5
NEG_INF = -1e30
ATTN_SCALE = HEAD_DIM ** -0.5
N_SUB = 3

LANES = 128
MIB = 1 << 20

SLAB_QA, SLAB_KA, SLAB_VA = 0, 4, 8
SLAB_QB, SLAB_KB, SLAB_VB = 12, 20, 21
COL_VAL, COL_GATE = 2816, 3328
N_IN = 3840


def _cparams(n_axes, vmem_mib):
    return pltpu.CompilerParams(
        dimension_semantics=("arbitrary",) * n_axes,
        vmem_limit_bytes=vmem_mib * MIB,
    )


def _sigmoid(x):
    return jax.nn.sigmoid(x)


def _pre_norm(x, g, shift, scale):
    ms = jnp.mean(x * x, axis=-1, keepdims=True)
    return (x * lax.rsqrt(ms + RMS_EPS)) * g * (1.0 + scale) + shift


def _gated_post(y, g, gate, weight):
    ms = jnp.mean(y * y, axis=-1, keepdims=True)
    return (weight * gate) * ((y * lax.rsqrt(ms + RMS_EPS)) * g)


def _rope(x, cos, sin_lo, sin_hi):
    return x * cos + pltpu.roll(x, LANES - ROT_HALF, 1) * sin_lo + pltpu.roll(x, ROT_HALF, 1) * sin_hi


def _ada_kernel(c_ref, w_ref, b_ref, o_ref):
    c = c_ref[...]
    a = (c * _sigmoid(c)).astype(BF16)
    o_ref[...] = jnp.dot(a, w_ref[...].astype(BF16), preferred_element_type=F32) + b_ref[...]


def _ada_mod(c_all, w_ada, b_ada):
    depth, d, n = w_ada.shape
    rows = c_all.shape[0]
    tn = 1024
    return pl.pallas_call(
        _ada_kernel,
        grid=(depth, n // tn),
        in_specs=[
            pl.BlockSpec((rows, d), lambda l, j: (0, 0)),
            pl.BlockSpec((None, d, tn), lambda l, j: (l, 0, j)),
            pl.BlockSpec((None, 1, tn), lambda l, j: (l, 0, j)),
        ],
        out_specs=pl.BlockSpec((None, rows, tn), lambda l, j: (l, 0, j)),
        out_shape=jax.ShapeDtypeStruct((depth, rows, n), F32),
        compiler_params=_cparams(2, 40),
        name="ada_mod",
    )(c_all, w_ada, b_ada.reshape(depth, 1, n))


def _prenorm_kernel(x_ref, mod_ref, g_ref, h_ref):
    h_ref[...] = _pre_norm(x_ref[...], g_ref[...], mod_ref[0], mod_ref[1]).astype(BF16)


def _mod_spec(mod, rows_per_batch, tm):
    _, _, r, d = mod.shape
    if r == 1:
        per = rows_per_batch // tm
        return pl.BlockSpec((None, 3, 1, d), lambda i, *_: (i // per, 0, 0, 0))
    return pl.BlockSpec((None, 3, r, d), lambda i, *_: (0, 0, 0, 0))


def _prenorm(x, mod, g, *, tm, rows_per_batch):
    m, d = x.shape
    return pl.pallas_call(
        _prenorm_kernel,
        grid=(m // tm,),
        in_specs=[
            pl.BlockSpec((tm, d), lambda i: (i, 0)),
            _mod_spec(mod, rows_per_batch, tm),
            pl.BlockSpec((1, d), lambda i: (0, 0)),
        ],
        out_specs=pl.BlockSpec((tm, d), lambda i: (i, 0)),
        out_shape=jax.ShapeDtypeStruct((m, d), BF16),
        compiler_params=_cparams(1, 32),
        name="prenorm",
    )(x, mod, g)


W_BLK = 128
W_STREAMS = 4
TF = W_BLK * W_STREAMS


def _ffn_up_kernel(h_ref, *refs, f_valid):
    g_refs = refs[0:W_STREAMS]
    u_refs = refs[W_STREAMS:2 * W_STREAMS]
    a_ref, wg_s, wu_s = refs[2 * W_STREAMS:]
    j = pl.program_id(0)

    @pl.when(pl.program_id(1) == 0)
    def _():
        for q in range(W_STREAMS):
            wg_s[:, q * W_BLK:(q + 1) * W_BLK] = g_refs[q][...].astype(BF16)
            wu_s[:, q * W_BLK:(q + 1) * W_BLK] = u_refs[q][...].astype(BF16)

    h = h_ref[...]
    g = jnp.dot(h, wg_s[...], preferred_element_type=F32)
    u = jnp.dot(h, wu_s[...], preferred_element_type=F32)
    a = (g * _sigmoid(g)) * u
    col = lax.broadcasted_iota(jnp.int32, a.shape, 1)
    a_ref[...] = jnp.where(col < f_valid - j * TF, a, 0.0).astype(BF16)


def _ffn_up(h, w_gu, lead, *, tm):
    m, d = h.shape
    f = w_gu.shape[-1] // 2
    squeezed = (None,) * len(lead)
    nb = f // W_BLK
    nj = pl.cdiv(f, TF)

    def wspec(base, q):
        return pl.BlockSpec(
            squeezed + (d, W_BLK),
            lambda j, i: lead + (0, base + jnp.minimum(W_STREAMS * j + q, nb - 1)))

    in_specs = [pl.BlockSpec((tm, d), lambda j, i: (i, 0))]
    in_specs += [wspec(0, q) for q in range(W_STREAMS)]
    in_specs += [wspec(nb, q) for q in range(W_STREAMS)]
    return pl.pallas_call(
        functools.partial(_ffn_up_kernel, f_valid=f),
        grid=(nj, m // tm),
        in_specs=in_specs,
        out_specs=pl.BlockSpec((tm, TF), lambda j, i: (i, j)),
        out_shape=jax.ShapeDtypeStruct((m, nj * TF), BF16),
        scratch_shapes=[pltpu.VMEM((d, TF), BF16), pltpu.VMEM((d, TF), BF16)],
        compiler_params=_cparams(2, 48),
        name="ffn_up",
    )(h, *([w_gu] * (2 * W_STREAMS)))


EPI_ROWS = 256


def _cast_kernel(w_ref, o_ref, *, k_valid):
    rows = w_ref.shape[0]
    row = lax.broadcasted_iota(jnp.int32, w_ref.shape, 0)
    o_ref[...] = jnp.where(row < k_valid - pl.program_id(1) * rows, w_ref[...], 0.0).astype(BF16)


def _cast_pad_rows(w, k_pad):
    n, k, d = w.shape
    last = k // W_BLK - 1
    return pl.pallas_call(
        functools.partial(_cast_kernel, k_valid=k),
        grid=(n, k_pad // W_BLK),
        in_specs=[pl.BlockSpec((None, W_BLK, d), lambda i, j: (i, jnp.minimum(j, last), 0))],
        out_specs=pl.BlockSpec((None, W_BLK, d), lambda i, j: (i, j, 0)),
        out_shape=jax.ShapeDtypeStruct((n, k_pad, d), BF16),
        compiler_params=_cparams(2, 16),
        name="cast_weights",
    )(w)


def _resid_kernel(*refs, nk, nxc, weight, emit_h, tm, per_token):
    it = iter(refs)
    lhs_ref, w_ref, xin_ref, modc_ref, gpost_ref = [next(it) for _ in range(5)]
    modn_ref = gpre_ref = None
    if emit_h:
        modn_ref, gpre_ref = next(it), next(it)
    x_out = next(it)
    h_out = next(it) if emit_h else None
    acc = next(it)
    k = pl.program_id(1)

    if nxc > 1:
        xr = tm // nxc

        @pl.when(k < nxc)
        def _():
            x_out[pl.ds(pl.multiple_of(k * xr, xr), xr), :] = xin_ref[...]

    @pl.when(k == 0)
    def _():
        acc[...] = jnp.dot(lhs_ref[...], w_ref[...], preferred_element_type=F32)

    @pl.when(k > 0)
    def _():
        acc[...] += jnp.dot(lhs_ref[...], w_ref[...], preferred_element_type=F32)

    def epilogue(rows):
        y = acc[rows, :]
        x = xin_ref[rows, :] if nxc == 1 else x_out[rows, :]
        if per_token:
            gate = modc_ref[2, rows, :]
        else:
            gate = modc_ref[2]
        xn = x + _gated_post(y, gpost_ref[...], gate, weight)
        x_out[rows, :] = xn
        if emit_h:
            if per_token:
                shift, scale = modn_ref[0, rows, :], modn_ref[1, rows, :]
            else:
                shift, scale = modn_ref[0], modn_ref[1]
            h_out[rows, :] = _pre_norm(xn, gpre_ref[...], shift, scale).astype(BF16)

    @pl.when(k == nk - 1)
    def _():
        if tm <= EPI_ROWS:
            epilogue(slice(None))
        else:
            def body(c, carry):
                epilogue(pl.ds(pl.multiple_of(c * EPI_ROWS, EPI_ROWS), EPI_ROWS))
                return carry
            lax.fori_loop(0, tm // EPI_ROWS, body, 0)


def _resid_proj(lhs, w, mat, x, modc, gpost, modn, gpre, *, tm, tk, rows_per_batch, weight):
    m, d = x.shape
    emit_h = modn is not None
    nk = lhs.shape[1] // tk
    per_token = modc.shape[2] != 1
    nxc = min(nk, 4) if tm >= 512 else 1

    in_specs = [
        pl.BlockSpec((tm, tk), lambda i, k: (i, k)),
        pl.BlockSpec((None, tk, d), lambda i, k: (mat, k, 0)),
    ]
    if nxc > 1:
        in_specs.append(pl.BlockSpec(
            (tm // nxc, d), lambda i, k: (i * nxc + jnp.minimum(k, nxc - 1), 0)))
    else:
        in_specs.append(pl.BlockSpec((tm, d), lambda i, k: (i, 0)))
    in_specs.append(_mod_spec(modc, rows_per_batch, tm))
    in_specs.append(pl.BlockSpec((1, d), lambda i, k: (0, 0)))
    args = [lhs, w, x, modc, gpost]
    out_specs = [pl.BlockSpec((tm, d), lambda i, k: (i, 0))]
    out_shape = [jax.ShapeDtypeStruct((m, d), F32)]
    if emit_h:
        in_specs.append(_mod_spec(modn, rows_per_batch, tm))
        in_specs.append(pl.BlockSpec((1, d), lambda i, k: (0, 0)))
        args += [modn, gpre]
        out_specs.append(pl.BlockSpec((tm, d), lambda i, k: (i, 0)))
        out_shape.append(jax.ShapeDtypeStruct((m, d), BF16))
    outs = pl.pallas_call(
        functools.partial(_resid_kernel, nk=nk, nxc=nxc, weight=weight,
                          emit_h=emit_h, tm=tm, per_token=per_token),
        grid=(m // tm, nk),
        in_specs=in_specs,
        out_specs=out_specs,
        out_shape=out_shape,
        scratch_shapes=[pltpu.VMEM((tm, d), F32)],
        compiler_params=_cparams(2, 56),
        name="resid_proj",
    )(*args)
    return (outs[0], outs[1]) if emit_h else (outs[0], None)


def _inproj_kernel(h_ref, w_ref, o_ref, w_s):
    @pl.when(pl.program_id(1) == 0)
    def _():
        w_s[...] = w_ref[...].astype(BF16)

    o_ref[...] = jnp.dot(h_ref[...], w_s[...], preferred_element_type=F32)


def _in_proj(h, w_in, layer, *, tm):
    m, d = h.shape
    n = w_in.shape[-1]
    tn = 768
    return pl.pallas_call(
        _inproj_kernel,
        grid=(n // tn, m // tm),
        in_specs=[
            pl.BlockSpec((tm, d), lambda j, i: (i, 0)),
            pl.BlockSpec((None, d, tn), lambda j, i: (layer, 0, j)),
        ],
        out_specs=pl.BlockSpec((tm, tn), lambda j, i: (i, j)),
        out_shape=jax.ShapeDtypeStruct((m, n), F32),
        scratch_shapes=[pltpu.VMEM((d, tn), BF16)],
        compiler_params=_cparams(2, 48),
        name="in_proj",
    )(h, w_in)


QBLK = 128
PREP_ROWS = 512
BLOCK_UNROLL_DILATED = 8
BLOCK_UNROLL_GQA = 1
STAT_LANES = LANES // 4
NT_DIMS = (((1,), (1,)), ((), ()))


def _band_mask():
    row = lax.broadcasted_iota(jnp.int32, (QBLK, 2 * QBLK), 0)
    col = lax.broadcasted_iota(jnp.int32, (QBLK, 2 * QBLK), 1)
    return (col >= row) & (col <= row + HOPS), col >= QBLK


def _block_softmax(q_pieces, kb, vb, valid, fuse_den):
    s = lax.dot_general(jnp.concatenate(q_pieces, axis=0), kb, NT_DIMS,
                        preferred_element_type=F32)
    maxima, probs = [], []
    for i in range(len(q_pieces)):
        sp = jnp.where(valid, s[i * QBLK:(i + 1) * QBLK, :], NEG_INF)
        mb = jnp.max(sp, axis=1, keepdims=True)
        maxima.append(mb)
        probs.append(jnp.exp(sp - mb).astype(BF16))
    pb = jnp.concatenate(probs, axis=0)
    if fuse_den:
        both = jnp.dot(pb, jnp.concatenate([vb, jnp.ones_like(vb)], axis=1),
                       preferred_element_type=F32)
        return maxima, both[:, 0:LANES], both[:, LANES:2 * LANES]
    num = jnp.dot(pb, vb, preferred_element_type=F32)
    den = jnp.dot(pb, jnp.ones_like(vb), preferred_element_type=F32)
    return maxima, num, den


def _write_tails(kt_ref, vt_ref, k_tail, v_tail):
    for c in range(k_tail.shape[0] // QBLK):
        kt_ref[:, c * QBLK:(c + 1) * QBLK] = k_tail[c * QBLK:(c + 1) * QBLK, :].T
        vt_ref[:, c * QBLK:(c + 1) * QBLK] = v_tail[c * QBLK:(c + 1) * QBLK, :].T


def _dilated_kernel(q_ref, k_ref, v_ref, cos_ref, slo_ref, shi_ref, o_ref, kt_ref, vt_ref,
                    qs, kp, vp, *stat_scratch, dils, t, tail):
    pad = QBLK * max(dils)
    n_pat = len(dils)
    num_refs, stat_refs = stat_scratch[:n_pat], stat_scratch[n_pat:]
    lane = lax.broadcasted_iota(jnp.int32, (1, LANES), 1)
    low = lane < HEAD_DIM

    kp[0:pad, :] = jnp.zeros((pad, LANES), F32)
    vp[0:pad, :] = jnp.zeros((pad, LANES), F32)

    def prep(c, carry):
        r0 = pl.multiple_of(c * PREP_ROWS, PREP_ROWS)
        rows = pl.ds(r0, PREP_ROWS)
        cos, slo, shi = cos_ref[rows, :], slo_ref[rows, :], shi_ref[rows, :]
        qs[rows, :] = _rope(q_ref[rows, :], cos, slo, shi) * ATTN_SCALE
        kp[pl.ds(pad + r0, PREP_ROWS), :] = _rope(k_ref[rows, :], cos, slo, shi)
        vp[pl.ds(pad + r0, PREP_ROWS), :] = v_ref[rows, :]
        return carry

    lax.fori_loop(0, t // PREP_ROWS, prep, 0)
    _write_tails(kt_ref, vt_ref, kp[pad + t - tail:pad + t, :], v_ref[t - tail:t, :])

    band, current = _band_mask()
    for pi, d in enumerate(dils):
        shift = d.bit_length() - 1

        def block(b, carry, d=d, shift=shift, pi=pi):
            n = lax.shift_right_logical(b, jnp.int32(shift))
            q0 = n * (QBLK * d) + (b & (d - 1))
            k0 = q0 + pad - QBLK * d
            if d == 1:
                qsl = pl.ds(pl.multiple_of(q0, QBLK), QBLK)
                ksl = pl.ds(pl.multiple_of(k0, QBLK), 2 * QBLK)
            else:
                qsl = pl.ds(q0, QBLK, stride=d)
                ksl = pl.ds(k0, 2 * QBLK, stride=d)
            q = qs[qsl, :]
            pieces = [jnp.where(low, q, 0.0).astype(BF16),
                      jnp.where(low, 0.0, q).astype(BF16)]
            (m0, m1), num, den = _block_softmax(
                pieces, kp[ksl, :].astype(BF16), vp[ksl, :].astype(BF16),
                band & (current | (n > 0)), fuse_den=True)
            num_refs[pi][qsl, :] = jnp.where(low, num[0:QBLK, :], num[QBLK:2 * QBLK, :])
            stat_refs[pi][qsl, :] = jnp.where(
                lane < STAT_LANES, m0,
                jnp.where(low, den[0:QBLK, :],
                          jnp.where(lane < 3 * STAT_LANES, m1, den[QBLK:2 * QBLK, :])))
            return carry

        lax.fori_loop(0, t // QBLK, block, 0, unroll=BLOCK_UNROLL_DILATED)

    is_max = (lane & (HEAD_DIM - 1)) < STAT_LANES

    def finish(c, carry):
        rows = pl.ds(pl.multiple_of(c * PREP_ROWS, PREP_ROWS), PREP_ROWS)
        ms, ls = [], []
        for r in stat_refs:
            st = r[rows, :]
            ms.append(jnp.where(is_max, st, pltpu.roll(st, STAT_LANES, 1)))
            ls.append(jnp.where(is_max, pltpu.roll(st, LANES - STAT_LANES, 1), st))
        m = functools.reduce(jnp.maximum, ms)
        den = num = None
        for mi, li, r in zip(ms, ls, num_refs):
            w = jnp.exp(mi - m)
            den = w * li if den is None else den + w * li
            num = w * r[rows, :] if num is None else num + w * r[rows, :]
        o_ref[rows, :] = (num / den).astype(BF16)
        return carry

    lax.fori_loop(0, t // PREP_ROWS, finish, 0)


def _attention_dilated(u, tables, *, n_batch, t, dils, q_slab, k_slab, v_slab, n_slab, tail):
    pad = QBLK * max(dils)
    once = pl.Buffered(1)
    table = pl.BlockSpec((t, LANES), lambda b, s: (0, 0), pipeline_mode=once)
    tail_spec = pl.BlockSpec((None, LANES, tail), lambda b, s: (b, s, 0))
    scratch = [pltpu.VMEM((t, LANES), F32)]
    scratch += [pltpu.VMEM((pad + t, LANES), F32) for _ in range(2)]
    scratch += [pltpu.VMEM((t, LANES), F32) for _ in range(2 * len(dils))]
    return pl.pallas_call(
        functools.partial(_dilated_kernel, dils=dils, t=t, tail=tail),
        grid=(n_batch, n_slab),
        in_specs=[
            pl.BlockSpec((t, LANES), lambda b, s: (b, q_slab + s)),
            pl.BlockSpec((t, LANES), lambda b, s: (b, k_slab + s)),
            pl.BlockSpec((t, LANES), lambda b, s: (b, v_slab + s)),
            table, table, table,
        ],
        out_specs=[pl.BlockSpec((t, LANES), lambda b, s: (b, s)), tail_spec, tail_spec],
        out_shape=[
            jax.ShapeDtypeStruct((n_batch * t, n_slab * LANES), BF16),
            jax.ShapeDtypeStruct((n_batch, n_slab * LANES, tail), F32),
            jax.ShapeDtypeStruct((n_batch, n_slab * LANES, tail), F32),
        ],
        scratch_shapes=scratch,
        compiler_params=_cparams(2, 48),
        name="attn_dilated",
    )(u, u, u, *tables)


GQA_SLABS = 4


def _gqa_kernel(q_ref, k_ref, v_ref, cos_ref, slo_ref, shi_ref, sink_ref, o_ref, kt_ref, vt_ref,
                kp, vp, *, t, tail):
    pad = QBLK
    kv_head = pl.program_id(1)
    lane = lax.broadcasted_iota(jnp.int32, (1, LANES), 1)
    low = lane < HEAD_DIM
    keep = lax.shift_right_logical(lane, jnp.int32(6)) == kv_head

    kp[0:pad, :] = jnp.zeros((pad, LANES), BF16)
    vp[0:pad, :] = jnp.zeros((pad, LANES), BF16)

    def prep(c, carry):
        r0 = pl.multiple_of(c * PREP_ROWS, PREP_ROWS)
        rows = pl.ds(r0, PREP_ROWS)
        kr = _rope(k_ref[rows, :], cos_ref[rows, :], slo_ref[rows, :], shi_ref[rows, :])
        v = v_ref[rows, :]
        prow = pl.ds(pad + r0, PREP_ROWS)
        kp[prow, :] = jnp.where(keep, kr, pltpu.roll(kr, HEAD_DIM, 1)).astype(BF16)
        vp[prow, :] = jnp.where(keep, v, pltpu.roll(v, HEAD_DIM, 1)).astype(BF16)
        return carry

    lax.fori_loop(0, t // PREP_ROWS, prep, 0)
    trows = slice(t - tail, t)
    _write_tails(kt_ref, vt_ref,
                 _rope(k_ref[trows, :], cos_ref[trows, :], slo_ref[trows, :], shi_ref[trows, :]),
                 v_ref[trows, :])

    band, current = _band_mask()

    def block(b, carry):
        q0 = pl.multiple_of(b * QBLK, QBLK)
        rows = pl.ds(q0, QBLK)
        cos, slo, shi = cos_ref[rows, :], slo_ref[rows, :], shi_ref[rows, :]
        pieces = []
        for s in range(GQA_SLABS):
            q = _rope(q_ref[rows, s * LANES:(s + 1) * LANES], cos, slo, shi) * ATTN_SCALE
            pieces += [jnp.where(low, q, 0.0).astype(BF16), jnp.where(low, 0.0, q).astype(BF16)]
        ksl = pl.ds(q0, 2 * QBLK)
        maxima, num, den = _block_softmax(pieces, kp[ksl, :], vp[ksl, :],
                                          band & (current | (b > 0)), fuse_den=False)
        for s in range(GQA_SLABS):
            outs = []
            for hh in range(2):
                i = 2 * s + hh
                piece = slice(i * QBLK, (i + 1) * QBLK)
                sink = sink_ref[:, s * LANES + hh * HEAD_DIM:s * LANES + hh * HEAD_DIM + 1]
                outs.append(num[piece, :] / (den[piece, :] + jnp.exp(sink - maxima[i])))
            o_ref[rows, s * LANES:(s + 1) * LANES] = jnp.where(low, outs[0], outs[1]).astype(BF16)
        return carry

    lax.fori_loop(0, t // QBLK, block, 0, unroll=BLOCK_UNROLL_GQA)


def _attention_gqa(u, tables, sinks, *, n_batch, t, q_slab, k_slab, v_slab, tail):
    width = GQA_SLABS * LANES
    once = pl.Buffered(1)
    table = pl.BlockSpec((t, LANES), lambda b, g: (0, 0), pipeline_mode=once)
    tail_spec = pl.BlockSpec((None, LANES, tail), lambda b, g: (b, 0, 0))
    q_blk = q_slab // GQA_SLABS
    return pl.pallas_call(
        functools.partial(_gqa_kernel, t=t, tail=tail),
        grid=(n_batch, N_KV_B),
        in_specs=[
            pl.BlockSpec((t, width), lambda b, g: (b, q_blk + g)),
            pl.BlockSpec((t, LANES), lambda b, g: (b, k_slab)),
            pl.BlockSpec((t, LANES), lambda b, g: (b, v_slab)),
            table, table, table,
            pl.BlockSpec((None, 1, width), lambda b, g: (g, 0, 0)),
        ],
        out_specs=[pl.BlockSpec((t, width), lambda b, g: (b, g)), tail_spec, tail_spec],
        out_shape=[
            jax.ShapeDtypeStruct((n_batch * t, N_KV_B * width), BF16),
            jax.ShapeDtypeStruct((n_batch, LANES, tail), F32),
            jax.ShapeDtypeStruct((n_batch, LANES, tail), F32),
        ],
        scratch_shapes=[pltpu.VMEM((QBLK + t, LANES), BF16), pltpu.VMEM((QBLK + t, LANES), BF16)],
        compiler_params=_cparams(2, 48),
        name="attn_gqa",
    )(u, u, u, *tables, sinks)


CONV_TB = 256
CONV_HALO = 32
CONV_RC = 32


def _layer_norm_swish(y, g, b):
    mu = jnp.mean(y, axis=-1, keepdims=True)
    yc = y - mu
    var = jnp.mean(yc * yc, axis=-1, keepdims=True)
    z = yc * lax.rsqrt(var + LN_EPS) * g + b
    return z * _sigmoid(z)


def _conv_kernel(v0, v1, g0, g1, cw_ref, cb_ref, lg_ref, lb_ref, o_ref, st_ref, ext):
    tt = pl.program_id(1)
    half = D_CONV // 2

    @pl.when(tt == 0)
    def _():
        ext[0:CONV_HALO, :] = jnp.zeros((CONV_HALO, D_CONV), F32)

    @pl.when(tt > 0)
    def _():
        ext[0:CONV_HALO, :] = ext[CONV_TB:CONV_TB + CONV_HALO, :]

    ext[CONV_HALO:CONV_HALO + CONV_TB, 0:half] = v0[...] * _sigmoid(g0[...])
    ext[CONV_HALO:CONV_HALO + CONV_TB, half:D_CONV] = v1[...] * _sigmoid(g1[...])
    st_ref[...] = ext[CONV_TB:CONV_TB + CONV_HALO, :]

    lead = CONV_HALO - (CONV_WIDTH - 1)
    for c in range(CONV_TB // CONV_RC):
        r0 = c * CONV_RC
        y = jnp.zeros((CONV_RC, D_CONV), F32) + cb_ref[...]
        for w in range(CONV_WIDTH):
            y = y + ext[r0 + lead + w:r0 + lead + w + CONV_RC, :] * cw_ref[w:w + 1, :]
        o_ref[r0:r0 + CONV_RC, :] = _layer_norm_swish(y, lg_ref[...], lb_ref[...]).astype(BF16)


def _conv_module(u, cw, cb, lg, lb, *, n_batch, t):
    nt = t // CONV_TB
    half = D_CONV // 2
    v_blk, g_blk = COL_VAL // half, COL_GATE // half
    row = lambda b, i: b * nt + i
    vec = pl.BlockSpec((1, D_CONV), lambda b, i: (0, 0))
    return pl.pallas_call(
        _conv_kernel,
        grid=(n_batch, nt),
        in_specs=[
            pl.BlockSpec((CONV_TB, half), lambda b, i: (row(b, i), v_blk)),
            pl.BlockSpec((CONV_TB, half), lambda b, i: (row(b, i), v_blk + 1)),
            pl.BlockSpec((CONV_TB, half), lambda b, i: (row(b, i), g_blk)),
            pl.BlockSpec((CONV_TB, half), lambda b, i: (row(b, i), g_blk + 1)),
            pl.BlockSpec((CONV_HALO, D_CONV), lambda b, i: (0, 0)),
            vec, vec, vec,
        ],
        out_specs=[
            pl.BlockSpec((CONV_TB, D_CONV), lambda b, i: (row(b, i), 0)),
            pl.BlockSpec((CONV_HALO, D_CONV), lambda b, i: (b, 0)),
        ],
        out_shape=[
            jax.ShapeDtypeStruct((n_batch * t, D_CONV), BF16),
            jax.ShapeDtypeStruct((n_batch * CONV_HALO, D_CONV), F32),
        ],
        scratch_shapes=[pltpu.VMEM((CONV_HALO + CONV_TB, D_CONV), F32)],
        compiler_params=_cparams(2, 32),
        name="conv_module",
    )(u, u, u, u, cw, cb, lg, lb)


ROPE_SLABS = tuple(range(SLAB_QA, SLAB_VA)) + tuple(range(SLAB_QB, SLAB_VB))
Q_SLABS = tuple(range(SLAB_QA, SLAB_KA)) + tuple(range(SLAB_QB, SLAB_KB))
VAL_SLABS = tuple(range(COL_VAL // LANES, COL_GATE // LANES))


def _sample_post_kernel(u_ref, cos_ref, slo_ref, shi_ref, o_ref):
    for s in range(N_IN // LANES):
        x = u_ref[:, s * LANES:(s + 1) * LANES]
        if s in ROPE_SLABS:
            x = _rope(x, cos_ref[...], slo_ref[...], shi_ref[...])
            if s in Q_SLABS:
                x = x * ATTN_SCALE
        elif s in VAL_SLABS:
            gs = s + D_CONV // LANES
            x = x * _sigmoid(u_ref[:, gs * LANES:(gs + 1) * LANES])
        o_ref[:, s * LANES:(s + 1) * LANES] = x


def _sample_post(u, tables):
    m, n = u.shape
    full = pl.BlockSpec((m, n), lambda i: (0, 0))
    vec = pl.BlockSpec((1, LANES), lambda i: (0, 0))
    return pl.pallas_call(
        _sample_post_kernel,
        grid=(1,),
        in_specs=[full, vec, vec, vec],
        out_specs=full,
        out_shape=jax.ShapeDtypeStruct((m, n), F32),
        compiler_params=_cparams(1, 16),
        name="sample_post",
    )(u, *tables)


def _pattern_multiplicity(la):
    dist = la - jnp.arange(la, dtype=jnp.int32)
    mult = jnp.zeros((la,), F32)
    for d in DILATIONS_A:
        mult = mult + ((dist % d == 0) & (dist <= HOPS * d)).astype(F32)
    return mult.reshape(1, la)


def _sattn_shift_kernel(kt_ref, vt_ref, q_ref, knr_ref, vnr_ref, knc_ref, vnc_ref, mult_ref,
                        *rest):
    o_ref, ko_ref, vo_ref = rest[-3:]
    nh, e, la = kt_ref.shape
    w = nh * e
    k = kt_ref[...].reshape(w, la)
    v = vt_ref[...].reshape(w, la)
    q = q_ref[...]
    mult = mult_ref[...]
    s = jnp.dot(q.astype(BF16), k.astype(BF16), preferred_element_type=F32)
    s = jnp.where(mult > 0.0, s, NEG_INF)
    s_new = jnp.sum(q * knr_ref[...], axis=1, keepdims=True)
    m = jnp.maximum(jnp.max(s, axis=1, keepdims=True), s_new)
    p = jnp.exp(s - m) * mult
    p_new = jnp.exp(s_new - m) * float(len(DILATIONS_A))
    den = jnp.sum(p, axis=1, keepdims=True) + p_new
    num = lax.dot_general(p.astype(BF16), v.astype(BF16), NT_DIMS, preferred_element_type=F32)
    o_ref[...] = (num + p_new * vnr_ref[...]) / den
    newest = lax.broadcasted_iota(jnp.int32, (1, la), 1) == la - 1
    ko_ref[...] = jnp.where(newest, knc_ref[...], pltpu.roll(k, la - 1, 1)).reshape(nh, e, la)
    vo_ref[...] = jnp.where(newest, vnc_ref[...], pltpu.roll(v, la - 1, 1)).reshape(nh, e, la)


def _sample_attn_shift(kt, vt, prev, layer, q_bd, kn, vn, mult):
    depth, nb, nh, e, la = kt.shape
    w = nh * e
    slab = pl.BlockSpec((None, None, nh, e, la), lambda b: (layer, b, 0, 0, 0))
    tokq = pl.BlockSpec((None, nh, w), lambda b: (b, 0, 0))
    row = pl.BlockSpec((None, 1, w), lambda b: (b, 0, 0))
    col = pl.BlockSpec((None, w, 1), lambda b: (b, 0, 0))
    in_specs = [slab, slab, tokq, row, row, col, col, pl.BlockSpec((1, la), lambda b: (0, 0))]
    args = [kt, vt, q_bd, kn.reshape(nb, 1, w), vn.reshape(nb, 1, w),
            kn.reshape(nb, w, 1), vn.reshape(nb, w, 1), mult]
    aliases = {}
    if prev is not None:
        aliases = {len(args): 1, len(args) + 1: 2}
        in_specs += [pl.BlockSpec(memory_space=pl.ANY)] * 2
        args += list(prev)
    return pl.pallas_call(
        _sattn_shift_kernel,
        grid=(nb,),
        in_specs=in_specs,
        out_specs=[tokq, slab, slab],
        out_shape=[jax.ShapeDtypeStruct((nb, nh, w), F32),
                   jax.ShapeDtypeStruct(kt.shape, F32), jax.ShapeDtypeStruct(vt.shape, F32)],
        input_output_aliases=aliases,
        compiler_params=_cparams(1, 56),
        name="sample_attn_shift",
    )(*args)


def _sattn_b_kernel(kc_ref, vc_ref, q_ref, kn_ref, vn_ref, sink_ref, o_ref):
    q = q_ref[...]
    s = lax.dot_general(q.astype(BF16), kc_ref[...].astype(BF16), NT_DIMS,
                        preferred_element_type=F32)
    s_new = jnp.sum(q * kn_ref[...], axis=1, keepdims=True)
    m = jnp.maximum(jnp.max(s, axis=1, keepdims=True), s_new)
    p = jnp.exp(s - m)
    p_new = jnp.exp(s_new - m)
    den = jnp.sum(p, axis=1, keepdims=True) + p_new + jnp.exp(sink_ref[...] - m)
    num = jnp.dot(p.astype(BF16), vc_ref[...].astype(BF16), preferred_element_type=F32)
    o_ref[...] = (num + p_new * vn_ref[...]) / den


def _sample_attn_b(kc, vc, layer, q_exp, kn, vn, sinks):
    _, nb, lb, w = kc.shape
    nh = q_exp.shape[1]
    cache = pl.BlockSpec((None, None, lb, w), lambda b: (layer, b, 0, 0))
    tokq = pl.BlockSpec((None, nh, w), lambda b: (b, 0, 0))
    tok1 = pl.BlockSpec((None, 1, w), lambda b: (b, 0, 0))
    return pl.pallas_call(
        _sattn_b_kernel,
        grid=(nb,),
        in_specs=[cache, cache, tokq, tok1, tok1, pl.BlockSpec((nh, 1), lambda b: (0, 0))],
        out_specs=tokq,
        out_shape=jax.ShapeDtypeStruct((nb, nh, w), F32),
        compiler_params=_cparams(1, 16),
        name="sample_attn_shared",
    )(kc, vc, q_exp, kn, vn, sinks)


def _sconv_kernel(st_ref, glu_ref, cw_ref, cb_ref, lg_ref, lb_ref, o_ref, ns_ref):
    hist = CONV_WIDTH - 1
    glu = glu_ref[...]
    y = glu * cw_ref[hist:hist + 1, :] + cb_ref[...]
    for w in range(hist):
        y = y + st_ref[:, w * D_CONV:(w + 1) * D_CONV] * cw_ref[w:w + 1, :]
    o_ref[...] = _layer_norm_swish(y, lg_ref[...], lb_ref[...]).astype(BF16)
    ns_ref[:, 0:(hist - 1) * D_CONV] = st_ref[:, D_CONV:hist * D_CONV]
    ns_ref[:, (hist - 1) * D_CONV:hist * D_CONV] = glu


def _sample_conv(state2d, layer, glu, cw, cb, lg, lb):
    _, nb, width = state2d.shape
    st = pl.BlockSpec((None, nb, width), lambda i: (layer, 0, 0))
    tok = pl.BlockSpec((nb, D_CONV), lambda i: (0, 0))
    vec = pl.BlockSpec((1, D_CONV), lambda i: (0, 0))
    return pl.pallas_call(
        _sconv_kernel,
        grid=(1,),
        in_specs=[st, tok, pl.BlockSpec((CONV_HALO, D_CONV), lambda i: (0, 0)), vec, vec, vec],
        out_specs=[tok, pl.BlockSpec((nb, width), lambda i: (0, 0))],
        out_shape=[jax.ShapeDtypeStruct((nb, D_CONV), BF16),
                   jax.ShapeDtypeStruct((nb, width), F32)],
        compiler_params=_cparams(1, 16),
        name="sample_conv",
    )(state2d, glu, cw, cb, lg, lb)


def _shift_kernel(km, kx, kn, vm, vx, vn, ko, vo):
    last = pl.program_id(2) == pl.num_programs(2) - 1
    for main, nxt, new, out in ((km, kx, kn, ko), (vm, vx, vn, vo)):
        rb = main.shape[1]
        out[:, 0:rb - 1] = main[:, 1:rb]
        out[:, rb - 1:rb] = jnp.where(last, new[...], nxt[...])


def _shift_caches(cache_k, cache_v, new_k, new_v, *, batch_chunk, row_block):
    depth, nb, rows, nh, e = cache_k.shape
    main = pl.BlockSpec((None, batch_chunk, row_block, nh, e), lambda l, b, r: (l, b, r, 0, 0))
    nxt = pl.BlockSpec((None, batch_chunk, 1, nh, e),
                       lambda l, b, r: (l, b, jnp.minimum((r + 1) * row_block, rows - 1), 0, 0))
    new = pl.BlockSpec((None, batch_chunk, 1, nh, e), lambda l, b, r: (l, b, 0, 0, 0))
    shape = jax.ShapeDtypeStruct(cache_k.shape, cache_k.dtype)
    return pl.pallas_call(
        _shift_kernel,
        grid=(depth, nb // batch_chunk, rows // row_block),
        in_specs=[main, nxt, new, main, nxt, new],
        out_specs=[main, main],
        out_shape=[shape, shape],
        compiler_params=_cparams(3, 48),
        name="cache_shift",
    )(cache_k, cache_k, new_k, cache_v, cache_v, new_v)


def _rope_tables(pos):
    n = pos.shape[0]
    inv_freq = 1.0 / (ROPE_THETA ** (jnp.arange(ROT_HALF, dtype=F32) / ROT_HALF))
    ang = pos.astype(F32)[:, None] * inv_freq[None, :]
    cos, sin = jnp.cos(ang), jnp.sin(ang)
    rest = HEAD_DIM - 2 * ROT_HALF
    c = jnp.concatenate([cos, cos, jnp.ones((n, rest), F32)], axis=1)
    lo = jnp.concatenate([-sin, jnp.zeros((n, HEAD_DIM - ROT_HALF), F32)], axis=1)
    hi = jnp.concatenate([jnp.zeros((n, ROT_HALF), F32), sin, jnp.zeros((n, rest), F32)], axis=1)
    return tuple(jnp.tile(x, (1, LANES // HEAD_DIM)) for x in (c, lo, hi))


def kernel(x_prompt, x_sample, c_prompt, c_sample, cache_a_k, cache_a_v, cache_b_k, cache_b_v,
           state_c_conv, w_ada, b_ada, g_pre, g_post, w_ffn_gu, w_ffn_down, w_in, w_out,
           attn_sinks, conv_w, conv_b, conv_ln_g, conv_ln_b):
    nbp, t, d = x_prompt.shape
    nbs = x_sample.shape[0]
    mp = nbp * t
    depth = w_ada.shape[0]
    la, lb = cache_a_k.shape[2], cache_b_k.shape[2]
    hist = CONV_WIDTH - 1
    tm_p = 1024

    xp = x_prompt.reshape(mp, d)
    xs = x_sample.reshape(nbs, d)

    s_row = 32
    c_all = jnp.zeros((s_row + nbs, d), F32).at[:nbp].set(c_prompt).at[s_row:].set(c_sample)
    mod_all = _ada_mod(c_all, w_ada, b_ada)

    def mod_p(l, s):
        return mod_all[l, :nbp, 3 * s * d:3 * (s + 1) * d].reshape(nbp, 3, 1, d)

    def mod_s(l, s):
        m = mod_all[l, s_row:, 3 * s * d:3 * (s + 1) * d].reshape(nbs, 3, d)
        return jnp.transpose(m, (1, 0, 2))[None]

    def nxt(l, s):
        return (l, s + 1) if s + 1 < N_SUB else (l + 1, 0)

    tab_p = _rope_tables(jnp.arange(t, dtype=jnp.int32))
    tab_s = _rope_tables(PAST_LEN + jnp.arange(1, dtype=jnp.int32))
    sink_groups = jnp.repeat(attn_sinks, HEAD_DIM, axis=1).reshape(depth, N_KV_B, 1, -1)
    sink_cols = attn_sinks.reshape(depth, N_HEADS_B, 1)
    cw_pad = jnp.pad(conv_w, ((0, 0), (0, CONV_HALO - CONV_WIDTH), (0, 0)))
    vec = lambda a, l: a[l].reshape(1, -1)

    cache_bk2 = cache_b_k.reshape(depth, nbs, lb, N_KV_B * HEAD_DIM)
    cache_bv2 = cache_b_v.reshape(depth, nbs, lb, N_KV_B * HEAD_DIM)
    state2d = state_c_conv.reshape(depth, nbs, hist * D_CONV)
    kv_of_head = (jnp.arange(N_HEADS_B) // (N_HEADS_B // N_KV_B))[:, None] == jnp.arange(N_KV_B)
    cache_akt = jnp.transpose(cache_a_k, (0, 1, 3, 4, 2))
    cache_avt = jnp.transpose(cache_a_v, (0, 1, 3, 4, 2))
    mult_a = _pattern_multiplicity(la)
    head_eye = jnp.eye(N_HEADS_A, dtype=bool)[None, :, :, None]

    hp = _prenorm(xp, mod_p(0, 0), vec(g_pre[0], 0), tm=tm_p, rows_per_batch=t)
    hs = _prenorm(xs, mod_s(0, 0), vec(g_pre[0], 0), tm=nbs, rows_per_batch=nbs)

    st_p = [[] for _ in range(5)]
    new_rows = [[], []]
    shift_a = None
    st_c_s = []

    f_pad = pl.cdiv(w_ffn_down.shape[2], TF) * TF
    wd_bf = _cast_pad_rows(w_ffn_down.reshape(depth * 2, -1, d), f_pad)
    wo_bf = _cast_pad_rows(w_out, w_out.shape[1])

    def resid(lhs, s, x, l, weight, mod_fn, tm, rpb):
        ln, sn = nxt(l, s)
        last = ln >= depth
        if s == 1:
            lhs, w, mat, tk = jnp.concatenate(lhs, axis=1), wo_bf, l, TF
        else:
            w, mat, tk = wd_bf, 2 * l + s // 2, f_pad // 4
        return _resid_proj(
            lhs, w, mat, x, mod_fn(l, s), vec(g_post[l], s),
            None if last else mod_fn(ln, sn), None if last else vec(g_pre[ln], sn),
            tm=tm, tk=tk, rows_per_batch=rpb, weight=weight)

    for l in range(depth):
        a = _ffn_up(hp, w_ffn_gu, (l, 0), tm=tm_p)
        xp, hp = resid(a, 0, xp, l, 0.5, mod_p, tm_p, t)
        u = _in_proj(hp, w_in, l, tm=tm_p)
        oa, kta, vta = _attention_dilated(
            u, tab_p, n_batch=nbp, t=t, dils=DILATIONS_A, q_slab=SLAB_QA, k_slab=SLAB_KA,
            v_slab=SLAB_VA, n_slab=N_HEADS_A // 2, tail=min(la, t))
        ob, ktb, vtb = _attention_gqa(
            u, tab_p, sink_groups[l], n_batch=nbp, t=t, q_slab=SLAB_QB, k_slab=SLAB_KB,
            v_slab=SLAB_VB, tail=min(lb, t))
        oc, cst = _conv_module(u, cw_pad[l], vec(conv_b, l), vec(conv_ln_g, l), vec(conv_ln_b, l),
                               n_batch=nbp, t=t)
        xp, hp = resid([oa, ob, oc], 1, xp, l, 1.0, mod_p, tm_p, t)
        a = _ffn_up(hp, w_ffn_gu, (l, 1), tm=tm_p)
        xp, hp = resid(a, 2, xp, l, 0.5, mod_p, tm_p, t)

        for dst, tail_t, heads in ((st_p[0], kta, N_HEADS_A), (st_p[1], vta, N_HEADS_A),
                                   (st_p[2], ktb, N_KV_B), (st_p[3], vtb, N_KV_B)):
            dst.append(jnp.transpose(tail_t.reshape(nbp, heads, HEAD_DIM, -1), (0, 3, 1, 2)))
        st_p[4].append(cst.reshape(nbp, CONV_HALO, D_CONV)[:, CONV_HALO - hist:])

        a = _ffn_up(hs, w_ffn_gu, (l, 0), tm=nbs)
        xs, hs = resid(a, 0, xs, l, 0.5, mod_s, nbs, nbs)
        ur = _sample_post(_in_proj(hs, w_in, l, tm=nbs), tab_s)
        seg = lambda lo, hi: ur[:, lo * LANES:hi * LANES]
        qa = seg(SLAB_QA, SLAB_KA).reshape(nbs, 1, N_HEADS_A, HEAD_DIM)
        q_bd = jnp.where(head_eye, qa, 0.0).reshape(nbs, N_HEADS_A, N_HEADS_A * HEAD_DIM)
        oa3, *shift_a = _sample_attn_shift(cache_akt, cache_avt, shift_a, l, q_bd,
                                           seg(SLAB_KA, SLAB_VA), seg(SLAB_VA, SLAB_QB), mult_a)
        oa = jnp.sum(jnp.where(head_eye, oa3.reshape(nbs, N_HEADS_A, N_HEADS_A, HEAD_DIM), 0.0),
                     axis=1)
        qb = seg(SLAB_QB, SLAB_KB).reshape(nbs, N_HEADS_B, 1, HEAD_DIM)
        q_exp = jnp.where(kv_of_head[None, :, :, None], qb, 0.0).reshape(nbs, N_HEADS_B, LANES)
        knb = seg(SLAB_KB, SLAB_VB)
        vnb = seg(SLAB_VB, SLAB_VB + 1)
        ob2 = _sample_attn_b(cache_bk2, cache_bv2, l, q_exp, knb.reshape(nbs, 1, LANES),
                             vnb.reshape(nbs, 1, LANES), sink_cols[l])
        ob = jnp.sum(jnp.where(kv_of_head[None, :, :, None],
                               ob2.reshape(nbs, N_HEADS_B, N_KV_B, HEAD_DIM), 0.0), axis=2)
        glu = ur[:, COL_VAL:COL_GATE]
        oc, ns = _sample_conv(state2d, l, glu, cw_pad[l], vec(conv_b, l), vec(conv_ln_g, l),
                              vec(conv_ln_b, l))
        mix = [oa.reshape(nbs, -1).astype(BF16), ob.reshape(nbs, -1).astype(BF16), oc]
        xs, hs = resid(mix, 1, xs, l, 1.0, mod_s, nbs, nbs)
        a = _ffn_up(hs, w_ffn_gu, (l, 1), tm=nbs)
        xs, hs = resid(a, 2, xs, l, 0.5, mod_s, nbs, nbs)

        new_rows[0].append(knb.reshape(nbs, 1, N_KV_B, HEAD_DIM))
        new_rows[1].append(vnb.reshape(nbs, 1, N_KV_B, HEAD_DIM))
        st_c_s.append(ns.reshape(nbs, hist, D_CONV))

    new_rows = [jnp.stack(r) for r in new_rows]
    shift_a = [jnp.transpose(c, (0, 1, 4, 2, 3)) for c in shift_a]
    shift_b = _shift_caches(cache_b_k, cache_b_v, new_rows[0], new_rows[1],
                            batch_chunk=8, row_block=lb)
    return (xp.reshape(nbp, t, d), xs.reshape(nbs, 1, d),
            *[jnp.stack(s) for s in st_p], *shift_a, *shift_b, jnp.stack(st_c_s))
```

```python
import functools

import jax
import jax.numpy as jnp
from jax import lax
from jax.experimental import pallas as pl
from jax.experimental.pallas import tpu as pltpu

F32 = jnp.float32
BF16 = jnp.bfloat16

D_MODEL = 2048
DEPTH = 4
HEAD_DIM = 64
N_HEADS_A = 8
N_HEADS_B = 16
N_KV_B = 2
D_CONV = 512
DILATIONS_A = (1, 4, 16)
HOPS = 128
CONV_WIDTH = 31
ROT_HALF = 8
ROPE_THETA = 500000.0
PAST_LEN = 16384
RMS_EPS = 1e-6
LN_EPS = 1e-5
NEG_INF = -1e30
ATTN_SCALE = HEAD_DIM ** -0.5
N_SUB = 3

LANES = 128
SUBLANES = 8
MIB = 1 << 20

SLAB_QA, SLAB_KA, SLAB_VA = 0, 4, 8
SLAB_QB, SLAB_KB, SLAB_VB = 12, 20, 21
COL_VAL, COL_GATE = 2816, 3328
N_IN = 3840


def _cparams(n_axes, vmem_mib):
    return pltpu.CompilerParams(
        dimension_semantics=("arbitrary",) * n_axes,
        vmem_limit_bytes=vmem_mib * MIB,
    )


def _sigmoid(x):
    return jax.nn.sigmoid(x)


def _pre_norm(x, g, shift, scale):
    ms = jnp.mean(x * x, axis=-1, keepdims=True)
    return (x * lax.rsqrt(ms + RMS_EPS)) * g * (1.0 + scale) + shift


def _gated_post(y, g, gate, weight):
    ms = jnp.mean(y * y, axis=-1, keepdims=True)
    return (weight * gate) * ((y * lax.rsqrt(ms + RMS_EPS)) * g)


def _rope(x, cos, sin_lo, sin_hi):
    return x * cos + pltpu.roll(x, LANES - ROT_HALF, 1) * sin_lo + pltpu.roll(x, ROT_HALF, 1) * sin_hi


def _ada_kernel(c_ref, w_ref, b_ref, o_ref):
    c = c_ref[...]
    a = (c * _sigmoid(c)).astype(BF16)
    o_ref[...] = jnp.dot(a, w_ref[...].astype(BF16), preferred_element_type=F32) + b_ref[...]


def _ada_mod(c_all, w_ada, b_ada):
    depth, d, n = w_ada.shape
    rows = c_all.shape[0]
    tn = 1024
    return pl.pallas_call(
        _ada_kernel,
        grid=(depth, n // tn),
        in_specs=[
            pl.BlockSpec((rows, d), lambda l, j: (0, 0)),
            pl.BlockSpec((None, d, tn), lambda l, j: (l, 0, j)),
            pl.BlockSpec((None, 1, tn), lambda l, j: (l, 0, j)),
        ],
        out_specs=pl.BlockSpec((None, rows, tn), lambda l, j: (l, 0, j)),
        out_shape=jax.ShapeDtypeStruct((depth, rows, n), F32),
        compiler_params=_cparams(2, 40),
        name="ada_mod",
    )(c_all, w_ada, b_ada.reshape(depth, 1, n))


def _prenorm_kernel(x_ref, mod_ref, g_ref, h_ref):
    h_ref[...] = _pre_norm(x_ref[...], g_ref[...], mod_ref[0], mod_ref[1]).astype(BF16)


def _mod_spec(mod, rows_per_batch, tm):
    _, _, r, d = mod.shape
    if r == 1:
        per = rows_per_batch // tm
        return pl.BlockSpec((None, 3, 1, d), lambda i, *_: (i // per, 0, 0, 0))
    return pl.BlockSpec((None, 3, r, d), lambda i, *_: (0, 0, 0, 0))


def _prenorm(x, mod, g, *, tm, rows_per_batch):
    m, d = x.shape
    return pl.pallas_call(
        _prenorm_kernel,
        grid=(m // tm,),
        in_specs=[
            pl.BlockSpec((tm, d), lambda i: (i, 0)),
            _mod_spec(mod, rows_per_batch, tm),
            pl.BlockSpec((1, d), lambda i: (0, 0)),
        ],
        out_specs=pl.BlockSpec((tm, d), lambda i: (i, 0)),
        out_shape=jax.ShapeDtypeStruct((m, d), BF16),
        compiler_params=_cparams(1, 32),
        name="prenorm",
    )(x, mod, g)


W_BLK = 128
W_STREAMS = 4
TF = W_BLK * W_STREAMS


def _ffn_up_kernel(h_ref, *refs, f_valid):
    g_refs = refs[0:W_STREAMS]
    u_refs = refs[W_STREAMS:2 * W_STREAMS]
    a_ref, wg_s, wu_s = refs[2 * W_STREAMS:]
    j = pl.program_id(0)

    @pl.when(pl.program_id(1) == 0)
    def _():
        for q in range(W_STREAMS):
            wg_s[:, q * W_BLK:(q + 1) * W_BLK] = g_refs[q][...].astype(BF16)
            wu_s[:, q * W_BLK:(q + 1) * W_BLK] = u_refs[q][...].astype(BF16)

    h = h_ref[...]
    g = jnp.dot(h, wg_s[...], preferred_element_type=F32)
    u = jnp.dot(h, wu_s[...], preferred_element_type=F32)
    a = (g * _sigmoid(g)) * u
    col = lax.broadcasted_iota(jnp.int32, a.shape, 1)
    a_ref[...] = jnp.where(col < f_valid - j * TF, a, 0.0).astype(BF16)


def _ffn_up(h, w_gu, lead, *, tm):
    m, d = h.shape
    f = w_gu.shape[-1] // 2
    squeezed = (None,) * len(lead)
    nb = f // W_BLK
    nj = pl.cdiv(f, TF)

    def wspec(base, q):
        return pl.BlockSpec(
            squeezed + (d, W_BLK),
            lambda j, i: lead + (0, base + jnp.minimum(W_STREAMS * j + q, nb - 1)))

    in_specs = [pl.BlockSpec((tm, d), lambda j, i: (i, 0))]
    in_specs += [wspec(0, q) for q in range(W_STREAMS)]
    in_specs += [wspec(nb, q) for q in range(W_STREAMS)]
    return pl.pallas_call(
        functools.partial(_ffn_up_kernel, f_valid=f),
        grid=(nj, m // tm),
        in_specs=in_specs,
        out_specs=pl.BlockSpec((tm, TF), lambda j, i: (i, j)),
        out_shape=jax.ShapeDtypeStruct((m, nj * TF), BF16),
        scratch_shapes=[pltpu.VMEM((d, TF), BF16), pltpu.VMEM((d, TF), BF16)],
        compiler_params=_cparams(2, 48),
        name="ffn_up",
    )(h, *([w_gu] * (2 * W_STREAMS)))


EPI_ROWS = 256


def _cast_kernel(w_ref, o_ref, *, k_valid):
    rows = w_ref.shape[1]
    row = lax.broadcasted_iota(jnp.int32, w_ref.shape, 1)
    o_ref[...] = jnp.where(row < k_valid - pl.program_id(0) * rows, w_ref[...], 0.0).astype(BF16)


def _cast_pad_rows(w, k_pad):
    n, k, d = w.shape
    last = k // W_BLK - 1
    return pl.pallas_call(
        functools.partial(_cast_kernel, k_valid=k),
        grid=(k_pad // W_BLK,),
        in_specs=[pl.BlockSpec((n, W_BLK, d), lambda j: (0, jnp.minimum(j, last), 0))],
        out_specs=pl.BlockSpec((n, W_BLK, d), lambda j: (0, j, 0)),
        out_shape=jax.ShapeDtypeStruct((n, k_pad, d), BF16),
        compiler_params=_cparams(1, 40),
        name="cast_weights",
    )(w)


def _resid_kernel(*refs, nk, nxc, weight, emit_h, tm, per_token):
    it = iter(refs)
    lhs_ref, w_ref, xin_ref, modc_ref, gpost_ref = [next(it) for _ in range(5)]
    modn_ref = gpre_ref = None
    if emit_h:
        modn_ref, gpre_ref = next(it), next(it)
    x_out = next(it)
    h_out = next(it) if emit_h else None
    acc = next(it)
    k = pl.program_id(1)

    if nxc > 1:
        xr = tm // nxc

        @pl.when(k < nxc)
        def _():
            x_out[pl.ds(pl.multiple_of(k * xr, xr), xr), :] = xin_ref[...]

    @pl.when(k == 0)
    def _():
        acc[...] = jnp.dot(lhs_ref[...], w_ref[...], preferred_element_type=F32)

    @pl.when(k > 0)
    def _():
        acc[...] += jnp.dot(lhs_ref[...], w_ref[...], preferred_element_type=F32)

    def epilogue(rows):
        y = acc[rows, :]
        x = xin_ref[rows, :] if nxc == 1 else x_out[rows, :]
        if per_token:
            gate = modc_ref[2, rows, :]
        else:
            gate = modc_ref[2]
        xn = x + _gated_post(y, gpost_ref[...], gate, weight)
        x_out[rows, :] = xn
        if emit_h:
            if per_token:
                shift, scale = modn_ref[0, rows, :], modn_ref[1, rows, :]
            else:
                shift, scale = modn_ref[0], modn_ref[1]
            h_out[rows, :] = _pre_norm(xn, gpre_ref[...], shift, scale).astype(BF16)

    @pl.when(k == nk - 1)
    def _():
        if tm <= EPI_ROWS:
            epilogue(slice(None))
        else:
            def body(c, carry):
                epilogue(pl.ds(pl.multiple_of(c * EPI_ROWS, EPI_ROWS), EPI_ROWS))
                return carry
            lax.fori_loop(0, tm // EPI_ROWS, body, 0)


def _resid_proj(lhs, w, mat, x, modc, gpost, modn, gpre, *, tm, tk, rows_per_batch, weight):
    m, d = x.shape
    emit_h = modn is not None
    nk = lhs.shape[1] // tk
    per_token = modc.shape[2] != 1
    nxc = min(nk, 4) if tm >= 512 else 1

    in_specs = [
        pl.BlockSpec((tm, tk), lambda i, k: (i, k)),
        pl.BlockSpec((None, tk, d), lambda i, k: (mat, k, 0)),
    ]
    if nxc > 1:
        in_specs.append(pl.BlockSpec(
            (tm // nxc, d), lambda i, k: (i * nxc + jnp.minimum(k, nxc - 1), 0)))
    else:
        in_specs.append(pl.BlockSpec((tm, d), lambda i, k: (i, 0)))
    in_specs.append(_mod_spec(modc, rows_per_batch, tm))
    in_specs.append(pl.BlockSpec((1, d), lambda i, k: (0, 0)))
    args = [lhs, w, x, modc, gpost]
    out_specs = [pl.BlockSpec((tm, d), lambda i, k: (i, 0))]
    out_shape = [jax.ShapeDtypeStruct((m, d), F32)]
    if emit_h:
        in_specs.append(_mod_spec(modn, rows_per_batch, tm))
        in_specs.append(pl.BlockSpec((1, d), lambda i, k: (0, 0)))
        args += [modn, gpre]
        out_specs.append(pl.BlockSpec((tm, d), lambda i, k: (i, 0)))
        out_shape.append(jax.ShapeDtypeStruct((m, d), BF16))
    outs = pl.pallas_call(
        functools.partial(_resid_kernel, nk=nk, nxc=nxc, weight=weight,
                          emit_h=emit_h, tm=tm, per_token=per_token),
        grid=(m // tm, nk),
        in_specs=in_specs,
        out_specs=out_specs,
        out_shape=out_shape,
        scratch_shapes=[pltpu.VMEM((tm, d), F32)],
        compiler_params=_cparams(2, 56),
        name="resid_proj",
    )(*args)
    return (outs[0], outs[1]) if emit_h else (outs[0], None)


def _inproj_kernel(h_ref, w_ref, o_ref, w_s):
    @pl.when(pl.program_id(1) == 0)
    def _():
        w_s[...] = w_ref[...].astype(BF16)

    o_ref[...] = jnp.dot(h_ref[...], w_s[...], preferred_element_type=F32)


def _in_proj(h, w_in, layer, *, tm):
    m, d = h.shape
    n = w_in.shape[-1]
    tn = 768
    return pl.pallas_call(
        _inproj_kernel,
        grid=(n // tn, m // tm),
        in_specs=[
            pl.BlockSpec((tm, d), lambda j, i: (i, 0)),
            pl.BlockSpec((None, d, tn), lambda j, i: (layer, 0, j)),
        ],
        out_specs=pl.BlockSpec((tm, tn), lambda j, i: (i, j)),
        out_shape=jax.ShapeDtypeStruct((m, n), F32),
        scratch_shapes=[pltpu.VMEM((d, tn), BF16)],
        compiler_params=_cparams(2, 48),
        name="in_proj",
    )(h, w_in)


QBLK = 128
PREP_ROWS = 512
BLOCK_UNROLL_DILATED = 8
BLOCK_UNROLL_GQA = 1
STAT_LANES = LANES // 4
NT_DIMS = (((1,), (1,)), ((), ()))


def _band_mask():
    row = lax.broadcasted_iota(jnp.int32, (QBLK, 2 * QBLK), 0)
    col = lax.broadcasted_iota(jnp.int32, (QBLK, 2 * QBLK), 1)
    return (col >= row) & (col <= row + HOPS), col >= QBLK


def _block_softmax(q_pieces, kb, vb, valid, fuse_den):
    s = lax.dot_general(jnp.concatenate(q_pieces, axis=0), kb, NT_DIMS,
                        preferred_element_type=F32)
    maxima, probs = [], []
    for i in range(len(q_pieces)):
        sp = jnp.where(valid, s[i * QBLK:(i + 1) * QBLK, :], NEG_INF)
        mb = jnp.max(sp, axis=1, keepdims=True)
        maxima.append(mb)
        probs.append(jnp.exp(sp - mb).astype(BF16))
    pb = jnp.concatenate(probs, axis=0)
    if fuse_den:
        both = jnp.dot(pb, jnp.concatenate([vb, jnp.ones_like(vb)], axis=1),
                       preferred_element_type=F32)
        return maxima, both[:, 0:LANES], both[:, LANES:2 * LANES]
    num = jnp.dot(pb, vb, preferred_element_type=F32)
    den = jnp.dot(pb, jnp.ones_like(vb), preferred_element_type=F32)
    return maxima, num, den


def _write_tails(kt_ref, vt_ref, k_tail, v_tail):
    for c in range(k_tail.shape[0] // QBLK):
        kt_ref[:, c * QBLK:(c + 1) * QBLK] = k_tail[c * QBLK:(c + 1) * QBLK, :].T
        vt_ref[:, c * QBLK:(c + 1) * QBLK] = v_tail[c * QBLK:(c + 1) * QBLK, :].T


def _dilated_kernel(q_ref, k_ref, v_ref, cos_ref, slo_ref, shi_ref, o_ref, kt_ref, vt_ref,
                    qs, kp, vp, *stat_scratch, dils, t, tail):
    pad = QBLK * max(dils)
    n_pat = len(dils)
    num_refs, stat_refs = stat_scratch[:n_pat], stat_scratch[n_pat:]
    lane = lax.broadcasted_iota(jnp.int32, (1, LANES), 1)
    low = lane < HEAD_DIM

    kp[0:pad, :] = jnp.zeros((pad, LANES), F32)
    vp[0:pad, :] = jnp.zeros((pad, LANES), F32)

    def prep(c, carry):
        r0 = pl.multiple_of(c * PREP_ROWS, PREP_ROWS)
        rows = pl.ds(r0, PREP_ROWS)
        cos, slo, shi = cos_ref[rows, :], slo_ref[rows, :], shi_ref[rows, :]
        qs[rows, :] = _rope(q_ref[rows, :], cos, slo, shi) * ATTN_SCALE
        kp[pl.ds(pad + r0, PREP_ROWS), :] = _rope(k_ref[rows, :], cos, slo, shi)
        vp[pl.ds(pad + r0, PREP_ROWS), :] = v_ref[rows, :]
        return carry

    lax.fori_loop(0, t // PREP_ROWS, prep, 0)
    _write_tails(kt_ref, vt_ref, kp[pad + t - tail:pad + t, :], v_ref[t - tail:t, :])

    band, current = _band_mask()
    for pi, d in enumerate(dils):
        shift = d.bit_length() - 1

        def block(b, carry, d=d, shift=shift, pi=pi):
            n = lax.shift_right_logical(b, jnp.int32(shift))
            q0 = n * (QBLK * d) + (b & (d - 1))
            k0 = q0 + pad - QBLK * d
            if d == 1:
                qsl = pl.ds(pl.multiple_of(q0, QBLK), QBLK)
                ksl = pl.ds(pl.multiple_of(k0, QBLK), 2 * QBLK)
            else:
                qsl = pl.ds(q0, QBLK, stride=d)
                ksl = pl.ds(k0, 2 * QBLK, stride=d)
            q = qs[qsl, :]
            pieces = [jnp.where(low, q, 0.0).astype(BF16),
                      jnp.where(low, 0.0, q).astype(BF16)]
            (m0, m1), num, den = _block_softmax(
                pieces, kp[ksl, :].astype(BF16), vp[ksl, :].astype(BF16),
                band & (current | (n > 0)), fuse_den=True)
            num_refs[pi][qsl, :] = jnp.where(low, num[0:QBLK, :], num[QBLK:2 * QBLK, :])
            stat_refs[pi][qsl, :] = jnp.where(
                lane < STAT_LANES, m0,
                jnp.where(low, den[0:QBLK, :],
                          jnp.where(lane < 3 * STAT_LANES, m1, den[QBLK:2 * QBLK, :])))
            return carry

        lax.fori_loop(0, t // QBLK, block, 0, unroll=BLOCK_UNROLL_DILATED)

    is_max = (lane & (HEAD_DIM - 1)) < STAT_LANES

    def finish(c, carry):
        rows = pl.ds(pl.multiple_of(c * PREP_ROWS, PREP_ROWS), PREP_ROWS)
        ms, ls = [], []
        for r in stat_refs:
            st = r[rows, :]
            ms.append(jnp.where(is_max, st, pltpu.roll(st, STAT_LANES, 1)))
            ls.append(jnp.where(is_max, pltpu.roll(st, LANES - STAT_LANES, 1), st))
        m = functools.reduce(jnp.maximum, ms)
        den = num = None
        for mi, li, r in zip(ms, ls, num_refs):
            w = jnp.exp(mi - m)
            den = w * li if den is None else den + w * li
            num = w * r[rows, :] if num is None else num + w * r[rows, :]
        o_ref[rows, :] = (num / den).astype(BF16)
        return carry

    lax.fori_loop(0, t // PREP_ROWS, finish, 0)


def _attention_dilated(u, tables, *, n_batch, t, dils, q_slab, k_slab, v_slab, n_slab, tail):
    pad = QBLK * max(dils)
    once = pl.Buffered(1)
    table = pl.BlockSpec((t, LANES), lambda b, s: (0, 0), pipeline_mode=once)
    tail_spec = pl.BlockSpec((None, LANES, tail), lambda b, s: (b, s, 0))
    scratch = [pltpu.VMEM((t, LANES), F32)]
    scratch += [pltpu.VMEM((pad + t, LANES), F32) for _ in range(2)]
    scratch += [pltpu.VMEM((t, LANES), F32) for _ in range(2 * len(dils))]
    return pl.pallas_call(
        functools.partial(_dilated_kernel, dils=dils, t=t, tail=tail),
        grid=(n_batch, n_slab),
        in_specs=[
            pl.BlockSpec((t, LANES), lambda b, s: (b, q_slab + s)),
            pl.BlockSpec((t, LANES), lambda b, s: (b, k_slab + s)),
            pl.BlockSpec((t, LANES), lambda b, s: (b, v_slab + s)),
            table, table, table,
        ],
        out_specs=[pl.BlockSpec((t, LANES), lambda b, s: (b, s)), tail_spec, tail_spec],
        out_shape=[
            jax.ShapeDtypeStruct((n_batch * t, n_slab * LANES), BF16),
            jax.ShapeDtypeStruct((n_batch, n_slab * LANES, tail), F32),
            jax.ShapeDtypeStruct((n_batch, n_slab * LANES, tail), F32),
        ],
        scratch_shapes=scratch,
        compiler_params=_cparams(2, 48),
        name="attn_dilated",
    )(u, u, u, *tables)


GQA_SLABS = 4


def _gqa_kernel(q_ref, k_ref, v_ref, cos_ref, slo_ref, shi_ref, sink_ref, o_ref, kt_ref, vt_ref,
                kp, vp, *, t, tail):
    pad = QBLK
    kv_head = pl.program_id(1)
    lane = lax.broadcasted_iota(jnp.int32, (1, LANES), 1)
    low = lane < HEAD_DIM
    keep = lax.shift_right_logical(lane, jnp.int32(6)) == kv_head

    kp[0:pad, :] = jnp.zeros((pad, LANES), BF16)
    vp[0:pad, :] = jnp.zeros((pad, LANES), BF16)

    def prep(c, carry):
        r0 = pl.multiple_of(c * PREP_ROWS, PREP_ROWS)
        rows = pl.ds(r0, PREP_ROWS)
        kr = _rope(k_ref[rows, :], cos_ref[rows, :], slo_ref[rows, :], shi_ref[rows, :])
        v = v_ref[rows, :]
        prow = pl.ds(pad + r0, PREP_ROWS)
        kp[prow, :] = jnp.where(keep, kr, pltpu.roll(kr, HEAD_DIM, 1)).astype(BF16)
        vp[prow, :] = jnp.where(keep, v, pltpu.roll(v, HEAD_DIM, 1)).astype(BF16)
        return carry

    lax.fori_loop(0, t // PREP_ROWS, prep, 0)
    trows = slice(t - tail, t)
    _write_tails(kt_ref, vt_ref,
                 _rope(k_ref[trows, :], cos_ref[trows, :], slo_ref[trows, :], shi_ref[trows, :]),
                 v_ref[trows, :])

    band, current = _band_mask()

    def block(b, carry):
        q0 = pl.multiple_of(b * QBLK, QBLK)
        rows = pl.ds(q0, QBLK)
        cos, slo, shi = cos_ref[rows, :], slo_ref[rows, :], shi_ref[rows, :]
        pieces = []
        for s in range(GQA_SLABS):
            q = _rope(q_ref[rows, s * LANES:(s + 1) * LANES], cos, slo, shi) * ATTN_SCALE
            pieces += [jnp.where(low, q, 0.0).astype(BF16), jnp.where(low, 0.0, q).astype(BF16)]
        ksl = pl.ds(q0, 2 * QBLK)
        maxima, num, den = _block_softmax(pieces, kp[ksl, :], vp[ksl, :],
                                          band & (current | (b > 0)), fuse_den=False)
        for s in range(GQA_SLABS):
            outs = []
            for hh in range(2):
                i = 2 * s + hh
                piece = slice(i * QBLK, (i + 1) * QBLK)
                sink = sink_ref[:, s * LANES + hh * HEAD_DIM:s * LANES + hh * HEAD_DIM + 1]
                outs.append(num[piece, :] / (den[piece, :] + jnp.exp(sink - maxima[i])))
            o_ref[rows, s * LANES:(s + 1) * LANES] = jnp.where(low, outs[0], outs[1]).astype(BF16)
        return carry

    lax.fori_loop(0, t // QBLK, block, 0, unroll=BLOCK_UNROLL_GQA)


def _attention_gqa(u, tables, sinks, *, n_batch, t, q_slab, k_slab, v_slab, tail):
    width = GQA_SLABS * LANES
    once = pl.Buffered(1)
    table = pl.BlockSpec((t, LANES), lambda b, g: (0, 0), pipeline_mode=once)
    tail_spec = pl.BlockSpec((None, LANES, tail), lambda b, g: (b, 0, 0))
    q_blk = q_slab // GQA_SLABS
    return pl.pallas_call(
        functools.partial(_gqa_kernel, t=t, tail=tail),
        grid=(n_batch, N_KV_B),
        in_specs=[
            pl.BlockSpec((t, width), lambda b, g: (b, q_blk + g)),
            pl.BlockSpec((t, LANES), lambda b, g: (b, k_slab)),
            pl.BlockSpec((t, LANES), lambda b, g: (b, v_slab)),
            table, table, table,
            pl.BlockSpec((None, 1, width), lambda b, g: (g, 0, 0)),
        ],
        out_specs=[pl.BlockSpec((t, width), lambda b, g: (b, g)), tail_spec, tail_spec],
        out_shape=[
            jax.ShapeDtypeStruct((n_batch * t, N_KV_B * width), BF16),
            jax.ShapeDtypeStruct((n_batch, LANES, tail), F32),
            jax.ShapeDtypeStruct((n_batch, LANES, tail), F32),
        ],
        scratch_shapes=[pltpu.VMEM((QBLK + t, LANES), BF16), pltpu.VMEM((QBLK + t, LANES), BF16)],
        compiler_params=_cparams(2, 48),
        name="attn_gqa",
    )(u, u, u, *tables, sinks)


CONV_TB = 256
CONV_HALO = 32
CONV_RC = 32


def _layer_norm_swish(y, g, b):
    mu = jnp.mean(y, axis=-1, keepdims=True)
    yc = y - mu
    var = jnp.mean(yc * yc, axis=-1, keepdims=True)
    z = yc * lax.rsqrt(var + LN_EPS) * g + b
    return z * _sigmoid(z)


def _conv_kernel(v0, v1, g0, g1, cw_ref, cb_ref, lg_ref, lb_ref, o_ref, st_ref, ext, shifted):
    tt = pl.program_id(1)
    half = D_CONV // 2

    @pl.when(tt == 0)
    def _():
        ext[0:CONV_HALO, :] = jnp.zeros((CONV_HALO, D_CONV), F32)

    @pl.when(tt > 0)
    def _():
        ext[0:CONV_HALO, :] = ext[CONV_TB:CONV_TB + CONV_HALO, :]

    ext[CONV_HALO:CONV_HALO + CONV_TB, 0:half] = v0[...] * _sigmoid(g0[...])
    ext[CONV_HALO:CONV_HALO + CONV_TB, half:D_CONV] = v1[...] * _sigmoid(g1[...])
    st_ref[...] = ext[CONV_TB:CONV_TB + CONV_HALO, :]

    span = CONV_TB + CONV_HALO - SUBLANES
    for s in range(1, SUBLANES):
        shifted[s - 1, 0:span, :] = ext[s:s + span, :]

    lead = CONV_HALO - (CONV_WIDTH - 1)
    for c in range(CONV_TB // CONV_RC):
        r0 = c * CONV_RC
        y = jnp.zeros((CONV_RC, D_CONV), F32) + cb_ref[...]
        for w in range(CONV_WIDTH):
            s = (lead + w) % SUBLANES
            base = r0 + lead + w - s
            win = ext[base:base + CONV_RC, :] if s == 0 else shifted[s - 1, base:base + CONV_RC, :]
            y = y + win * cw_ref[w:w + 1, :]
        o_ref[r0:r0 + CONV_RC, :] = _layer_norm_swish(y, lg_ref[...], lb_ref[...]).astype(BF16)


def _conv_module(u, cw, cb, lg, lb, *, n_batch, t):
    nt = t // CONV_TB
    half = D_CONV // 2
    v_blk, g_blk = COL_VAL // half, COL_GATE // half
    row = lambda b, i: b * nt + i
    vec = pl.BlockSpec((1, D_CONV), lambda b, i: (0, 0))
    return pl.pallas_call(
        _conv_kernel,
        grid=(n_batch, nt),
        in_specs=[
            pl.BlockSpec((CONV_TB, half), lambda b, i: (row(b, i), v_blk)),
            pl.BlockSpec((CONV_TB, half), lambda b, i: (row(b, i), v_blk + 1)),
            pl.BlockSpec((CONV_TB, half), lambda b, i: (row(b, i), g_blk)),
            pl.BlockSpec((CONV_TB, half), lambda b, i: (row(b, i), g_blk + 1)),
            pl.BlockSpec((CONV_HALO, D_CONV), lambda b, i: (0, 0)),
            vec, vec, vec,
        ],
        out_specs=[
            pl.BlockSpec((CONV_TB, D_CONV), lambda b, i: (row(b, i), 0)),
            pl.BlockSpec((CONV_HALO, D_CONV), lambda b, i: (b, 0)),
        ],
        out_shape=[
            jax.ShapeDtypeStruct((n_batch * t, D_CONV), BF16),
            jax.ShapeDtypeStruct((n_batch * CONV_HALO, D_CONV), F32),
        ],
        scratch_shapes=[pltpu.VMEM((CONV_HALO + CONV_TB, D_CONV), F32),
                        pltpu.VMEM((SUBLANES - 1, CONV_HALO + CONV_TB, D_CONV), F32)],
        compiler_params=_cparams(2, 32),
        name="conv_module",
    )(u, u, u, u, cw, cb, lg, lb)


ROPE_SLABS = tuple(range(SLAB_QA, SLAB_VA)) + tuple(range(SLAB_QB, SLAB_VB))
Q_SLABS = tuple(range(SLAB_QA, SLAB_KA)) + tuple(range(SLAB_QB, SLAB_KB))
VAL_SLABS = tuple(range(COL_VAL // LANES, COL_GATE // LANES))


def _sample_post_kernel(u_ref, cos_ref, slo_ref, shi_ref, o_ref):
    for s in range(N_IN // LANES):
        x = u_ref[:, s * LANES:(s + 1) * LANES]
        if s in ROPE_SLABS:
            x = _rope(x, cos_ref[...], slo_ref[...], shi_ref[...])
            if s in Q_SLABS:
                x = x * ATTN_SCALE
        elif s in VAL_SLABS:
            gs = s + D_CONV // LANES
            x = x * _sigmoid(u_ref[:, gs * LANES:(gs + 1) * LANES])
        o_ref[:, s * LANES:(s + 1) * LANES] = x


def _sample_post(u, tables):
    m, n = u.shape
    full = pl.BlockSpec((m, n), lambda i: (0, 0))
    vec = pl.BlockSpec((1, LANES), lambda i: (0, 0))
    return pl.pallas_call(
        _sample_post_kernel,
        grid=(1,),
        in_specs=[full, vec, vec, vec],
        out_specs=full,
        out_shape=jax.ShapeDtypeStruct((m, n), F32),
        compiler_params=_cparams(1, 16),
        name="sample_post",
    )(u, *tables)


def _pattern_multiplicity(la):
    dist = la - jnp.arange(la, dtype=jnp.int32)
    mult = jnp.zeros((la,), F32)
    for d in DILATIONS_A:
        mult = mult + ((dist % d == 0) & (dist <= HOPS * d)).astype(F32)
    return mult.reshape(1, la)


def _sattn_shift_kernel(kt_ref, vt_ref, q_ref, knr_ref, vnr_ref, knc_ref, vnc_ref, mult_ref,
                        *rest):
    o_ref, ko_ref, vo_ref = rest[-3:]
    nh, e, la = kt_ref.shape
    w = nh * e
    k = kt_ref[...].reshape(w, la)
    v = vt_ref[...].reshape(w, la)
    q = q_ref[...]
    mult = mult_ref[...]
    s = jnp.dot(q.astype(BF16), k.astype(BF16), preferred_element_type=F32)
    s = jnp.where(mult > 0.0, s, NEG_INF)
    s_new = jnp.sum(q * knr_ref[...], axis=1, keepdims=True)
    m = jnp.maximum(jnp.max(s, axis=1, keepdims=True), s_new)
    p = jnp.exp(s - m) * mult
    p_new = jnp.exp(s_new - m) * float(len(DILATIONS_A))
    den = jnp.sum(p, axis=1, keepdims=True) + p_new
    num = lax.dot_general(p.astype(BF16), v.astype(BF16), NT_DIMS, preferred_element_type=F32)
    o_ref[...] = (num + p_new * vnr_ref[...]) / den
    newest = lax.broadcasted_iota(jnp.int32, (1, la), 1) == la - 1
    ko_ref[...] = jnp.where(newest, knc_ref[...], pltpu.roll(k, la - 1, 1)).reshape(nh, e, la)
    vo_ref[...] = jnp.where(newest, vnc_ref[...], pltpu.roll(v, la - 1, 1)).reshape(nh, e, la)


def _sample_attn_shift(kt, vt, prev, layer, q_bd, kn, vn, mult):
    depth, nb, nh, e, la = kt.shape
    w = nh * e
    slab = pl.BlockSpec((None, None, nh, e, la), lambda b: (layer, b, 0, 0, 0))
    tokq = pl.BlockSpec((None, nh, w), lambda b: (b, 0, 0))
    row = pl.BlockSpec((None, 1, w), lambda b: (b, 0, 0))
    col = pl.BlockSpec((None, w, 1), lambda b: (b, 0, 0))
    in_specs = [slab, slab, tokq, row, row, col, col, pl.BlockSpec((1, la), lambda b: (0, 0))]
    args = [kt, vt, q_bd, kn.reshape(nb, 1, w), vn.reshape(nb, 1, w),
            kn.reshape(nb, w, 1), vn.reshape(nb, w, 1), mult]
    aliases = {}
    if prev is not None:
        aliases = {len(args): 1, len(args) + 1: 2}
        in_specs += [pl.BlockSpec(memory_space=pl.ANY)] * 2
        args += list(prev)
    return pl.pallas_call(
        _sattn_shift_kernel,
        grid=(nb,),
        in_specs=in_specs,
        out_specs=[tokq, slab, slab],
        out_shape=[jax.ShapeDtypeStruct((nb, nh, w), F32),
                   jax.ShapeDtypeStruct(kt.shape, F32), jax.ShapeDtypeStruct(vt.shape, F32)],
        input_output_aliases=aliases,
        compiler_params=_cparams(1, 56),
        name="sample_attn_shift",
    )(*args)


def _sattn_b_kernel(kc_ref, vc_ref, q_ref, kn_ref, vn_ref, sink_ref, o_ref):
    q = q_ref[...]
    s = lax.dot_general(q.astype(BF16), kc_ref[...].astype(BF16), NT_DIMS,
                        preferred_element_type=F32)
    s_new = jnp.sum(q * kn_ref[...], axis=1, keepdims=True)
    m = jnp.maximum(jnp.max(s, axis=1, keepdims=True), s_new)
    p = jnp.exp(s - m)
    p_new = jnp.exp(s_new - m)
    den = jnp.sum(p, axis=1, keepdims=True) + p_new + jnp.exp(sink_ref[...] - m)
    num = jnp.dot(p.astype(BF16), vc_ref[...].astype(BF16), preferred_element_type=F32)
    o_ref[...] = (num + p_new * vn_ref[...]) / den


def _sample_attn_b(kc, vc, layer, q_exp, kn, vn, sinks):
    _, nb, lb, w = kc.shape
    nh = q_exp.shape[1]
    cache = pl.BlockSpec((None, None, lb, w), lambda b: (layer, b, 0, 0))
    tokq = pl.BlockSpec((None, nh, w), lambda b: (b, 0, 0))
    tok1 = pl.BlockSpec((None, 1, w), lambda b: (b, 0, 0))
    return pl.pallas_call(
        _sattn_b_kernel,
        grid=(nb,),
        in_specs=[cache, cache, tokq, tok1, tok1, pl.BlockSpec((nh, 1), lambda b: (0, 0))],
        out_specs=tokq,
        out_shape=jax.ShapeDtypeStruct((nb, nh, w), F32),
        compiler_params=_cparams(1, 16),
        name="sample_attn_shared",
    )(kc, vc, q_exp, kn, vn, sinks)


def _sconv_kernel(st_ref, glu_ref, cw_ref, cb_ref, lg_ref, lb_ref, o_ref, ns_ref):
    hist = CONV_WIDTH - 1
    glu = glu_ref[...]
    y = glu * cw_ref[hist:hist + 1, :] + cb_ref[...]
    for w in range(hist):
        y = y + st_ref[:, w * D_CONV:(w + 1) * D_CONV] * cw_ref[w:w + 1, :]
    o_ref[...] = _layer_norm_swish(y, lg_ref[...], lb_ref[...]).astype(BF16)
    ns_ref[:, 0:(hist - 1) * D_CONV] = st_ref[:, D_CONV:hist * D_CONV]
    ns_ref[:, (hist - 1) * D_CONV:hist * D_CONV] = glu


def _sample_conv(state2d, layer, glu, cw, cb, lg, lb):
    _, nb, width = state2d.shape
    st = pl.BlockSpec((None, nb, width), lambda i: (layer, 0, 0))
    tok = pl.BlockSpec((nb, D_CONV), lambda i: (0, 0))
    vec = pl.BlockSpec((1, D_CONV), lambda i: (0, 0))
    return pl.pallas_call(
        _sconv_kernel,
        grid=(1,),
        in_specs=[st, tok, pl.BlockSpec((CONV_HALO, D_CONV), lambda i: (0, 0)), vec, vec, vec],
        out_specs=[tok, pl.BlockSpec((nb, width), lambda i: (0, 0))],
        out_shape=[jax.ShapeDtypeStruct((nb, D_CONV), BF16),
                   jax.ShapeDtypeStruct((nb, width), F32)],
        compiler_params=_cparams(1, 16),
        name="sample_conv",
    )(state2d, glu, cw, cb, lg, lb)


def _shift_kernel(km, kx, kn, vm, vx, vn, ko, vo):
    last = pl.program_id(2) == pl.num_programs(2) - 1
    for main, nxt, new, out in ((km, kx, kn, ko), (vm, vx, vn, vo)):
        rb = main.shape[1]
        out[:, 0:rb - 1] = main[:, 1:rb]
        out[:, rb - 1:rb] = jnp.where(last, new[...], nxt[...])


def _shift_caches(cache_k, cache_v, new_k, new_v, *, batch_chunk, row_block):
    depth, nb, rows, nh, e = cache_k.shape
    main = pl.BlockSpec((None, batch_chunk, row_block, nh, e), lambda l, b, r: (l, b, r, 0, 0))
    nxt = pl.BlockSpec((None, batch_chunk, 1, nh, e),
                       lambda l, b, r: (l, b, jnp.minimum((r + 1) * row_block, rows - 1), 0, 0))
    new = pl.BlockSpec((None, batch_chunk, 1, nh, e), lambda l, b, r: (l, b, 0, 0, 0))
    shape = jax.ShapeDtypeStruct(cache_k.shape, cache_k.dtype)
    return pl.pallas_call(
        _shift_kernel,
        grid=(depth, nb // batch_chunk, rows // row_block),
        in_specs=[main, nxt, new, main, nxt, new],
        out_specs=[main, main],
        out_shape=[shape, shape],
        compiler_params=_cparams(3, 48),
        name="cache_shift",
    )(cache_k, cache_k, new_k, cache_v, cache_v, new_v)


def _rope_tables(pos):
    n = pos.shape[0]
    inv_freq = 1.0 / (ROPE_THETA ** (jnp.arange(ROT_HALF, dtype=F32) / ROT_HALF))
    ang = pos.astype(F32)[:, None] * inv_freq[None, :]
    cos, sin = jnp.cos(ang), jnp.sin(ang)
    rest = HEAD_DIM - 2 * ROT_HALF
    c = jnp.concatenate([cos, cos, jnp.ones((n, rest), F32)], axis=1)
    lo = jnp.concatenate([-sin, jnp.zeros((n, HEAD_DIM - ROT_HALF), F32)], axis=1)
    hi = jnp.concatenate([jnp.zeros((n, ROT_HALF), F32), sin, jnp.zeros((n, rest), F32)], axis=1)
    return tuple(jnp.tile(x, (1, LANES // HEAD_DIM)) for x in (c, lo, hi))


def kernel(x_prompt, x_sample, c_prompt, c_sample, cache_a_k, cache_a_v, cache_b_k, cache_b_v,
           state_c_conv, w_ada, b_ada, g_pre, g_post, w_ffn_gu, w_ffn_down, w_in, w_out,
           attn_sinks, conv_w, conv_b, conv_ln_g, conv_ln_b):
    nbp, t, d = x_prompt.shape
    nbs = x_sample.shape[0]
    mp = nbp * t
    depth = w_ada.shape[0]
    la, lb = cache_a_k.shape[2], cache_b_k.shape[2]
    hist = CONV_WIDTH - 1
    tm_p = 1024

    xp = x_prompt.reshape(mp, d)
    xs = x_sample.reshape(nbs, d)

    s_row = 32
    c_all = jnp.zeros((s_row + nbs, d), F32).at[:nbp].set(c_prompt).at[s_row:].set(c_sample)
    mod_all = _ada_mod(c_all, w_ada, b_ada)

    def mod_p(l, s):
        return mod_all[l, :nbp, 3 * s * d:3 * (s + 1) * d].reshape(nbp, 3, 1, d)

    def mod_s(l, s):
        m = mod_all[l, s_row:, 3 * s * d:3 * (s + 1) * d].reshape(nbs, 3, d)
        return jnp.transpose(m, (1, 0, 2))[None]

    def nxt(l, s):
        return (l, s + 1) if s + 1 < N_SUB else (l + 1, 0)

    tab_p = _rope_tables(jnp.arange(t, dtype=jnp.int32))
    tab_s = _rope_tables(PAST_LEN + jnp.arange(1, dtype=jnp.int32))
    sink_groups = jnp.repeat(attn_sinks, HEAD_DIM, axis=1).reshape(depth, N_KV_B, 1, -1)
    sink_cols = attn_sinks.reshape(depth, N_HEADS_B, 1)
    cw_pad = jnp.pad(conv_w, ((0, 0), (0, CONV_HALO - CONV_WIDTH), (0, 0)))
    vec = lambda a, l: a[l].reshape(1, -1)

    cache_bk2 = cache_b_k.reshape(depth, nbs, lb, N_KV_B * HEAD_DIM)
    cache_bv2 = cache_b_v.reshape(depth, nbs, lb, N_KV_B * HEAD_DIM)
    state2d = state_c_conv.reshape(depth, nbs, hist * D_CONV)
    kv_of_head = (jnp.arange(N_HEADS_B) // (N_HEADS_B // N_KV_B))[:, None] == jnp.arange(N_KV_B)
    cache_akt = jnp.transpose(cache_a_k, (0, 1, 3, 4, 2))
    cache_avt = jnp.transpose(cache_a_v, (0, 1, 3, 4, 2))
    mult_a = _pattern_multiplicity(la)
    head_eye = jnp.eye(N_HEADS_A, dtype=bool)[None, :, :, None]

    hp = _prenorm(xp, mod_p(0, 0), vec(g_pre[0], 0), tm=tm_p, rows_per_batch=t)
    hs = _prenorm(xs, mod_s(0, 0), vec(g_pre[0], 0), tm=nbs, rows_per_batch=nbs)

    st_p = [[] for _ in range(5)]
    new_rows = [[], []]
    shift_a = None
    st_c_s = []

    f_pad = pl.cdiv(w_ffn_down.shape[2], TF) * TF
    wd_bf = _cast_pad_rows(w_ffn_down.reshape(depth * 2, -1, d), f_pad)
    wo_bf = _cast_pad_rows(w_out, w_out.shape[1])

    def resid(lhs, s, x, l, weight, mod_fn, tm, rpb):
        ln, sn = nxt(l, s)
        last = ln >= depth
        if s == 1:
            lhs, w, mat, tk = jnp.concatenate(lhs, axis=1), wo_bf, l, TF
        else:
            w, mat, tk = wd_bf, 2 * l + s // 2, f_pad // 4
        return _resid_proj(
            lhs, w, mat, x, mod_fn(l, s), vec(g_post[l], s),
            None if last else mod_fn(ln, sn), None if last else vec(g_pre[ln], sn),
            tm=tm, tk=tk, rows_per_batch=rpb, weight=weight)

    for l in range(depth):
        a = _ffn_up(hp, w_ffn_gu, (l, 0), tm=tm_p)
        xp, hp = resid(a, 0, xp, l, 0.5, mod_p, tm_p, t)
        u = _in_proj(hp, w_in, l, tm=tm_p)
        oa, kta, vta = _attention_dilated(
            u, tab_p, n_batch=nbp, t=t, dils=DILATIONS_A, q_slab=SLAB_QA, k_slab=SLAB_KA,
            v_slab=SLAB_VA, n_slab=N_HEADS_A // 2, tail=min(la, t))
        ob, ktb, vtb = _attention_gqa(
            u, tab_p, sink_groups[l], n_batch=nbp, t=t, q_slab=SLAB_QB, k_slab=SLAB_KB,
            v_slab=SLAB_VB, tail=min(lb, t))
        oc, cst = _conv_module(u, cw_pad[l], vec(conv_b, l), vec(conv_ln_g, l), vec(conv_ln_b, l),
                               n_batch=nbp, t=t)
        xp, hp = resid([oa, ob, oc], 1, xp, l, 1.0, mod_p, tm_p, t)
        a = _ffn_up(hp, w_ffn_gu, (l, 1), tm=tm_p)
        xp, hp = resid(a, 2, xp, l, 0.5, mod_p, tm_p, t)

        for dst, tail_t, heads in ((st_p[0], kta, N_HEADS_A), (st_p[1], vta, N_HEADS_A),
                                   (st_p[2], ktb, N_KV_B), (st_p[3], vtb, N_KV_B)):
            dst.append(jnp.transpose(tail_t.reshape(nbp, heads, HEAD_DIM, -1), (0, 3, 1, 2)))
        st_p[4].append(cst.reshape(nbp, CONV_HALO, D_CONV)[:, CONV_HALO - hist:])

        a = _ffn_up(hs, w_ffn_gu, (l, 0), tm=nbs)
        xs, hs = resid(a, 0, xs, l, 0.5, mod_s, nbs, nbs)
        ur = _sample_post(_in_proj(hs, w_in, l, tm=nbs), tab_s)
        seg = lambda lo, hi: ur[:, lo * LANES:hi * LANES]
        qa = seg(SLAB_QA, SLAB_KA).reshape(nbs, 1, N_HEADS_A, HEAD_DIM)
        q_bd = jnp.where(head_eye, qa, 0.0).reshape(nbs, N_HEADS_A, N_HEADS_A * HEAD_DIM)
        oa3, *shift_a = _sample_attn_shift(cache_akt, cache_avt, shift_a, l, q_bd,
                                           seg(SLAB_KA, SLAB_VA), seg(SLAB_VA, SLAB_QB), mult_a)
        oa = jnp.sum(jnp.where(head_eye, oa3.reshape(nbs, N_HEADS_A, N_HEADS_A, HEAD_DIM), 0.0),
                     axis=1)
        qb = seg(SLAB_QB, SLAB_KB).reshape(nbs, N_HEADS_B, 1, HEAD_DIM)
        q_exp = jnp.where(kv_of_head[None, :, :, None], qb, 0.0).reshape(nbs, N_HEADS_B, LANES)
        knb = seg(SLAB_KB, SLAB_VB)
        vnb = seg(SLAB_VB, SLAB_VB + 1)
        ob2 = _sample_attn_b(cache_bk2, cache_bv2, l, q_exp, knb.reshape(nbs, 1, LANES),
                             vnb.reshape(nbs, 1, LANES), sink_cols[l])
        ob = jnp.sum(jnp.where(kv_of_head[None, :, :, None],
                               ob2.reshape(nbs, N_HEADS_B, N_KV_B, HEAD_DIM), 0.0), axis=2)
        glu = ur[:, COL_VAL:COL_GATE]
        oc, ns = _sample_conv(state2d, l, glu, cw_pad[l], vec(conv_b, l), vec(conv_ln_g, l),
                              vec(conv_ln_b, l))
        mix = [oa.reshape(nbs, -1).astype(BF16), ob.reshape(nbs, -1).astype(BF16), oc]
        xs, hs = resid(mix, 1, xs, l, 1.0, mod_s, nbs, nbs)
        a = _ffn_up(hs, w_ffn_gu, (l, 1), tm=nbs)
        xs, hs = resid(a, 2, xs, l, 0.5, mod_s, nbs, nbs)

        new_rows[0].append(knb.reshape(nbs, 1, N_KV_B, HEAD_DIM))
        new_rows[1].append(vnb.reshape(nbs, 1, N_KV_B, HEAD_DIM))
        st_c_s.append(ns.reshape(nbs, hist, D_CONV))

    new_rows = [jnp.stack(r) for r in new_rows]
    shift_a = [jnp.transpose(c, (0, 1, 4, 2, 3)) for c in shift_a]
    shift_b = _shift_caches(cache_b_k, cache_b_v, new_rows[0], new_rows[1],
                            batch_chunk=8, row_block=lb)
    return (xp.reshape(nbp, t, d), xs.reshape(nbs, 1, d),
            *[jnp.stack(s) for s in st_p], *shift_a, *shift_b, jnp.stack(st_c_s))
```

```python
import functools

import jax
import jax.numpy as jnp
from jax import lax
from jax.experimental import pallas as pl
from jax.experimental.pallas import tpu as pltpu

F32 = jnp.float32
BF16 = jnp.bfloat16

D_MODEL = 2048
DEPTH = 4
HEAD_DIM = 64
N_HEADS_A = 8
N_HEADS_B = 16
N_KV_B = 2
D_CONV = 512
DILATIONS_A = (1, 4, 16)
HOPS = 128
CONV_WIDTH = 31
ROT_HALF = 8
ROPE_THETA = 500000.0
PAST_LEN = 16384
RMS_EPS = 1e-6
LN_EPS = 1e-5
NEG_INF = -1e30
ATTN_SCALE = HEAD_DIM ** -0.5
N_SUB = 3

LANES = 128
SUBLANES = 8
MIB = 1 << 20

SLAB_QA, SLAB_KA, SLAB_VA = 0, 4, 8
SLAB_QB, SLAB_KB, SLAB_VB = 12, 20, 21
COL_VAL, COL_GATE = 2816, 3328
N_IN = 3840


def _cparams(n_axes, vmem_mib):
    return pltpu.CompilerParams(
        dimension_semantics=("arbitrary",) * n_axes,
        vmem_limit_bytes=vmem_mib * MIB,
    )


def _sigmoid(x):
    return jax.nn.sigmoid(x)


def _pre_norm(x, g, shift, scale):
    ms = jnp.mean(x * x, axis=-1, keepdims=True)
    return (x * lax.rsqrt(ms + RMS_EPS)) * g * (1.0 + scale) + shift


def _gated_post(y, g, gate, weight):
    ms = jnp.mean(y * y, axis=-1, keepdims=True)
    return (weight * gate) * ((y * lax.rsqrt(ms + RMS_EPS)) * g)


def _rope(x, cos, sin_lo, sin_hi):
    return x * cos + pltpu.roll(x, LANES - ROT_HALF, 1) * sin_lo + pltpu.roll(x, ROT_HALF, 1) * sin_hi


def _ada_kernel(c_ref, w_ref, b_ref, o_ref):
    c = c_ref[...]
    a = (c * _sigmoid(c)).astype(BF16)
    o_ref[...] = jnp.dot(a, w_ref[...].astype(BF16), preferred_element_type=F32) + b_ref[...]


def _ada_mod(c_all, w_ada, b_ada):
    depth, d, n = w_ada.shape
    rows = c_all.shape[0]
    tn = 1024
    return pl.pallas_call(
        _ada_kernel,
        grid=(depth, n // tn),
        in_specs=[
            pl.BlockSpec((rows, d), lambda l, j: (0, 0)),
            pl.BlockSpec((None, d, tn), lambda l, j: (l, 0, j)),
            pl.BlockSpec((None, 1, tn), lambda l, j: (l, 0, j)),
        ],
        out_specs=pl.BlockSpec((None, rows, tn), lambda l, j: (l, 0, j)),
        out_shape=jax.ShapeDtypeStruct((depth, rows, n), F32),
        compiler_params=_cparams(2, 40),
        name="ada_mod",
    )(c_all, w_ada, b_ada.reshape(depth, 1, n))


def _prenorm_kernel(x_ref, mod_ref, g_ref, h_ref):
    h_ref[...] = _pre_norm(x_ref[...], g_ref[...], mod_ref[0], mod_ref[1]).astype(BF16)


def _mod_spec(mod, rows_per_batch, tm):
    _, _, r, d = mod.shape
    if r == 1:
        per = rows_per_batch // tm
        return pl.BlockSpec((None, 3, 1, d), lambda i, *_: (i // per, 0, 0, 0))
    return pl.BlockSpec((None, 3, r, d), lambda i, *_: (0, 0, 0, 0))


def _prenorm(x, mod, g, *, tm, rows_per_batch):
    m, d = x.shape
    return pl.pallas_call(
        _prenorm_kernel,
        grid=(m // tm,),
        in_specs=[
            pl.BlockSpec((tm, d), lambda i: (i, 0)),
            _mod_spec(mod, rows_per_batch, tm),
            pl.BlockSpec((1, d), lambda i: (0, 0)),
        ],
        out_specs=pl.BlockSpec((tm, d), lambda i: (i, 0)),
        out_shape=jax.ShapeDtypeStruct((m, d), BF16),
        compiler_params=_cparams(1, 32),
        name="prenorm",
    )(x, mod, g)


W_BLK = 128
W_STREAMS = 4
TF = W_BLK * W_STREAMS


def _ffn_up_kernel(h_ref, hs_ref, *refs, f_valid):
    g_refs = refs[0:W_STREAMS]
    u_refs = refs[W_STREAMS:2 * W_STREAMS]
    a_ref, as_ref, wg_s, wu_s = refs[2 * W_STREAMS:]
    j = pl.program_id(0)

    def gated(h):
        g = jnp.dot(h, wg_s[...], preferred_element_type=F32)
        u = jnp.dot(h, wu_s[...], preferred_element_type=F32)
        a = (g * _sigmoid(g)) * u
        col = lax.broadcasted_iota(jnp.int32, a.shape, 1)
        return jnp.where(col < f_valid - j * TF, a, 0.0).astype(BF16)

    @pl.when(pl.program_id(1) == 0)
    def _():
        for q in range(W_STREAMS):
            wg_s[:, q * W_BLK:(q + 1) * W_BLK] = g_refs[q][...].astype(BF16)
            wu_s[:, q * W_BLK:(q + 1) * W_BLK] = u_refs[q][...].astype(BF16)
        as_ref[...] = gated(hs_ref[...])

    a_ref[...] = gated(h_ref[...])


def _ffn_up(h, hs, w_gu, lead, *, tm):
    m, d = h.shape
    ms = hs.shape[0]
    f = w_gu.shape[-1] // 2
    squeezed = (None,) * len(lead)
    nb = f // W_BLK
    nj = pl.cdiv(f, TF)

    def wspec(base, q):
        return pl.BlockSpec(
            squeezed + (d, W_BLK),
            lambda j, i: lead + (0, base + jnp.minimum(W_STREAMS * j + q, nb - 1)))

    in_specs = [pl.BlockSpec((tm, d), lambda j, i: (i, 0)),
                pl.BlockSpec((ms, d), lambda j, i: (0, 0))]
    in_specs += [wspec(0, q) for q in range(W_STREAMS)]
    in_specs += [wspec(nb, q) for q in range(W_STREAMS)]
    return pl.pallas_call(
        functools.partial(_ffn_up_kernel, f_valid=f),
        grid=(nj, m // tm),
        in_specs=in_specs,
        out_specs=[pl.BlockSpec((tm, TF), lambda j, i: (i, j)),
                   pl.BlockSpec((ms, TF), lambda j, i: (0, j))],
        out_shape=[jax.ShapeDtypeStruct((m, nj * TF), BF16),
                   jax.ShapeDtypeStruct((ms, nj * TF), BF16)],
        scratch_shapes=[pltpu.VMEM((d, TF), BF16), pltpu.VMEM((d, TF), BF16)],
        compiler_params=_cparams(2, 48),
        name="ffn_up",
    )(h, hs, *([w_gu] * (2 * W_STREAMS)))


EPI_ROWS = 256


def _cast_kernel(w_ref, o_ref, *, k_valid):
    rows = w_ref.shape[1]
    row = lax.broadcasted_iota(jnp.int32, w_ref.shape, 1)
    o_ref[...] = jnp.where(row < k_valid - pl.program_id(0) * rows, w_ref[...], 0.0).astype(BF16)


def _cast_pad_rows(w, k_pad):
    n, k, d = w.shape
    last = k // W_BLK - 1
    return pl.pallas_call(
        functools.partial(_cast_kernel, k_valid=k),
        grid=(k_pad // W_BLK,),
        in_specs=[pl.BlockSpec((n, W_BLK, d), lambda j: (0, jnp.minimum(j, last), 0))],
        out_specs=pl.BlockSpec((n, W_BLK, d), lambda j: (0, j, 0)),
        out_shape=jax.ShapeDtypeStruct((n, k_pad, d), BF16),
        compiler_params=_cparams(1, 40),
        name="cast_weights",
    )(w)


def _resid_kernel(*refs, nk, nxc, weight, emit_h, tm, per_token):
    it = iter(refs)
    lhs_ref, w_ref, xin_ref, modc_ref, gpost_ref = [next(it) for _ in range(5)]
    modn_ref = gpre_ref = None
    if emit_h:
        modn_ref, gpre_ref = next(it), next(it)
    x_out = next(it)
    h_out = next(it) if emit_h else None
    acc = next(it)
    k = pl.program_id(1)

    if nxc > 1:
        xr = tm // nxc

        @pl.when(k < nxc)
        def _():
            x_out[pl.ds(pl.multiple_of(k * xr, xr), xr), :] = xin_ref[...]

    @pl.when(k == 0)
    def _():
        acc[...] = jnp.dot(lhs_ref[...], w_ref[...], preferred_element_type=F32)

    @pl.when(k > 0)
    def _():
        acc[...] += jnp.dot(lhs_ref[...], w_ref[...], preferred_element_type=F32)

    def epilogue(rows):
        y = acc[rows, :]
        x = xin_ref[rows, :] if nxc == 1 else x_out[rows, :]
        if per_token:
            gate = modc_ref[2, rows, :]
        else:
            gate = modc_ref[2]
        xn = x + _gated_post(y, gpost_ref[...], gate, weight)
        x_out[rows, :] = xn
        if emit_h:
            if per_token:
                shift, scale = modn_ref[0, rows, :], modn_ref[1, rows, :]
            else:
                shift, scale = modn_ref[0], modn_ref[1]
            h_out[rows, :] = _pre_norm(xn, gpre_ref[...], shift, scale).astype(BF16)

    @pl.when(k == nk - 1)
    def _():
        if tm <= EPI_ROWS:
            epilogue(slice(None))
        else:
            def body(c, carry):
                epilogue(pl.ds(pl.multiple_of(c * EPI_ROWS, EPI_ROWS), EPI_ROWS))
                return carry
            lax.fori_loop(0, tm // EPI_ROWS, body, 0)


def _resid_proj(lhs, w, mat, x, modc, gpost, modn, gpre, *, tm, tk, rows_per_batch, weight):
    m, d = x.shape
    emit_h = modn is not None
    nk = lhs.shape[1] // tk
    per_token = modc.shape[2] != 1
    nxc = min(nk, 4) if tm >= 512 else 1

    in_specs = [
        pl.BlockSpec((tm, tk), lambda i, k: (i, k)),
        pl.BlockSpec((None, tk, d), lambda i, k: (mat, k, 0)),
    ]
    if nxc > 1:
        in_specs.append(pl.BlockSpec(
            (tm // nxc, d), lambda i, k: (i * nxc + jnp.minimum(k, nxc - 1), 0)))
    else:
        in_specs.append(pl.BlockSpec((tm, d), lambda i, k: (i, 0)))
    in_specs.append(_mod_spec(modc, rows_per_batch, tm))
    in_specs.append(pl.BlockSpec((1, d), lambda i, k: (0, 0)))
    args = [lhs, w, x, modc, gpost]
    out_specs = [pl.BlockSpec((tm, d), lambda i, k: (i, 0))]
    out_shape = [jax.ShapeDtypeStruct((m, d), F32)]
    if emit_h:
        in_specs.append(_mod_spec(modn, rows_per_batch, tm))
        in_specs.append(pl.BlockSpec((1, d), lambda i, k: (0, 0)))
        args += [modn, gpre]
        out_specs.append(pl.BlockSpec((tm, d), lambda i, k: (i, 0)))
        out_shape.append(jax.ShapeDtypeStruct((m, d), BF16))
    outs = pl.pallas_call(
        functools.partial(_resid_kernel, nk=nk, nxc=nxc, weight=weight,
                          emit_h=emit_h, tm=tm, per_token=per_token),
        grid=(m // tm, nk),
        in_specs=in_specs,
        out_specs=out_specs,
        out_shape=out_shape,
        scratch_shapes=[pltpu.VMEM((tm, d), F32)],
        compiler_params=_cparams(2, 56),
        name="resid_proj",
    )(*args)
    return (outs[0], outs[1]) if emit_h else (outs[0], None)


def _inproj_kernel(h_ref, w_ref, o_ref, w_s):
    @pl.when(pl.program_id(1) == 0)
    def _():
        w_s[...] = w_ref[...].astype(BF16)

    o_ref[...] = jnp.dot(h_ref[...], w_s[...], preferred_element_type=F32)


def _in_proj(h, w_in, layer, *, tm):
    m, d = h.shape
    n = w_in.shape[-1]
    tn = 768
    return pl.pallas_call(
        _inproj_kernel,
        grid=(n // tn, m // tm),
        in_specs=[
            pl.BlockSpec((tm, d), lambda j, i: (i, 0)),
            pl.BlockSpec((None, d, tn), lambda j, i: (layer, 0, j)),
        ],
        out_specs=pl.BlockSpec((tm, tn), lambda j, i: (i, j)),
        out_shape=jax.ShapeDtypeStruct((m, n), F32),
        scratch_shapes=[pltpu.VMEM((d, tn), BF16)],
        compiler_params=_cparams(2, 48),
        name="in_proj",
    )(h, w_in)


QBLK = 128
PREP_ROWS = 512
BLOCK_UNROLL_DILATED = 8
BLOCK_UNROLL_GQA = 1
STAT_LANES = LANES // 4
NT_DIMS = (((1,), (1,)), ((), ()))


def _band_mask():
    row = lax.broadcasted_iota(jnp.int32, (QBLK, 2 * QBLK), 0)
    col = lax.broadcasted_iota(jnp.int32, (QBLK, 2 * QBLK), 1)
    return (col >= row) & (col <= row + HOPS), col >= QBLK


def _block_softmax(q_pieces, kb, vb, valid, fuse_den):
    s = lax.dot_general(jnp.concatenate(q_pieces, axis=0), kb, NT_DIMS,
                        preferred_element_type=F32)
    maxima, probs = [], []
    for i in range(len(q_pieces)):
        sp = jnp.where(valid, s[i * QBLK:(i + 1) * QBLK, :], NEG_INF)
        mb = jnp.max(sp, axis=1, keepdims=True)
        maxima.append(mb)
        probs.append(jnp.exp(sp - mb).astype(BF16))
    pb = jnp.concatenate(probs, axis=0)
    if fuse_den:
        both = jnp.dot(pb, jnp.concatenate([vb, jnp.ones_like(vb)], axis=1),
                       preferred_element_type=F32)
        return maxima, both[:, 0:LANES], both[:, LANES:2 * LANES]
    num = jnp.dot(pb, vb, preferred_element_type=F32)
    den = jnp.dot(pb, jnp.ones_like(vb), preferred_element_type=F32)
    return maxima, num, den


def _write_tails(kt_ref, vt_ref, k_tail, v_tail):
    for c in range(k_tail.shape[0] // QBLK):
        kt_ref[:, c * QBLK:(c + 1) * QBLK] = k_tail[c * QBLK:(c + 1) * QBLK, :].T
        vt_ref[:, c * QBLK:(c + 1) * QBLK] = v_tail[c * QBLK:(c + 1) * QBLK, :].T


def _dilated_kernel(q_ref, k_ref, v_ref, cos_ref, slo_ref, shi_ref, o_ref, kt_ref, vt_ref,
                    qs, kp, vp, *stat_scratch, dils, t, tail):
    pad = QBLK * max(dils)
    n_pat = len(dils)
    num_refs, stat_refs = stat_scratch[:n_pat], stat_scratch[n_pat:]
    lane = lax.broadcasted_iota(jnp.int32, (1, LANES), 1)
    low = lane < HEAD_DIM

    kp[0:pad, :] = jnp.zeros((pad, LANES), F32)
    vp[0:pad, :] = jnp.zeros((pad, LANES), F32)

    def prep(c, carry):
        r0 = pl.multiple_of(c * PREP_ROWS, PREP_ROWS)
        rows = pl.ds(r0, PREP_ROWS)
        cos, slo, shi = cos_ref[rows, :], slo_ref[rows, :], shi_ref[rows, :]
        qs[rows, :] = _rope(q_ref[rows, :], cos, slo, shi) * ATTN_SCALE
        kp[pl.ds(pad + r0, PREP_ROWS), :] = _rope(k_ref[rows, :], cos, slo, shi)
        vp[pl.ds(pad + r0, PREP_ROWS), :] = v_ref[rows, :]
        return carry

    lax.fori_loop(0, t // PREP_ROWS, prep, 0)
    _write_tails(kt_ref, vt_ref, kp[pad + t - tail:pad + t, :], v_ref[t - tail:t, :])

    band, current = _band_mask()
    for pi, d in enumerate(dils):
        shift = d.bit_length() - 1

        def block(b, carry, d=d, shift=shift, pi=pi):
            n = lax.shift_right_logical(b, jnp.int32(shift))
            q0 = n * (QBLK * d) + (b & (d - 1))
            k0 = q0 + pad - QBLK * d
            if d == 1:
                qsl = pl.ds(pl.multiple_of(q0, QBLK), QBLK)
                ksl = pl.ds(pl.multiple_of(k0, QBLK), 2 * QBLK)
            else:
                qsl = pl.ds(q0, QBLK, stride=d)
                ksl = pl.ds(k0, 2 * QBLK, stride=d)
            q = qs[qsl, :]
            pieces = [jnp.where(low, q, 0.0).astype(BF16),
                      jnp.where(low, 0.0, q).astype(BF16)]
            (m0, m1), num, den = _block_softmax(
                pieces, kp[ksl, :].astype(BF16), vp[ksl, :].astype(BF16),
                band & (current | (n > 0)), fuse_den=True)
            num_refs[pi][qsl, :] = jnp.where(low, num[0:QBLK, :], num[QBLK:2 * QBLK, :])
            stat_refs[pi][qsl, :] = jnp.where(
                lane < STAT_LANES, m0,
                jnp.where(low, den[0:QBLK, :],
                          jnp.where(lane < 3 * STAT_LANES, m1, den[QBLK:2 * QBLK, :])))
            return carry

        lax.fori_loop(0, t // QBLK, block, 0, unroll=BLOCK_UNROLL_DILATED)

    is_max = (lane & (HEAD_DIM - 1)) < STAT_LANES

    def finish(c, carry):
        rows = pl.ds(pl.multiple_of(c * PREP_ROWS, PREP_ROWS), PREP_ROWS)
        ms, ls = [], []
        for r in stat_refs:
            st = r[rows, :]
            ms.append(jnp.where(is_max, st, pltpu.roll(st, STAT_LANES, 1)))
            ls.append(jnp.where(is_max, pltpu.roll(st, LANES - STAT_LANES, 1), st))
        m = functools.reduce(jnp.maximum, ms)
        den = num = None
        for mi, li, r in zip(ms, ls, num_refs):
            w = jnp.exp(mi - m)
            den = w * li if den is None else den + w * li
            num = w * r[rows, :] if num is None else num + w * r[rows, :]
        o_ref[rows, :] = (num / den).astype(BF16)
        return carry

    lax.fori_loop(0, t // PREP_ROWS, finish, 0)


def _attention_dilated(u, tables, *, n_batch, t, dils, q_slab, k_slab, v_slab, n_slab, tail):
    pad = QBLK * max(dils)
    once = pl.Buffered(1)
    table = pl.BlockSpec((t, LANES), lambda b, s: (0, 0), pipeline_mode=once)
    tail_spec = pl.BlockSpec((None, LANES, tail), lambda b, s: (b, s, 0))
    scratch = [pltpu.VMEM((t, LANES), F32)]
    scratch += [pltpu.VMEM((pad + t, LANES), F32) for _ in range(2)]
    scratch += [pltpu.VMEM((t, LANES), F32) for _ in range(2 * len(dils))]
    return pl.pallas_call(
        functools.partial(_dilated_kernel, dils=dils, t=t, tail=tail),
        grid=(n_batch, n_slab),
        in_specs=[
            pl.BlockSpec((t, LANES), lambda b, s: (b, q_slab + s)),
            pl.BlockSpec((t, LANES), lambda b, s: (b, k_slab + s)),
            pl.BlockSpec((t, LANES), lambda b, s: (b, v_slab + s)),
            table, table, table,
        ],
        out_specs=[pl.BlockSpec((t, LANES), lambda b, s: (b, s)), tail_spec, tail_spec],
        out_shape=[
            jax.ShapeDtypeStruct((n_batch * t, n_slab * LANES), BF16),
            jax.ShapeDtypeStruct((n_batch, n_slab * LANES, tail), F32),
            jax.ShapeDtypeStruct((n_batch, n_slab * LANES, tail), F32),
        ],
        scratch_shapes=scratch,
        compiler_params=_cparams(2, 48),
        name="attn_dilated",
    )(u, u, u, *tables)


GQA_SLABS = 4


def _gqa_kernel(q_ref, k_ref, v_ref, cos_ref, slo_ref, shi_ref, sink_ref, o_ref, kt_ref, vt_ref,
                kp, vp, *, t, tail):
    pad = QBLK
    kv_head = pl.program_id(1)
    lane = lax.broadcasted_iota(jnp.int32, (1, LANES), 1)
    low = lane < HEAD_DIM
    keep = lax.shift_right_logical(lane, jnp.int32(6)) == kv_head

    kp[0:pad, :] = jnp.zeros((pad, LANES), BF16)
    vp[0:pad, :] = jnp.zeros((pad, LANES), BF16)

    def prep(c, carry):
        r0 = pl.multiple_of(c * PREP_ROWS, PREP_ROWS)
        rows = pl.ds(r0, PREP_ROWS)
        kr = _rope(k_ref[rows, :], cos_ref[rows, :], slo_ref[rows, :], shi_ref[rows, :])
        v = v_ref[rows, :]
        prow = pl.ds(pad + r0, PREP_ROWS)
        kp[prow, :] = jnp.where(keep, kr, pltpu.roll(kr, HEAD_DIM, 1)).astype(BF16)
        vp[prow, :] = jnp.where(keep, v, pltpu.roll(v, HEAD_DIM, 1)).astype(BF16)
        return carry

    lax.fori_loop(0, t // PREP_ROWS, prep, 0)
    trows = slice(t - tail, t)
    _write_tails(kt_ref, vt_ref,
                 _rope(k_ref[trows, :], cos_ref[trows, :], slo_ref[trows, :], shi_ref[trows, :]),
                 v_ref[trows, :])

    band, current = _band_mask()

    def block(b, carry):
        q0 = pl.multiple_of(b * QBLK, QBLK)
        rows = pl.ds(q0, QBLK)
        cos, slo, shi = cos_ref[rows, :], slo_ref[rows, :], shi_ref[rows, :]
        pieces = []
        for s in range(GQA_SLABS):
            q = _rope(q_ref[rows, s * LANES:(s + 1) * LANES], cos, slo, shi) * ATTN_SCALE
            pieces += [jnp.where(low, q, 0.0).astype(BF16), jnp.where(low, 0.0, q).astype(BF16)]
        ksl = pl.ds(q0, 2 * QBLK)
        maxima, num, den = _block_softmax(pieces, kp[ksl, :], vp[ksl, :],
                                          band & (current | (b > 0)), fuse_den=False)
        for s in range(GQA_SLABS):
            outs = []
            for hh in range(2):
                i = 2 * s + hh
                piece = slice(i * QBLK, (i + 1) * QBLK)
                sink = sink_ref[:, s * LANES + hh * HEAD_DIM:s * LANES + hh * HEAD_DIM + 1]
                outs.append(num[piece, :] / (den[piece, :] + jnp.exp(sink - maxima[i])))
            o_ref[rows, s * LANES:(s + 1) * LANES] = jnp.where(low, outs[0], outs[1]).astype(BF16)
        return carry

    lax.fori_loop(0, t // QBLK, block, 0, unroll=BLOCK_UNROLL_GQA)


def _attention_gqa(u, tables, sinks, *, n_batch, t, q_slab, k_slab, v_slab, tail):
    width = GQA_SLABS * LANES
    once = pl.Buffered(1)
    table = pl.BlockSpec((t, LANES), lambda b, g: (0, 0), pipeline_mode=once)
    tail_spec = pl.BlockSpec((None, LANES, tail), lambda b, g: (b, 0, 0))
    q_blk = q_slab // GQA_SLABS
    return pl.pallas_call(
        functools.partial(_gqa_kernel, t=t, tail=tail),
        grid=(n_batch, N_KV_B),
        in_specs=[
            pl.BlockSpec((t, width), lambda b, g: (b, q_blk + g)),
            pl.BlockSpec((t, LANES), lambda b, g: (b, k_slab)),
            pl.BlockSpec((t, LANES), lambda b, g: (b, v_slab)),
            table, table, table,
            pl.BlockSpec((None, 1, width), lambda b, g: (g, 0, 0)),
        ],
        out_specs=[pl.BlockSpec((t, width), lambda b, g: (b, g)), tail_spec, tail_spec],
        out_shape=[
            jax.ShapeDtypeStruct((n_batch * t, N_KV_B * width), BF16),
            jax.ShapeDtypeStruct((n_batch, LANES, tail), F32),
            jax.ShapeDtypeStruct((n_batch, LANES, tail), F32),
        ],
        scratch_shapes=[pltpu.VMEM((QBLK + t, LANES), BF16), pltpu.VMEM((QBLK + t, LANES), BF16)],
        compiler_params=_cparams(2, 48),
        name="attn_gqa",
    )(u, u, u, *tables, sinks)


CONV_TB = 256
CONV_HALO = 32
CONV_RC = 32


def _layer_norm_swish(y, g, b):
    mu = jnp.mean(y, axis=-1, keepdims=True)
    yc = y - mu
    var = jnp.mean(yc * yc, axis=-1, keepdims=True)
    z = yc * lax.rsqrt(var + LN_EPS) * g + b
    return z * _sigmoid(z)


def _conv_kernel(v0, v1, g0, g1, cw_ref, cb_ref, lg_ref, lb_ref, o_ref, st_ref, ext, shifted):
    tt = pl.program_id(1)
    half = D_CONV // 2

    @pl.when(tt == 0)
    def _():
        ext[0:CONV_HALO, :] = jnp.zeros((CONV_HALO, D_CONV), F32)

    @pl.when(tt > 0)
    def _():
        ext[0:CONV_HALO, :] = ext[CONV_TB:CONV_TB + CONV_HALO, :]

    ext[CONV_HALO:CONV_HALO + CONV_TB, 0:half] = v0[...] * _sigmoid(g0[...])
    ext[CONV_HALO:CONV_HALO + CONV_TB, half:D_CONV] = v1[...] * _sigmoid(g1[...])
    st_ref[...] = ext[CONV_TB:CONV_TB + CONV_HALO, :]

    span = CONV_TB + CONV_HALO - SUBLANES
    for s in range(1, SUBLANES):
        shifted[s - 1, 0:span, :] = ext[s:s + span, :]

    lead = CONV_HALO - (CONV_WIDTH - 1)
    for c in range(CONV_TB // CONV_RC):
        r0 = c * CONV_RC
        y = jnp.zeros((CONV_RC, D_CONV), F32) + cb_ref[...]
        for w in range(CONV_WIDTH):
            s = (lead + w) % SUBLANES
            base = r0 + lead + w - s
            win = ext[base:base + CONV_RC, :] if s == 0 else shifted[s - 1, base:base + CONV_RC, :]
            y = y + win * cw_ref[w:w + 1, :]
        o_ref[r0:r0 + CONV_RC, :] = _layer_norm_swish(y, lg_ref[...], lb_ref[...]).astype(BF16)


def _conv_module(u, cw, cb, lg, lb, *, n_batch, t):
    nt = t // CONV_TB
    half = D_CONV // 2
    v_blk, g_blk = COL_VAL // half, COL_GATE // half
    row = lambda b, i: b * nt + i
    vec = pl.BlockSpec((1, D_CONV), lambda b, i: (0, 0))
    return pl.pallas_call(
        _conv_kernel,
        grid=(n_batch, nt),
        in_specs=[
            pl.BlockSpec((CONV_TB, half), lambda b, i: (row(b, i), v_blk)),
            pl.BlockSpec((CONV_TB, half), lambda b, i: (row(b, i), v_blk + 1)),
            pl.BlockSpec((CONV_TB, half), lambda b, i: (row(b, i), g_blk)),
            pl.BlockSpec((CONV_TB, half), lambda b, i: (row(b, i), g_blk + 1)),
            pl.BlockSpec((CONV_HALO, D_CONV), lambda b, i: (0, 0)),
            vec, vec, vec,
        ],
        out_specs=[
            pl.BlockSpec((CONV_TB, D_CONV), lambda b, i: (row(b, i), 0)),
            pl.BlockSpec((CONV_HALO, D_CONV), lambda b, i: (b, 0)),
        ],
        out_shape=[
            jax.ShapeDtypeStruct((n_batch * t, D_CONV), BF16),
            jax.ShapeDtypeStruct((n_batch * CONV_HALO, D_CONV), F32),
        ],
        scratch_shapes=[pltpu.VMEM((CONV_HALO + CONV_TB, D_CONV), F32),
                        pltpu.VMEM((SUBLANES - 1, CONV_HALO + CONV_TB, D_CONV), F32)],
        compiler_params=_cparams(2, 32),
        name="conv_module",
    )(u, u, u, u, cw, cb, lg, lb)


ROPE_SLABS = tuple(range(SLAB_QA, SLAB_VA)) + tuple(range(SLAB_QB, SLAB_VB))
Q_SLABS = tuple(range(SLAB_QA, SLAB_KA)) + tuple(range(SLAB_QB, SLAB_KB))
VAL_SLABS = tuple(range(COL_VAL // LANES, COL_GATE // LANES))


def _sample_post_kernel(u_ref, cos_ref, slo_ref, shi_ref, o_ref):
    for s in range(N_IN // LANES):
        x = u_ref[:, s * LANES:(s + 1) * LANES]
        if s in ROPE_SLABS:
            x = _rope(x, cos_ref[...], slo_ref[...], shi_ref[...])
            if s in Q_SLABS:
                x = x * ATTN_SCALE
        elif s in VAL_SLABS:
            gs = s + D_CONV // LANES
            x = x * _sigmoid(u_ref[:, gs * LANES:(gs + 1) * LANES])
        o_ref[:, s * LANES:(s + 1) * LANES] = x


def _sample_post(u, tables):
    m, n = u.shape
    full = pl.BlockSpec((m, n), lambda i: (0, 0))
    vec = pl.BlockSpec((1, LANES), lambda i: (0, 0))
    return pl.pallas_call(
        _sample_post_kernel,
        grid=(1,),
        in_specs=[full, vec, vec, vec],
        out_specs=full,
        out_shape=jax.ShapeDtypeStruct((m, n), F32),
        compiler_params=_cparams(1, 16),
        name="sample_post",
    )(u, *tables)


def _pattern_multiplicity(la):
    dist = la - jnp.arange(la, dtype=jnp.int32)
    mult = jnp.zeros((la,), F32)
    for d in DILATIONS_A:
        mult = mult + ((dist % d == 0) & (dist <= HOPS * d)).astype(F32)
    return mult.reshape(1, la)


def _sattn_shift_kernel(kt_ref, vt_ref, q_ref, knr_ref, vnr_ref, knc_ref, vnc_ref, mult_ref,
                        *rest):
    o_ref, ko_ref, vo_ref = rest[-3:]
    nh, e, la = kt_ref.shape
    w = nh * e
    k = kt_ref[...].reshape(w, la)
    v = vt_ref[...].reshape(w, la)
    q = q_ref[...]
    mult = mult_ref[...]
    s = jnp.dot(q.astype(BF16), k.astype(BF16), preferred_element_type=F32)
    s = jnp.where(mult > 0.0, s, NEG_INF)
    s_new = jnp.sum(q * knr_ref[...], axis=1, keepdims=True)
    m = jnp.maximum(jnp.max(s, axis=1, keepdims=True), s_new)
    p = jnp.exp(s - m) * mult
    p_new = jnp.exp(s_new - m) * float(len(DILATIONS_A))
    den = jnp.sum(p, axis=1, keepdims=True) + p_new
    num = lax.dot_general(p.astype(BF16), v.astype(BF16), NT_DIMS, preferred_element_type=F32)
    o_ref[...] = (num + p_new * vnr_ref[...]) / den
    newest = lax.broadcasted_iota(jnp.int32, (1, la), 1) == la - 1
    ko_ref[...] = jnp.where(newest, knc_ref[...], pltpu.roll(k, la - 1, 1)).reshape(nh, e, la)
    vo_ref[...] = jnp.where(newest, vnc_ref[...], pltpu.roll(v, la - 1, 1)).reshape(nh, e, la)


def _sample_attn_shift(kt, vt, prev, layer, q_bd, kn, vn, mult):
    depth, nb, nh, e, la = kt.shape
    w = nh * e
    slab = pl.BlockSpec((None, None, nh, e, la), lambda b: (layer, b, 0, 0, 0))
    tokq = pl.BlockSpec((None, nh, w), lambda b: (b, 0, 0))
    row = pl.BlockSpec((None, 1, w), lambda b: (b, 0, 0))
    col = pl.BlockSpec((None, w, 1), lambda b: (b, 0, 0))
    in_specs = [slab, slab, tokq, row, row, col, col, pl.BlockSpec((1, la), lambda b: (0, 0))]
    args = [kt, vt, q_bd, kn.reshape(nb, 1, w), vn.reshape(nb, 1, w),
            kn.reshape(nb, w, 1), vn.reshape(nb, w, 1), mult]
    aliases = {}
    if prev is not None:
        aliases = {len(args): 1, len(args) + 1: 2}
        in_specs += [pl.BlockSpec(memory_space=pl.ANY)] * 2
        args += list(prev)
    return pl.pallas_call(
        _sattn_shift_kernel,
        grid=(nb,),
        in_specs=in_specs,
        out_specs=[tokq, slab, slab],
        out_shape=[jax.ShapeDtypeStruct((nb, nh, w), F32),
                   jax.ShapeDtypeStruct(kt.shape, F32), jax.ShapeDtypeStruct(vt.shape, F32)],
        input_output_aliases=aliases,
        compiler_params=_cparams(1, 56),
        name="sample_attn_shift",
    )(*args)


def _sattn_b_kernel(kc_ref, vc_ref, q_ref, kn_ref, vn_ref, sink_ref, o_ref):
    q = q_ref[...]
    s = lax.dot_general(q.astype(BF16), kc_ref[...].astype(BF16), NT_DIMS,
                        preferred_element_type=F32)
    s_new = jnp.sum(q * kn_ref[...], axis=1, keepdims=True)
    m = jnp.maximum(jnp.max(s, axis=1, keepdims=True), s_new)
    p = jnp.exp(s - m)
    p_new = jnp.exp(s_new - m)
    den = jnp.sum(p, axis=1, keepdims=True) + p_new + jnp.exp(sink_ref[...] - m)
    num = jnp.dot(p.astype(BF16), vc_ref[...].astype(BF16), preferred_element_type=F32)
    o_ref[...] = (num + p_new * vn_ref[...]) / den


def _sample_attn_b(kc, vc, layer, q_exp, kn, vn, sinks):
    _, nb, lb, w = kc.shape
    nh = q_exp.shape[1]
    cache = pl.BlockSpec((None, None, lb, w), lambda b: (layer, b, 0, 0))
    tokq = pl.BlockSpec((None, nh, w), lambda b: (b, 0, 0))
    tok1 = pl.BlockSpec((None, 1, w), lambda b: (b, 0, 0))
    return pl.pallas_call(
        _sattn_b_kernel,
        grid=(nb,),
        in_specs=[cache, cache, tokq, tok1, tok1, pl.BlockSpec((nh, 1), lambda b: (0, 0))],
        out_specs=tokq,
        out_shape=jax.ShapeDtypeStruct((nb, nh, w), F32),
        compiler_params=_cparams(1, 16),
        name="sample_attn_shared",
    )(kc, vc, q_exp, kn, vn, sinks)


def _sconv_kernel(st_ref, glu_ref, cw_ref, cb_ref, lg_ref, lb_ref, o_ref, ns_ref):
    hist = CONV_WIDTH - 1
    glu = glu_ref[...]
    y = glu * cw_ref[hist:hist + 1, :] + cb_ref[...]
    for w in range(hist):
        y = y + st_ref[:, w * D_CONV:(w + 1) * D_CONV] * cw_ref[w:w + 1, :]
    o_ref[...] = _layer_norm_swish(y, lg_ref[...], lb_ref[...]).astype(BF16)
    ns_ref[:, 0:(hist - 1) * D_CONV] = st_ref[:, D_CONV:hist * D_CONV]
    ns_ref[:, (hist - 1) * D_CONV:hist * D_CONV] = glu


def _sample_conv(state2d, layer, glu, cw, cb, lg, lb):
    _, nb, width = state2d.shape
    st = pl.BlockSpec((None, nb, width), lambda i: (layer, 0, 0))
    tok = pl.BlockSpec((nb, D_CONV), lambda i: (0, 0))
    vec = pl.BlockSpec((1, D_CONV), lambda i: (0, 0))
    return pl.pallas_call(
        _sconv_kernel,
        grid=(1,),
        in_specs=[st, tok, pl.BlockSpec((CONV_HALO, D_CONV), lambda i: (0, 0)), vec, vec, vec],
        out_specs=[tok, pl.BlockSpec((nb, width), lambda i: (0, 0))],
        out_shape=[jax.ShapeDtypeStruct((nb, D_CONV), BF16),
                   jax.ShapeDtypeStruct((nb, width), F32)],
        compiler_params=_cparams(1, 16),
        name="sample_conv",
    )(state2d, glu, cw, cb, lg, lb)


def _shift_kernel(km, kx, kn, vm, vx, vn, ko, vo):
    last = pl.program_id(2) == pl.num_programs(2) - 1
    for main, nxt, new, out in ((km, kx, kn, ko), (vm, vx, vn, vo)):
        rb = main.shape[1]
        out[:, 0:rb - 1] = main[:, 1:rb]
        out[:, rb - 1:rb] = jnp.where(last, new[...], nxt[...])


def _shift_caches(cache_k, cache_v, new_k, new_v, *, batch_chunk, row_block):
    depth, nb, rows, nh, e = cache_k.shape
    main = pl.BlockSpec((None, batch_chunk, row_block, nh, e), lambda l, b, r: (l, b, r, 0, 0))
    nxt = pl.BlockSpec((None, batch_chunk, 1, nh, e),
                       lambda l, b, r: (l, b, jnp.minimum((r + 1) * row_block, rows - 1), 0, 0))
    new = pl.BlockSpec((None, batch_chunk, 1, nh, e), lambda l, b, r: (l, b, 0, 0, 0))
    shape = jax.ShapeDtypeStruct(cache_k.shape, cache_k.dtype)
    return pl.pallas_call(
        _shift_kernel,
        grid=(depth, nb // batch_chunk, rows // row_block),
        in_specs=[main, nxt, new, main, nxt, new],
        out_specs=[main, main],
        out_shape=[shape, shape],
        compiler_params=_cparams(3, 48),
        name="cache_shift",
    )(cache_k, cache_k, new_k, cache_v, cache_v, new_v)


def _rope_tables(pos):
    n = pos.shape[0]
    inv_freq = 1.0 / (ROPE_THETA ** (jnp.arange(ROT_HALF, dtype=F32) / ROT_HALF))
    ang = pos.astype(F32)[:, None] * inv_freq[None, :]
    cos, sin = jnp.cos(ang), jnp.sin(ang)
    rest = HEAD_DIM - 2 * ROT_HALF
    c = jnp.concatenate([cos, cos, jnp.ones((n, rest), F32)], axis=1)
    lo = jnp.concatenate([-sin, jnp.zeros((n, HEAD_DIM - ROT_HALF), F32)], axis=1)
    hi = jnp.concatenate([jnp.zeros((n, ROT_HALF), F32), sin, jnp.zeros((n, rest), F32)], axis=1)
    return tuple(jnp.tile(x, (1, LANES // HEAD_DIM)) for x in (c, lo, hi))


def kernel(x_prompt, x_sample, c_prompt, c_sample, cache_a_k, cache_a_v, cache_b_k, cache_b_v,
           state_c_conv, w_ada, b_ada, g_pre, g_post, w_ffn_gu, w_ffn_down, w_in, w_out,
           attn_sinks, conv_w, conv_b, conv_ln_g, conv_ln_b):
    nbp, t, d = x_prompt.shape
    nbs = x_sample.shape[0]
    mp = nbp * t
    depth = w_ada.shape[0]
    la, lb = cache_a_k.shape[2], cache_b_k.shape[2]
    hist = CONV_WIDTH - 1
    tm_p = 1024

    xp = x_prompt.reshape(mp, d)
    xs = x_sample.reshape(nbs, d)

    s_row = 32
    c_all = jnp.zeros((s_row + nbs, d), F32).at[:nbp].set(c_prompt).at[s_row:].set(c_sample)
    mod_all = _ada_mod(c_all, w_ada, b_ada)

    def mod_p(l, s):
        return mod_all[l, :nbp, 3 * s * d:3 * (s + 1) * d].reshape(nbp, 3, 1, d)

    def mod_s(l, s):
        m = mod_all[l, s_row:, 3 * s * d:3 * (s + 1) * d].reshape(nbs, 3, d)
        return jnp.transpose(m, (1, 0, 2))[None]

    def nxt(l, s):
        return (l, s + 1) if s + 1 < N_SUB else (l + 1, 0)

    tab_p = _rope_tables(jnp.arange(t, dtype=jnp.int32))
    tab_s = _rope_tables(PAST_LEN + jnp.arange(1, dtype=jnp.int32))
    sink_groups = jnp.repeat(attn_sinks, HEAD_DIM, axis=1).reshape(depth, N_KV_B, 1, -1)
    sink_cols = attn_sinks.reshape(depth, N_HEADS_B, 1)
    cw_pad = jnp.pad(conv_w, ((0, 0), (0, CONV_HALO - CONV_WIDTH), (0, 0)))
    vec = lambda a, l: a[l].reshape(1, -1)

    cache_bk2 = cache_b_k.reshape(depth, nbs, lb, N_KV_B * HEAD_DIM)
    cache_bv2 = cache_b_v.reshape(depth, nbs, lb, N_KV_B * HEAD_DIM)
    state2d = state_c_conv.reshape(depth, nbs, hist * D_CONV)
    kv_of_head = (jnp.arange(N_HEADS_B) // (N_HEADS_B // N_KV_B))[:, None] == jnp.arange(N_KV_B)
    cache_akt = jnp.transpose(cache_a_k, (0, 1, 3, 4, 2))
    cache_avt = jnp.transpose(cache_a_v, (0, 1, 3, 4, 2))
    mult_a = _pattern_multiplicity(la)
    head_eye = jnp.eye(N_HEADS_A, dtype=bool)[None, :, :, None]

    hp = _prenorm(xp, mod_p(0, 0), vec(g_pre[0], 0), tm=tm_p, rows_per_batch=t)
    hs = _prenorm(xs, mod_s(0, 0), vec(g_pre[0], 0), tm=nbs, rows_per_batch=nbs)

    st_p = [[] for _ in range(5)]
    new_rows = [[], []]
    shift_a = None
    st_c_s = []

    f_pad = pl.cdiv(w_ffn_down.shape[2], TF) * TF
    wd_bf = _cast_pad_rows(w_ffn_down.reshape(depth * 2, -1, d), f_pad)
    wo_bf = _cast_pad_rows(w_out, w_out.shape[1])

    def resid(lhs, s, x, l, weight, mod_fn, tm, rpb):
        ln, sn = nxt(l, s)
        last = ln >= depth
        if s == 1:
            lhs, w, mat, tk = jnp.concatenate(lhs, axis=1), wo_bf, l, TF
        else:
            w, mat, tk = wd_bf, 2 * l + s // 2, f_pad // 4
        return _resid_proj(
            lhs, w, mat, x, mod_fn(l, s), vec(g_post[l], s),
            None if last else mod_fn(ln, sn), None if last else vec(g_pre[ln], sn),
            tm=tm, tk=tk, rows_per_batch=rpb, weight=weight)

    for l in range(depth):
        a, a_s = _ffn_up(hp, hs, w_ffn_gu, (l, 0), tm=tm_p)
        xp, hp = resid(a, 0, xp, l, 0.5, mod_p, tm_p, t)
        xs, hs = resid(a_s, 0, xs, l, 0.5, mod_s, nbs, nbs)

        u = _in_proj(hp, w_in, l, tm=tm_p)
        oa, kta, vta = _attention_dilated(
            u, tab_p, n_batch=nbp, t=t, dils=DILATIONS_A, q_slab=SLAB_QA, k_slab=SLAB_KA,
            v_slab=SLAB_VA, n_slab=N_HEADS_A // 2, tail=min(la, t))
        ob, ktb, vtb = _attention_gqa(
            u, tab_p, sink_groups[l], n_batch=nbp, t=t, q_slab=SLAB_QB, k_slab=SLAB_KB,
            v_slab=SLAB_VB, tail=min(lb, t))
        oc, cst = _conv_module(u, cw_pad[l], vec(conv_b, l), vec(conv_ln_g, l), vec(conv_ln_b, l),
                               n_batch=nbp, t=t)
        xp, hp = resid([oa, ob, oc], 1, xp, l, 1.0, mod_p, tm_p, t)

        for dst, tail_t, heads in ((st_p[0], kta, N_HEADS_A), (st_p[1], vta, N_HEADS_A),
                                   (st_p[2], ktb, N_KV_B), (st_p[3], vtb, N_KV_B)):
            dst.append(jnp.transpose(tail_t.reshape(nbp, heads, HEAD_DIM, -1), (0, 3, 1, 2)))
        st_p[4].append(cst.reshape(nbp, CONV_HALO, D_CONV)[:, CONV_HALO - hist:])

        ur = _sample_post(_in_proj(hs, w_in, l, tm=nbs), tab_s)
        seg = lambda lo, hi: ur[:, lo * LANES:hi * LANES]
        qa = seg(SLAB_QA, SLAB_KA).reshape(nbs, 1, N_HEADS_A, HEAD_DIM)
        q_bd = jnp.where(head_eye, qa, 0.0).reshape(nbs, N_HEADS_A, N_HEADS_A * HEAD_DIM)
        oa3, *shift_a = _sample_attn_shift(cache_akt, cache_avt, shift_a, l, q_bd,
                                           seg(SLAB_KA, SLAB_VA), seg(SLAB_VA, SLAB_QB), mult_a)
        oa = jnp.sum(jnp.where(head_eye, oa3.reshape(nbs, N_HEADS_A, N_HEADS_A, HEAD_DIM), 0.0),
                     axis=1)
        qb = seg(SLAB_QB, SLAB_KB).reshape(nbs, N_HEADS_B, 1, HEAD_DIM)
        q_exp = jnp.where(kv_of_head[None, :, :, None], qb, 0.0).reshape(nbs, N_HEADS_B, LANES)
        knb = seg(SLAB_KB, SLAB_VB)
        vnb = seg(SLAB_VB, SLAB_VB + 1)
        ob2 = _sample_attn_b(cache_bk2, cache_bv2, l, q_exp, knb.reshape(nbs, 1, LANES),
                             vnb.reshape(nbs, 1, LANES), sink_cols[l])
        ob = jnp.sum(jnp.where(kv_of_head[None, :, :, None],
                               ob2.reshape(nbs, N_HEADS_B, N_KV_B, HEAD_DIM), 0.0), axis=2)
        glu = ur[:, COL_VAL:COL_GATE]
        oc, ns = _sample_conv(state2d, l, glu, cw_pad[l], vec(conv_b, l), vec(conv_ln_g, l),
                              vec(conv_ln_b, l))
        mix = [oa.reshape(nbs, -1).astype(BF16), ob.reshape(nbs, -1).astype(BF16), oc]
        xs, hs = resid(mix, 1, xs, l, 1.0, mod_s, nbs, nbs)

        a, a_s = _ffn_up(hp, hs, w_ffn_gu, (l, 1), tm=tm_p)
        xp, hp = resid(a, 2, xp, l, 0.5, mod_p, tm_p, t)
        xs, hs = resid(a_s, 2, xs, l, 0.5, mod_s, nbs, nbs)

        new_rows[0].append(knb.reshape(nbs, 1, N_KV_B, HEAD_DIM))
        new_rows[1].append(vnb.reshape(nbs, 1, N_KV_B, HEAD_DIM))
        st_c_s.append(ns.reshape(nbs, hist, D_CONV))

    new_rows = [jnp.stack(r) for r in new_rows]
    shift_a = [jnp.transpose(c, (0, 1, 4, 2, 3)) for c in shift_a]
    shift_b = _shift_caches(cache_b_k, cache_b_v, new_rows[0], new_rows[1],
                            batch_chunk=8, row_block=lb)
    return (xp.reshape(nbp, t, d), xs.reshape(nbs, 1, d),
            *[jnp.stack(s) for s in st_p], *shift_a, *shift_b, jnp.stack(st_c_s))
```

```python
import functools

import jax
import jax.numpy as jnp
from jax import lax
from jax.experimental import pallas as pl
from jax.experimental.pallas import tpu as pltpu

F32 = jnp.float32
BF16 = jnp.bfloat16

D_MODEL = 2048
DEPTH = 4
HEAD_DIM = 64
N_HEADS_A = 8
N_HEADS_B = 16
N_KV_B = 2
D_CONV = 512
DILATIONS_A = (1, 4, 16)
HOPS = 128
CONV_WIDTH = 31
ROT_HALF = 8
ROPE_THETA = 500000.0
PAST_LEN = 16384
RMS_EPS = 1e-6
LN_EPS = 1e-5
NEG_INF = -1e30
ATTN_SCALE = HEAD_DIM ** -0.5
N_SUB = 3

LANES = 128
SUBLANES = 8
MIB = 1 << 20

SLAB_QA, SLAB_KA, SLAB_VA = 0, 4, 8
SLAB_QB, SLAB_KB, SLAB_VB = 12, 20, 21
COL_VAL, COL_GATE = 2816, 3328
N_IN = 3840


def _cparams(n_axes, vmem_mib):
    return pltpu.CompilerParams(
        dimension_semantics=("arbitrary",) * n_axes,
        vmem_limit_bytes=vmem_mib * MIB,
    )


def _sigmoid(x):
    return jax.nn.sigmoid(x)


def _pre_norm(x, g, shift, scale):
    ms = jnp.mean(x * x, axis=-1, keepdims=True)
    return (x * lax.rsqrt(ms + RMS_EPS)) * g * (1.0 + scale) + shift


def _gated_post(y, g, gate, weight):
    ms = jnp.mean(y * y, axis=-1, keepdims=True)
    return (weight * gate) * ((y * lax.rsqrt(ms + RMS_EPS)) * g)


def _rope(x, cos, sin_lo, sin_hi):
    return x * cos + pltpu.roll(x, LANES - ROT_HALF, 1) * sin_lo + pltpu.roll(x, ROT_HALF, 1) * sin_hi


def _ada_kernel(c_ref, w_ref, b_ref, o_ref):
    c = c_ref[...]
    a = (c * _sigmoid(c)).astype(BF16)
    o_ref[...] = jnp.dot(a, w_ref[...].astype(BF16), preferred_element_type=F32) + b_ref[...]


def _ada_mod(c_all, w_ada, b_ada):
    depth, d, n = w_ada.shape
    rows = c_all.shape[0]
    tn = 1024
    return pl.pallas_call(
        _ada_kernel,
        grid=(depth, n // tn),
        in_specs=[
            pl.BlockSpec((rows, d), lambda l, j: (0, 0)),
            pl.BlockSpec((None, d, tn), lambda l, j: (l, 0, j)),
            pl.BlockSpec((None, 1, tn), lambda l, j: (l, 0, j)),
        ],
        out_specs=pl.BlockSpec((None, rows, tn), lambda l, j: (l, 0, j)),
        out_shape=jax.ShapeDtypeStruct((depth, rows, n), F32),
        compiler_params=_cparams(2, 40),
        name="ada_mod",
    )(c_all, w_ada, b_ada.reshape(depth, 1, n))


def _prenorm_kernel(x_ref, mod_ref, g_ref, h_ref):
    h_ref[...] = _pre_norm(x_ref[...], g_ref[...], mod_ref[0], mod_ref[1]).astype(BF16)


def _mod_spec(mod, rows_per_batch, tm):
    _, _, r, d = mod.shape
    if r == 1:
        per = rows_per_batch // tm
        return pl.BlockSpec((None, 3, 1, d), lambda i, *_: (i // per, 0, 0, 0))
    return pl.BlockSpec((None, 3, r, d), lambda i, *_: (0, 0, 0, 0))


def _prenorm(x, mod, g, *, tm, rows_per_batch):
    m, d = x.shape
    return pl.pallas_call(
        _prenorm_kernel,
        grid=(m // tm,),
        in_specs=[
            pl.BlockSpec((tm, d), lambda i: (i, 0)),
            _mod_spec(mod, rows_per_batch, tm),
            pl.BlockSpec((1, d), lambda i: (0, 0)),
        ],
        out_specs=pl.BlockSpec((tm, d), lambda i: (i, 0)),
        out_shape=jax.ShapeDtypeStruct((m, d), BF16),
        compiler_params=_cparams(1, 32),
        name="prenorm",
    )(x, mod, g)


W_BLK = 128
W_STREAMS = 4
TF = W_BLK * W_STREAMS


def _ffn_up_kernel(h_ref, hs_ref, *refs, f_valid):
    g_refs = refs[0:W_STREAMS]
    u_refs = refs[W_STREAMS:2 * W_STREAMS]
    a_ref, as_ref, wg_s, wu_s = refs[2 * W_STREAMS:]
    j = pl.program_id(0)

    def gated(h):
        g = jnp.dot(h, wg_s[...], preferred_element_type=F32)
        u = jnp.dot(h, wu_s[...], preferred_element_type=F32)
        a = (g * _sigmoid(g)) * u
        col = lax.broadcasted_iota(jnp.int32, a.shape, 1)
        return jnp.where(col < f_valid - j * TF, a, 0.0).astype(BF16)

    @pl.when(pl.program_id(1) == 0)
    def _():
        for q in range(W_STREAMS):
            wg_s[:, q * W_BLK:(q + 1) * W_BLK] = g_refs[q][...].astype(BF16)
            wu_s[:, q * W_BLK:(q + 1) * W_BLK] = u_refs[q][...].astype(BF16)
        as_ref[...] = gated(hs_ref[...])

    a_ref[...] = gated(h_ref[...])


def _ffn_up(h, hs, w_gu, lead, *, tm):
    m, d = h.shape
    ms = hs.shape[0]
    f = w_gu.shape[-1] // 2
    squeezed = (None,) * len(lead)
    nb = f // W_BLK
    nj = pl.cdiv(f, TF)

    def wspec(base, q):
        return pl.BlockSpec(
            squeezed + (d, W_BLK),
            lambda j, i: lead + (0, base + jnp.minimum(W_STREAMS * j + q, nb - 1)))

    in_specs = [pl.BlockSpec((tm, d), lambda j, i: (i, 0)),
                pl.BlockSpec((ms, d), lambda j, i: (0, 0))]
    in_specs += [wspec(0, q) for q in range(W_STREAMS)]
    in_specs += [wspec(nb, q) for q in range(W_STREAMS)]
    return pl.pallas_call(
        functools.partial(_ffn_up_kernel, f_valid=f),
        grid=(nj, m // tm),
        in_specs=in_specs,
        out_specs=[pl.BlockSpec((tm, TF), lambda j, i: (i, j)),
                   pl.BlockSpec((ms, TF), lambda j, i: (0, j))],
        out_shape=[jax.ShapeDtypeStruct((m, nj * TF), BF16),
                   jax.ShapeDtypeStruct((ms, nj * TF), BF16)],
        scratch_shapes=[pltpu.VMEM((d, TF), BF16), pltpu.VMEM((d, TF), BF16)],
        compiler_params=_cparams(2, 48),
        name="ffn_up",
    )(h, hs, *([w_gu] * (2 * W_STREAMS)))


EPI_ROWS = 256


def _cast_kernel(w_ref, o_ref, *, k_valid):
    rows = w_ref.shape[1]
    row = lax.broadcasted_iota(jnp.int32, w_ref.shape, 1)
    o_ref[...] = jnp.where(row < k_valid - pl.program_id(0) * rows, w_ref[...], 0.0).astype(BF16)


def _cast_pad_rows(w, k_pad):
    n, k, d = w.shape
    last = k // W_BLK - 1
    return pl.pallas_call(
        functools.partial(_cast_kernel, k_valid=k),
        grid=(k_pad // W_BLK,),
        in_specs=[pl.BlockSpec((n, W_BLK, d), lambda j: (0, jnp.minimum(j, last), 0))],
        out_specs=pl.BlockSpec((n, W_BLK, d), lambda j: (0, j, 0)),
        out_shape=jax.ShapeDtypeStruct((n, k_pad, d), BF16),
        compiler_params=_cparams(1, 40),
        name="cast_weights",
    )(w)


def _resid_kernel(*refs, nk, nxc, weight, emit_h, tm, per_token):
    it = iter(refs)
    lhs_ref, w_ref, xin_ref, modc_ref, gpost_ref = [next(it) for _ in range(5)]
    modn_ref = gpre_ref = None
    if emit_h:
        modn_ref, gpre_ref = next(it), next(it)
    x_out = next(it)
    h_out = next(it) if emit_h else None
    acc = next(it)
    k = pl.program_id(1)

    if nxc > 1:
        xr = tm // nxc

        @pl.when(k < nxc)
        def _():
            x_out[pl.ds(pl.multiple_of(k * xr, xr), xr), :] = xin_ref[...]

    @pl.when(k == 0)
    def _():
        acc[...] = jnp.dot(lhs_ref[...], w_ref[...], preferred_element_type=F32)

    @pl.when(k > 0)
    def _():
        acc[...] += jnp.dot(lhs_ref[...], w_ref[...], preferred_element_type=F32)

    def epilogue(rows):
        y = acc[rows, :]
        x = xin_ref[rows, :] if nxc == 1 else x_out[rows, :]
        if per_token:
            gate = modc_ref[2, rows, :]
        else:
            gate = modc_ref[2]
        xn = x + _gated_post(y, gpost_ref[...], gate, weight)
        x_out[rows, :] = xn
        if emit_h:
            if per_token:
                shift, scale = modn_ref[0, rows, :], modn_ref[1, rows, :]
            else:
                shift, scale = modn_ref[0], modn_ref[1]
            h_out[rows, :] = _pre_norm(xn, gpre_ref[...], shift, scale).astype(BF16)

    @pl.when(k == nk - 1)
    def _():
        if tm <= EPI_ROWS:
            epilogue(slice(None))
        else:
            def body(c, carry):
                epilogue(pl.ds(pl.multiple_of(c * EPI_ROWS, EPI_ROWS), EPI_ROWS))
                return carry
            lax.fori_loop(0, tm // EPI_ROWS, body, 0)


def _resid_proj(lhs, w, mat, x, modc, gpost, modn, gpre, *, tm, tk, rows_per_batch, weight):
    m, d = x.shape
    emit_h = modn is not None
    nk = lhs.shape[1] // tk
    per_token = modc.shape[2] != 1
    nxc = min(nk, 4) if tm >= 512 else 1

    in_specs = [
        pl.BlockSpec((tm, tk), lambda i, k: (i, k)),
        pl.BlockSpec((None, tk, d), lambda i, k: (mat, k, 0)),
    ]
    if nxc > 1:
        in_specs.append(pl.BlockSpec(
            (tm // nxc, d), lambda i, k: (i * nxc + jnp.minimum(k, nxc - 1), 0)))
    else:
        in_specs.append(pl.BlockSpec((tm, d), lambda i, k: (i, 0)))
    in_specs.append(_mod_spec(modc, rows_per_batch, tm))
    in_specs.append(pl.BlockSpec((1, d), lambda i, k: (0, 0)))
    args = [lhs, w, x, modc, gpost]
    out_specs = [pl.BlockSpec((tm, d), lambda i, k: (i, 0))]
    out_shape = [jax.ShapeDtypeStruct((m, d), F32)]
    if emit_h:
        in_specs.append(_mod_spec(modn, rows_per_batch, tm))
        in_specs.append(pl.BlockSpec((1, d), lambda i, k: (0, 0)))
        args += [modn, gpre]
        out_specs.append(pl.BlockSpec((tm, d), lambda i, k: (i, 0)))
        out_shape.append(jax.ShapeDtypeStruct((m, d), BF16))
    outs = pl.pallas_call(
        functools.partial(_resid_kernel, nk=nk, nxc=nxc, weight=weight,
                          emit_h=emit_h, tm=tm, per_token=per_token),
        grid=(m // tm, nk),
        in_specs=in_specs,
        out_specs=out_specs,
        out_shape=out_shape,
        scratch_shapes=[pltpu.VMEM((tm, d), F32)],
        compiler_params=_cparams(2, 56),
        name="resid_proj",
    )(*args)
    return (outs[0], outs[1]) if emit_h else (outs[0], None)


def _inproj_kernel(h_ref, hs_ref, w_ref, o_ref, os_ref, w_s):
    @pl.when(pl.program_id(1) == 0)
    def _():
        w_s[...] = w_ref[...].astype(BF16)
        os_ref[...] = jnp.dot(hs_ref[...], w_s[...], preferred_element_type=F32)

    o_ref[...] = jnp.dot(h_ref[...], w_s[...], preferred_element_type=F32)


def _in_proj(h, hs, w_in, layer, *, tm):
    m, d = h.shape
    ms = hs.shape[0]
    n = w_in.shape[-1]
    tn = 768
    return pl.pallas_call(
        _inproj_kernel,
        grid=(n // tn, m // tm),
        in_specs=[
            pl.BlockSpec((tm, d), lambda j, i: (i, 0)),
            pl.BlockSpec((ms, d), lambda j, i: (0, 0)),
            pl.BlockSpec((None, d, tn), lambda j, i: (layer, 0, j)),
        ],
        out_specs=[pl.BlockSpec((tm, tn), lambda j, i: (i, j)),
                   pl.BlockSpec((ms, tn), lambda j, i: (0, j))],
        out_shape=[jax.ShapeDtypeStruct((m, n), F32), jax.ShapeDtypeStruct((ms, n), F32)],
        scratch_shapes=[pltpu.VMEM((d, tn), BF16)],
        compiler_params=_cparams(2, 48),
        name="in_proj",
    )(h, hs, w_in)


QBLK = 128
PREP_ROWS = 512
BLOCK_UNROLL_DILATED = 8
BLOCK_UNROLL_GQA = 1
NT_DIMS = (((1,), (1,)), ((), ()))


def _band_mask():
    row = lax.broadcasted_iota(jnp.int32, (QBLK, 2 * QBLK), 0)
    col = lax.broadcasted_iota(jnp.int32, (QBLK, 2 * QBLK), 1)
    return (col >= row) & (col <= row + HOPS), col >= QBLK


def _block_softmax(q_pieces, kb, vb, valid, fuse_den):
    s = lax.dot_general(jnp.concatenate(q_pieces, axis=0), kb, NT_DIMS,
                        preferred_element_type=F32)
    maxima, probs = [], []
    for i in range(len(q_pieces)):
        sp = jnp.where(valid, s[i * QBLK:(i + 1) * QBLK, :], NEG_INF)
        mb = jnp.max(sp, axis=1, keepdims=True)
        maxima.append(mb)
        probs.append(jnp.exp(sp - mb).astype(BF16))
    pb = jnp.concatenate(probs, axis=0)
    if fuse_den:
        both = jnp.dot(pb, jnp.concatenate([vb, jnp.ones_like(vb)], axis=1),
                       preferred_element_type=F32)
        return maxima, both[:, 0:LANES], both[:, LANES:2 * LANES]
    num = jnp.dot(pb, vb, preferred_element_type=F32)
    den = jnp.dot(pb, jnp.ones_like(vb), preferred_element_type=F32)
    return maxima, num, den


def _write_tails(kt_ref, vt_ref, k_tail, v_tail):
    for c in range(k_tail.shape[0] // QBLK):
        kt_ref[:, c * QBLK:(c + 1) * QBLK] = k_tail[c * QBLK:(c + 1) * QBLK, :].T
        vt_ref[:, c * QBLK:(c + 1) * QBLK] = v_tail[c * QBLK:(c + 1) * QBLK, :].T


def _dilated_kernel(q_ref, k_ref, v_ref, cos_ref, slo_ref, shi_ref, o_ref, kt_ref, vt_ref,
                    qs, kp, vp, *stat_scratch, dils, t, tail):
    pad = QBLK * max(dils)
    n_pat = len(dils)
    num_refs = stat_scratch[:n_pat]
    max_refs = stat_scratch[n_pat:2 * n_pat]
    den_refs = stat_scratch[2 * n_pat:]
    lane = lax.broadcasted_iota(jnp.int32, (1, LANES), 1)
    low = lane < HEAD_DIM

    kp[0:pad, :] = jnp.zeros((pad, LANES), F32)
    vp[0:pad, :] = jnp.zeros((pad, LANES), F32)

    def prep(c, carry):
        r0 = pl.multiple_of(c * PREP_ROWS, PREP_ROWS)
        rows = pl.ds(r0, PREP_ROWS)
        cos, slo, shi = cos_ref[rows, :], slo_ref[rows, :], shi_ref[rows, :]
        qs[rows, :] = _rope(q_ref[rows, :], cos, slo, shi) * ATTN_SCALE
        kp[pl.ds(pad + r0, PREP_ROWS), :] = _rope(k_ref[rows, :], cos, slo, shi)
        vp[pl.ds(pad + r0, PREP_ROWS), :] = v_ref[rows, :]
        return carry

    lax.fori_loop(0, t // PREP_ROWS, prep, 0)
    _write_tails(kt_ref, vt_ref, kp[pad + t - tail:pad + t, :], v_ref[t - tail:t, :])

    band, current = _band_mask()
    for pi, d in enumerate(dils):
        shift = d.bit_length() - 1

        def block(b, carry, d=d, shift=shift, pi=pi):
            n = lax.shift_right_logical(b, jnp.int32(shift))
            q0 = n * (QBLK * d) + (b & (d - 1))
            k0 = q0 + pad - QBLK * d
            if d == 1:
                qsl = pl.ds(pl.multiple_of(q0, QBLK), QBLK)
                ksl = pl.ds(pl.multiple_of(k0, QBLK), 2 * QBLK)
            else:
                qsl = pl.ds(q0, QBLK, stride=d)
                ksl = pl.ds(k0, 2 * QBLK, stride=d)
            q = qs[qsl, :]
            pieces = [jnp.where(low, q, 0.0).astype(BF16),
                      jnp.where(low, 0.0, q).astype(BF16)]
            (m0, m1), num, den = _block_softmax(
                pieces, kp[ksl, :].astype(BF16), vp[ksl, :].astype(BF16),
                band & (current | (n > 0)), fuse_den=True)
            num_refs[pi][qsl, :] = jnp.where(low, num[0:QBLK, :], num[QBLK:2 * QBLK, :])
            max_refs[pi][qsl, :] = jnp.where(low, m0, m1)
            den_refs[pi][qsl, :] = jnp.where(low, den[0:QBLK, :], den[QBLK:2 * QBLK, :])
            return carry

        lax.fori_loop(0, t // QBLK, block, 0, unroll=BLOCK_UNROLL_DILATED)

    def finish(c, carry):
        rows = pl.ds(pl.multiple_of(c * PREP_ROWS, PREP_ROWS), PREP_ROWS)
        ms = [r[rows, :] for r in max_refs]
        ls = [r[rows, :] for r in den_refs]
        m = functools.reduce(jnp.maximum, ms)
        den = num = None
        for mi, li, r in zip(ms, ls, num_refs):
            w = jnp.exp(mi - m)
            den = w * li if den is None else den + w * li
            num = w * r[rows, :] if num is None else num + w * r[rows, :]
        o_ref[rows, :] = (num / den).astype(BF16)
        return carry

    lax.fori_loop(0, t // PREP_ROWS, finish, 0)


def _attention_dilated(u, tables, *, n_batch, t, dils, q_slab, k_slab, v_slab, n_slab, tail):
    pad = QBLK * max(dils)
    once = pl.Buffered(1)
    table = pl.BlockSpec((t, LANES), lambda b, s: (0, 0), pipeline_mode=once)
    tail_spec = pl.BlockSpec((None, LANES, tail), lambda b, s: (b, s, 0))
    scratch = [pltpu.VMEM((t, LANES), F32)]
    scratch += [pltpu.VMEM((pad + t, LANES), F32) for _ in range(2)]
    scratch += [pltpu.VMEM((t, LANES), F32) for _ in range(3 * len(dils))]
    return pl.pallas_call(
        functools.partial(_dilated_kernel, dils=dils, t=t, tail=tail),
        grid=(n_batch, n_slab),
        in_specs=[
            pl.BlockSpec((t, LANES), lambda b, s: (b, q_slab + s)),
            pl.BlockSpec((t, LANES), lambda b, s: (b, k_slab + s)),
            pl.BlockSpec((t, LANES), lambda b, s: (b, v_slab + s)),
            table, table, table,
        ],
        out_specs=[pl.BlockSpec((t, LANES), lambda b, s: (b, s)), tail_spec, tail_spec],
        out_shape=[
            jax.ShapeDtypeStruct((n_batch * t, n_slab * LANES), BF16),
            jax.ShapeDtypeStruct((n_batch, n_slab * LANES, tail), F32),
            jax.ShapeDtypeStruct((n_batch, n_slab * LANES, tail), F32),
        ],
        scratch_shapes=scratch,
        compiler_params=_cparams(2, 56),
        name="attn_dilated",
    )(u, u, u, *tables)


GQA_SLABS = 4


def _gqa_kernel(q_ref, k_ref, v_ref, cos_ref, slo_ref, shi_ref, sink_ref, o_ref, kt_ref, vt_ref,
                kp, vp, *, t, tail):
    pad = QBLK
    kv_head = pl.program_id(1)
    lane = lax.broadcasted_iota(jnp.int32, (1, LANES), 1)
    low = lane < HEAD_DIM
    keep = lax.shift_right_logical(lane, jnp.int32(6)) == kv_head

    kp[0:pad, :] = jnp.zeros((pad, LANES), BF16)
    vp[0:pad, :] = jnp.zeros((pad, LANES), BF16)

    def prep(c, carry):
        r0 = pl.multiple_of(c * PREP_ROWS, PREP_ROWS)
        rows = pl.ds(r0, PREP_ROWS)
        kr = _rope(k_ref[rows, :], cos_ref[rows, :], slo_ref[rows, :], shi_ref[rows, :])
        v = v_ref[rows, :]
        prow = pl.ds(pad + r0, PREP_ROWS)
        kp[prow, :] = jnp.where(keep, kr, pltpu.roll(kr, HEAD_DIM, 1)).astype(BF16)
        vp[prow, :] = jnp.where(keep, v, pltpu.roll(v, HEAD_DIM, 1)).astype(BF16)
        return carry

    lax.fori_loop(0, t // PREP_ROWS, prep, 0)
    trows = slice(t - tail, t)
    _write_tails(kt_ref, vt_ref,
                 _rope(k_ref[trows, :], cos_ref[trows, :], slo_ref[trows, :], shi_ref[trows, :]),
                 v_ref[trows, :])

    band, current = _band_mask()

    def block(b, carry):
        q0 = pl.multiple_of(b * QBLK, QBLK)
        rows = pl.ds(q0, QBLK)
        cos, slo, shi = cos_ref[rows, :], slo_ref[rows, :], shi_ref[rows, :]
        pieces = []
        for s in range(GQA_SLABS):
            q = _rope(q_ref[rows, s * LANES:(s + 1) * LANES], cos, slo, shi) * ATTN_SCALE
            pieces += [jnp.where(low, q, 0.0).astype(BF16), jnp.where(low, 0.0, q).astype(BF16)]
        ksl = pl.ds(q0, 2 * QBLK)
        maxima, num, den = _block_softmax(pieces, kp[ksl, :], vp[ksl, :],
                                          band & (current | (b > 0)), fuse_den=False)
        for s in range(GQA_SLABS):
            outs = []
            for hh in range(2):
                i = 2 * s + hh
                piece = slice(i * QBLK, (i + 1) * QBLK)
                sink = sink_ref[:, s * LANES + hh * HEAD_DIM:s * LANES + hh * HEAD_DIM + 1]
                outs.append(num[piece, :] / (den[piece, :] + jnp.exp(sink - maxima[i])))
            o_ref[rows, s * LANES:(s + 1) * LANES] = jnp.where(low, outs[0], outs[1]).astype(BF16)
        return carry

    lax.fori_loop(0, t // QBLK, block, 0, unroll=BLOCK_UNROLL_GQA)


def _attention_gqa(u, tables, sinks, *, n_batch, t, q_slab, k_slab, v_slab, tail):
    width = GQA_SLABS * LANES
    once = pl.Buffered(1)
    table = pl.BlockSpec((t, LANES), lambda b, g: (0, 0), pipeline_mode=once)
    tail_spec = pl.BlockSpec((None, LANES, tail), lambda b, g: (b, 0, 0))
    q_blk = q_slab // GQA_SLABS
    return pl.pallas_call(
        functools.partial(_gqa_kernel, t=t, tail=tail),
        grid=(n_batch, N_KV_B),
        in_specs=[
            pl.BlockSpec((t, width), lambda b, g: (b, q_blk + g)),
            pl.BlockSpec((t, LANES), lambda b, g: (b, k_slab)),
            pl.BlockSpec((t, LANES), lambda b, g: (b, v_slab)),
            table, table, table,
            pl.BlockSpec((None, 1, width), lambda b, g: (g, 0, 0)),
        ],
        out_specs=[pl.BlockSpec((t, width), lambda b, g: (b, g)), tail_spec, tail_spec],
        out_shape=[
            jax.ShapeDtypeStruct((n_batch * t, N_KV_B * width), BF16),
            jax.ShapeDtypeStruct((n_batch, LANES, tail), F32),
            jax.ShapeDtypeStruct((n_batch, LANES, tail), F32),
        ],
        scratch_shapes=[pltpu.VMEM((QBLK + t, LANES), BF16), pltpu.VMEM((QBLK + t, LANES), BF16)],
        compiler_params=_cparams(2, 48),
        name="attn_gqa",
    )(u, u, u, *tables, sinks)


CONV_TB = 256
CONV_HALO = 32
CONV_RC = 32


def _layer_norm_swish(y, g, b):
    mu = jnp.mean(y, axis=-1, keepdims=True)
    yc = y - mu
    var = jnp.mean(yc * yc, axis=-1, keepdims=True)
    z = yc * lax.rsqrt(var + LN_EPS) * g + b
    return z * _sigmoid(z)


def _conv_kernel(v0, v1, g0, g1, cw_ref, cb_ref, lg_ref, lb_ref, o_ref, st_ref, ext, shifted):
    tt = pl.program_id(1)
    half = D_CONV // 2

    @pl.when(tt == 0)
    def _():
        ext[0:CONV_HALO, :] = jnp.zeros((CONV_HALO, D_CONV), F32)

    @pl.when(tt > 0)
    def _():
        ext[0:CONV_HALO, :] = ext[CONV_TB:CONV_TB + CONV_HALO, :]

    ext[CONV_HALO:CONV_HALO + CONV_TB, 0:half] = v0[...] * _sigmoid(g0[...])
    ext[CONV_HALO:CONV_HALO + CONV_TB, half:D_CONV] = v1[...] * _sigmoid(g1[...])
    st_ref[...] = ext[CONV_TB:CONV_TB + CONV_HALO, :]

    span = CONV_TB + CONV_HALO - SUBLANES
    for s in range(1, SUBLANES):
        shifted[s - 1, 0:span, :] = ext[s:s + span, :]

    lead = CONV_HALO - (CONV_WIDTH - 1)
    for c in range(CONV_TB // CONV_RC):
        r0 = c * CONV_RC
        y = jnp.zeros((CONV_RC, D_CONV), F32) + cb_ref[...]
        for w in range(CONV_WIDTH):
            s = (lead + w) % SUBLANES
            base = r0 + lead + w - s
            win = ext[base:base + CONV_RC, :] if s == 0 else shifted[s - 1, base:base + CONV_RC, :]
            y = y + win * cw_ref[w:w + 1, :]
        o_ref[r0:r0 + CONV_RC, :] = _layer_norm_swish(y, lg_ref[...], lb_ref[...]).astype(BF16)


def _conv_module(u, cw, cb, lg, lb, *, n_batch, t):
    nt = t // CONV_TB
    half = D_CONV // 2
    v_blk, g_blk = COL_VAL // half, COL_GATE // half
    row = lambda b, i: b * nt + i
    vec = pl.BlockSpec((1, D_CONV), lambda b, i: (0, 0))
    return pl.pallas_call(
        _conv_kernel,
        grid=(n_batch, nt),
        in_specs=[
            pl.BlockSpec((CONV_TB, half), lambda b, i: (row(b, i), v_blk)),
            pl.BlockSpec((CONV_TB, half), lambda b, i: (row(b, i), v_blk + 1)),
            pl.BlockSpec((CONV_TB, half), lambda b, i: (row(b, i), g_blk)),
            pl.BlockSpec((CONV_TB, half), lambda b, i: (row(b, i), g_blk + 1)),
            pl.BlockSpec((CONV_HALO, D_CONV), lambda b, i: (0, 0)),
            vec, vec, vec,
        ],
        out_specs=[
            pl.BlockSpec((CONV_TB, D_CONV), lambda b, i: (row(b, i), 0)),
            pl.BlockSpec((CONV_HALO, D_CONV), lambda b, i: (b, 0)),
        ],
        out_shape=[
            jax.ShapeDtypeStruct((n_batch * t, D_CONV), BF16),
            jax.ShapeDtypeStruct((n_batch * CONV_HALO, D_CONV), F32),
        ],
        scratch_shapes=[pltpu.VMEM((CONV_HALO + CONV_TB, D_CONV), F32),
                        pltpu.VMEM((SUBLANES - 1, CONV_HALO + CONV_TB, D_CONV), F32)],
        compiler_params=_cparams(2, 32),
        name="conv_module",
    )(u, u, u, u, cw, cb, lg, lb)


ROPE_SLABS = tuple(range(SLAB_QA, SLAB_VA)) + tuple(range(SLAB_QB, SLAB_VB))
Q_SLABS = tuple(range(SLAB_QA, SLAB_KA)) + tuple(range(SLAB_QB, SLAB_KB))
VAL_SLABS = tuple(range(COL_VAL // LANES, COL_GATE // LANES))


def _sample_post_kernel(u_ref, cos_ref, slo_ref, shi_ref, o_ref):
    for s in range(N_IN // LANES):
        x = u_ref[:, s * LANES:(s + 1) * LANES]
        if s in ROPE_SLABS:
            x = _rope(x, cos_ref[...], slo_ref[...], shi_ref[...])
            if s in Q_SLABS:
                x = x * ATTN_SCALE
        elif s in VAL_SLABS:
            gs = s + D_CONV // LANES
            x = x * _sigmoid(u_ref[:, gs * LANES:(gs + 1) * LANES])
        o_ref[:, s * LANES:(s + 1) * LANES] = x


def _sample_post(u, tables):
    m, n = u.shape
    full = pl.BlockSpec((m, n), lambda i: (0, 0))
    vec = pl.BlockSpec((1, LANES), lambda i: (0, 0))
    return pl.pallas_call(
        _sample_post_kernel,
        grid=(1,),
        in_specs=[full, vec, vec, vec],
        out_specs=full,
        out_shape=jax.ShapeDtypeStruct((m, n), F32),
        compiler_params=_cparams(1, 16),
        name="sample_post",
    )(u, *tables)


def _pattern_multiplicity(la):
    dist = la - jnp.arange(la, dtype=jnp.int32)
    mult = jnp.zeros((la,), F32)
    for d in DILATIONS_A:
        mult = mult + ((dist % d == 0) & (dist <= HOPS * d)).astype(F32)
    return mult.reshape(1, la)


def _sattn_shift_kernel(kt_ref, vt_ref, q_ref, knr_ref, vnr_ref, knc_ref, vnc_ref, mult_ref,
                        *rest):
    o_ref, ko_ref, vo_ref = rest[-3:]
    nh, e, la = kt_ref.shape
    w = nh * e
    k = kt_ref[...].reshape(w, la)
    v = vt_ref[...].reshape(w, la)
    q = q_ref[...]
    mult = mult_ref[...]
    s = jnp.dot(q.astype(BF16), k.astype(BF16), preferred_element_type=F32)
    s = jnp.where(mult > 0.0, s, NEG_INF)
    s_new = jnp.sum(q * knr_ref[...], axis=1, keepdims=True)
    m = jnp.maximum(jnp.max(s, axis=1, keepdims=True), s_new)
    p = jnp.exp(s - m) * mult
    p_new = jnp.exp(s_new - m) * float(len(DILATIONS_A))
    den = jnp.sum(p, axis=1, keepdims=True) + p_new
    num = lax.dot_general(p.astype(BF16), v.astype(BF16), NT_DIMS, preferred_element_type=F32)
    o_ref[...] = (num + p_new * vnr_ref[...]) / den
    newest = lax.broadcasted_iota(jnp.int32, (1, la), 1) == la - 1
    ko_ref[...] = jnp.where(newest, knc_ref[...], pltpu.roll(k, la - 1, 1)).reshape(nh, e, la)
    vo_ref[...] = jnp.where(newest, vnc_ref[...], pltpu.roll(v, la - 1, 1)).reshape(nh, e, la)


def _sample_attn_shift(kt, vt, prev, layer, q_bd, kn, vn, mult):
    depth, nb, nh, e, la = kt.shape
    w = nh * e
    slab = pl.BlockSpec((None, None, nh, e, la), lambda b: (layer, b, 0, 0, 0))
    tokq = pl.BlockSpec((None, nh, w), lambda b: (b, 0, 0))
    row = pl.BlockSpec((None, 1, w), lambda b: (b, 0, 0))
    col = pl.BlockSpec((None, w, 1), lambda b: (b, 0, 0))
    in_specs = [slab, slab, tokq, row, row, col, col, pl.BlockSpec((1, la), lambda b: (0, 0))]
    args = [kt, vt, q_bd, kn.reshape(nb, 1, w), vn.reshape(nb, 1, w),
            kn.reshape(nb, w, 1), vn.reshape(nb, w, 1), mult]
    aliases = {}
    if prev is not None:
        aliases = {len(args): 1, len(args) + 1: 2}
        in_specs += [pl.BlockSpec(memory_space=pl.ANY)] * 2
        args += list(prev)
    return pl.pallas_call(
        _sattn_shift_kernel,
        grid=(nb,),
        in_specs=in_specs,
        out_specs=[tokq, slab, slab],
        out_shape=[jax.ShapeDtypeStruct((nb, nh, w), F32),
                   jax.ShapeDtypeStruct(kt.shape, F32), jax.ShapeDtypeStruct(vt.shape, F32)],
        input_output_aliases=aliases,
        compiler_params=_cparams(1, 56),
        name="sample_attn_shift",
    )(*args)


def _sattn_b_kernel(kc_ref, vc_ref, q_ref, kn_ref, vn_ref, sink_ref, o_ref):
    q = q_ref[...]
    s = lax.dot_general(q.astype(BF16), kc_ref[...].astype(BF16), NT_DIMS,
                        preferred_element_type=F32)
    s_new = jnp.sum(q * kn_ref[...], axis=1, keepdims=True)
    m = jnp.maximum(jnp.max(s, axis=1, keepdims=True), s_new)
    p = jnp.exp(s - m)
    p_new = jnp.exp(s_new - m)
    den = jnp.sum(p, axis=1, keepdims=True) + p_new + jnp.exp(sink_ref[...] - m)
    num = jnp.dot(p.astype(BF16), vc_ref[...].astype(BF16), preferred_element_type=F32)
    o_ref[...] = (num + p_new * vn_ref[...]) / den


def _sample_attn_b(kc, vc, layer, q_exp, kn, vn, sinks):
    _, nb, lb, w = kc.shape
    nh = q_exp.shape[1]
    cache = pl.BlockSpec((None, None, lb, w), lambda b: (layer, b, 0, 0))
    tokq = pl.BlockSpec((None, nh, w), lambda b: (b, 0, 0))
    tok1 = pl.BlockSpec((None, 1, w), lambda b: (b, 0, 0))
    return pl.pallas_call(
        _sattn_b_kernel,
        grid=(nb,),
        in_specs=[cache, cache, tokq, tok1, tok1, pl.BlockSpec((nh, 1), lambda b: (0, 0))],
        out_specs=tokq,
        out_shape=jax.ShapeDtypeStruct((nb, nh, w), F32),
        compiler_params=_cparams(1, 16),
        name="sample_attn_shared",
    )(kc, vc, q_exp, kn, vn, sinks)


def _sconv_kernel(st_ref, glu_ref, cw_ref, cb_ref, lg_ref, lb_ref, o_ref, ns_ref):
    hist = CONV_WIDTH - 1
    glu = glu_ref[...]
    y = glu * cw_ref[hist:hist + 1, :] + cb_ref[...]
    for w in range(hist):
        y = y + st_ref[:, w * D_CONV:(w + 1) * D_CONV] * cw_ref[w:w + 1, :]
    o_ref[...] = _layer_norm_swish(y, lg_ref[...], lb_ref[...]).astype(BF16)
    ns_ref[:, 0:(hist - 1) * D_CONV] = st_ref[:, D_CONV:hist * D_CONV]
    ns_ref[:, (hist - 1) * D_CONV:hist * D_CONV] = glu


def _sample_conv(state2d, layer, glu, cw, cb, lg, lb):
    _, nb, width = state2d.shape
    st = pl.BlockSpec((None, nb, width), lambda i: (layer, 0, 0))
    tok = pl.BlockSpec((nb, D_CONV), lambda i: (0, 0))
    vec = pl.BlockSpec((1, D_CONV), lambda i: (0, 0))
    return pl.pallas_call(
        _sconv_kernel,
        grid=(1,),
        in_specs=[st, tok, pl.BlockSpec((CONV_HALO, D_CONV), lambda i: (0, 0)), vec, vec, vec],
        out_specs=[tok, pl.BlockSpec((nb, width), lambda i: (0, 0))],
        out_shape=[jax.ShapeDtypeStruct((nb, D_CONV), BF16),
                   jax.ShapeDtypeStruct((nb, width), F32)],
        compiler_params=_cparams(1, 16),
        name="sample_conv",
    )(state2d, glu, cw, cb, lg, lb)


def _shift_kernel(km, kx, kn, vm, vx, vn, ko, vo):
    last = pl.program_id(2) == pl.num_programs(2) - 1
    for main, nxt, new, out in ((km, kx, kn, ko), (vm, vx, vn, vo)):
        rb = main.shape[1]
        out[:, 0:rb - 1] = main[:, 1:rb]
        out[:, rb - 1:rb] = jnp.where(last, new[...], nxt[...])


def _shift_caches(cache_k, cache_v, new_k, new_v, *, batch_chunk, row_block):
    depth, nb, rows, nh, e = cache_k.shape
    main = pl.BlockSpec((None, batch_chunk, row_block, nh, e), lambda l, b, r: (l, b, r, 0, 0))
    nxt = pl.BlockSpec((None, batch_chunk, 1, nh, e),
                       lambda l, b, r: (l, b, jnp.minimum((r + 1) * row_block, rows - 1), 0, 0))
    new = pl.BlockSpec((None, batch_chunk, 1, nh, e), lambda l, b, r: (l, b, 0, 0, 0))
    shape = jax.ShapeDtypeStruct(cache_k.shape, cache_k.dtype)
    return pl.pallas_call(
        _shift_kernel,
        grid=(depth, nb // batch_chunk, rows // row_block),
        in_specs=[main, nxt, new, main, nxt, new],
        out_specs=[main, main],
        out_shape=[shape, shape],
        compiler_params=_cparams(3, 48),
        name="cache_shift",
    )(cache_k, cache_k, new_k, cache_v, cache_v, new_v)


def _rope_tables(pos):
    n = pos.shape[0]
    inv_freq = 1.0 / (ROPE_THETA ** (jnp.arange(ROT_HALF, dtype=F32) / ROT_HALF))
    ang = pos.astype(F32)[:, None] * inv_freq[None, :]
    cos, sin = jnp.cos(ang), jnp.sin(ang)
    rest = HEAD_DIM - 2 * ROT_HALF
    c = jnp.concatenate([cos, cos, jnp.ones((n, rest), F32)], axis=1)
    lo = jnp.concatenate([-sin, jnp.zeros((n, HEAD_DIM - ROT_HALF), F32)], axis=1)
    hi = jnp.concatenate([jnp.zeros((n, ROT_HALF), F32), sin, jnp.zeros((n, rest), F32)], axis=1)
    return tuple(jnp.tile(x, (1, LANES // HEAD_DIM)) for x in (c, lo, hi))


def kernel(x_prompt, x_sample, c_prompt, c_sample, cache_a_k, cache_a_v, cache_b_k, cache_b_v,
           state_c_conv, w_ada, b_ada, g_pre, g_post, w_ffn_gu, w_ffn_down, w_in, w_out,
           attn_sinks, conv_w, conv_b, conv_ln_g, conv_ln_b):
    nbp, t, d = x_prompt.shape
    nbs = x_sample.shape[0]
    mp = nbp * t
    depth = w_ada.shape[0]
    la, lb = cache_a_k.shape[2], cache_b_k.shape[2]
    hist = CONV_WIDTH - 1
    tm_p = 1024

    xp = x_prompt.reshape(mp, d)
    xs = x_sample.reshape(nbs, d)

    s_row = 32
    c_all = jnp.zeros((s_row + nbs, d), F32).at[:nbp].set(c_prompt).at[s_row:].set(c_sample)
    mod_all = _ada_mod(c_all, w_ada, b_ada)

    def mod_p(l, s):
        return mod_all[l, :nbp, 3 * s * d:3 * (s + 1) * d].reshape(nbp, 3, 1, d)

    def mod_s(l, s):
        m = mod_all[l, s_row:, 3 * s * d:3 * (s + 1) * d].reshape(nbs, 3, d)
        return jnp.transpose(m, (1, 0, 2))[None]

    def nxt(l, s):
        return (l, s + 1) if s + 1 < N_SUB else (l + 1, 0)

    tab_p = _rope_tables(jnp.arange(t, dtype=jnp.int32))
    tab_s = _rope_tables(PAST_LEN + jnp.arange(1, dtype=jnp.int32))
    sink_groups = jnp.repeat(attn_sinks, HEAD_DIM, axis=1).reshape(depth, N_KV_B, 1, -1)
    sink_cols = attn_sinks.reshape(depth, N_HEADS_B, 1)
    cw_pad = jnp.pad(conv_w, ((0, 0), (0, CONV_HALO - CONV_WIDTH), (0, 0)))
    vec = lambda a, l: a[l].reshape(1, -1)

    cache_bk2 = cache_b_k.reshape(depth, nbs, lb, N_KV_B * HEAD_DIM)
    cache_bv2 = cache_b_v.reshape(depth, nbs, lb, N_KV_B * HEAD_DIM)
    state2d = state_c_conv.reshape(depth, nbs, hist * D_CONV)
    kv_of_head = (jnp.arange(N_HEADS_B) // (N_HEADS_B // N_KV_B))[:, None] == jnp.arange(N_KV_B)
    cache_akt = jnp.transpose(cache_a_k, (0, 1, 3, 4, 2))
    cache_avt = jnp.transpose(cache_a_v, (0, 1, 3, 4, 2))
    mult_a = _pattern_multiplicity(la)
    head_eye = jnp.eye(N_HEADS_A, dtype=bool)[None, :, :, None]

    hp = _prenorm(xp, mod_p(0, 0), vec(g_pre[0], 0), tm=tm_p, rows_per_batch=t)
    hs = _prenorm(xs, mod_s(0, 0), vec(g_pre[0], 0), tm=nbs, rows_per_batch=nbs)

    st_p = [[] for _ in range(5)]
    new_rows = [[], []]
    shift_a = None
    st_c_s = []

    f_pad = pl.cdiv(w_ffn_down.shape[2], TF) * TF
    wd_bf = _cast_pad_rows(w_ffn_down.reshape(depth * 2, -1, d), f_pad)
    wo_bf = _cast_pad_rows(w_out, w_out.shape[1])

    def resid(lhs, s, x, l, weight, mod_fn, tm, rpb):
        ln, sn = nxt(l, s)
        last = ln >= depth
        if s == 1:
            lhs, w, mat, tk = jnp.concatenate(lhs, axis=1), wo_bf, l, TF
        else:
            w, mat, tk = wd_bf, 2 * l + s // 2, f_pad // 4
        return _resid_proj(
            lhs, w, mat, x, mod_fn(l, s), vec(g_post[l], s),
            None if last else mod_fn(ln, sn), None if last else vec(g_pre[ln], sn),
            tm=tm, tk=tk, rows_per_batch=rpb, weight=weight)

    for l in range(depth):
        a, a_s = _ffn_up(hp, hs, w_ffn_gu, (l, 0), tm=tm_p)
        xp, hp = resid(a, 0, xp, l, 0.5, mod_p, tm_p, t)
        xs, hs = resid(a_s, 0, xs, l, 0.5, mod_s, nbs, nbs)

        u, u_s = _in_proj(hp, hs, w_in, l, tm=tm_p)
        oa, kta, vta = _attention_dilated(
            u, tab_p, n_batch=nbp, t=t, dils=DILATIONS_A, q_slab=SLAB_QA, k_slab=SLAB_KA,
            v_slab=SLAB_VA, n_slab=N_HEADS_A // 2, tail=min(la, t))
        ob, ktb, vtb = _attention_gqa(
            u, tab_p, sink_groups[l], n_batch=nbp, t=t, q_slab=SLAB_QB, k_slab=SLAB_KB,
            v_slab=SLAB_VB, tail=min(lb, t))
        oc, cst = _conv_module(u, cw_pad[l], vec(conv_b, l), vec(conv_ln_g, l), vec(conv_ln_b, l),
                               n_batch=nbp, t=t)
        xp, hp = resid([oa, ob, oc], 1, xp, l, 1.0, mod_p, tm_p, t)

        for dst, tail_t, heads in ((st_p[0], kta, N_HEADS_A), (st_p[1], vta, N_HEADS_A),
                                   (st_p[2], ktb, N_KV_B), (st_p[3], vtb, N_KV_B)):
            dst.append(jnp.transpose(tail_t.reshape(nbp, heads, HEAD_DIM, -1), (0, 3, 1, 2)))
        st_p[4].append(cst.reshape(nbp, CONV_HALO, D_CONV)[:, CONV_HALO - hist:])

        ur = _sample_post(u_s, tab_s)
        seg = lambda lo, hi: ur[:, lo * LANES:hi * LANES]
        qa = seg(SLAB_QA, SLAB_KA).reshape(nbs, 1, N_HEADS_A, HEAD_DIM)
        q_bd = jnp.where(head_eye, qa, 0.0).reshape(nbs, N_HEADS_A, N_HEADS_A * HEAD_DIM)
        oa3, *shift_a = _sample_attn_shift(cache_akt, cache_avt, shift_a, l, q_bd,
                                           seg(SLAB_KA, SLAB_VA), seg(SLAB_VA, SLAB_QB), mult_a)
        oa = jnp.sum(jnp.where(head_eye, oa3.reshape(nbs, N_HEADS_A, N_HEADS_A, HEAD_DIM), 0.0),
                     axis=1)
        qb = seg(SLAB_QB, SLAB_KB).reshape(nbs, N_HEADS_B, 1, HEAD_DIM)
        q_exp = jnp.where(kv_of_head[None, :, :, None], qb, 0.0).reshape(nbs, N_HEADS_B, LANES)
        knb = seg(SLAB_KB, SLAB_VB)
        vnb = seg(SLAB_VB, SLAB_VB + 1)
        ob2 = _sample_attn_b(cache_bk2, cache_bv2, l, q_exp, knb.reshape(nbs, 1, LANES),
                             vnb.reshape(nbs, 1, LANES), sink_cols[l])
        ob = jnp.sum(jnp.where(kv_of_head[None, :, :, None],
                               ob2.reshape(nbs, N_HEADS_B, N_KV_B, HEAD_DIM), 0.0), axis=2)
        glu = ur[:, COL_VAL:COL_GATE]
        oc, ns = _sample_conv(state2d, l, glu, cw_pad[l], vec(conv_b, l), vec(conv_ln_g, l),
                              vec(conv_ln_b, l))
        mix = [oa.reshape(nbs, -1).astype(BF16), ob.reshape(nbs, -1).astype(BF16), oc]
        xs, hs = resid(mix, 1, xs, l, 1.0, mod_s, nbs, nbs)

        a, a_s = _ffn_up(hp, hs, w_ffn_gu, (l, 1), tm=tm_p)
        xp, hp = resid(a, 2, xp, l, 0.5, mod_p, tm_p, t)
        xs, hs = resid(a_s, 2, xs, l, 0.5, mod_s, nbs, nbs)

        new_rows[0].append(knb.reshape(nbs, 1, N_KV_B, HEAD_DIM))
        new_rows[1].append(vnb.reshape(nbs, 1, N_KV_B, HEAD_DIM))
        st_c_s.append(ns.reshape(nbs, hist, D_CONV))

    new_rows = [jnp.stack(r) for r in new_rows]
    shift_a = [jnp.transpose(c, (0, 1, 4, 2, 3)) for c in shift_a]
    shift_b = _shift_caches(cache_b_k, cache_b_v, new_rows[0], new_rows[1],
                            batch_chunk=8, row_block=lb)
    return (xp.reshape(nbp, t, d), xs.reshape(nbs, 1, d),
            *[jnp.stack(s) for s in st_p], *shift_a, *shift_b, jnp.stack(st_c_s))
```

```python
import functools

import jax
import jax.numpy as jnp
from jax import lax
from jax.experimental import pallas as pl
from jax.experimental.pallas import tpu as pltpu

F32 = jnp.float32
BF16 = jnp.bfloat16

D_MODEL = 2048
DEPTH = 4
HEAD_DIM = 64
N_HEADS_A = 8
N_HEADS_B = 16
N_KV_B = 2
D_CONV = 512
DILATIONS_A = (1, 4, 16)
HOPS = 128
CONV_WIDTH = 31
ROT_HALF = 8
ROPE_THETA = 500000.0
PAST_LEN = 16384
RMS_EPS = 1e-6
LN_EPS = 1e-5
NEG_INF = -1e30
ATTN_SCALE = HEAD_DIM ** -0.5
N_SUB = 3

LANES = 128
SUBLANES = 8
MIB = 1 << 20

SLAB_QA, SLAB_KA, SLAB_VA = 0, 4, 8
SLAB_QB, SLAB_KB, SLAB_VB = 12, 20, 21
COL_VAL, COL_GATE = 2816, 3328
N_IN = 3840


def _cparams(n_axes, vmem_mib):
    return pltpu.CompilerParams(
        dimension_semantics=("arbitrary",) * n_axes,
        vmem_limit_bytes=vmem_mib * MIB,
    )


def _sigmoid(x):
    return jax.nn.sigmoid(x)


def _pre_norm(x, g, shift, scale):
    ms = jnp.mean(x * x, axis=-1, keepdims=True)
    return (x * lax.rsqrt(ms + RMS_EPS)) * (g * (1.0 + scale)) + shift


def _gated_post(y, g, gate, weight):
    ms = jnp.mean(y * y, axis=-1, keepdims=True)
    return (y * lax.rsqrt(ms + RMS_EPS)) * ((weight * gate) * g)


def _rope(x, cos, sin_lo, sin_hi):
    return x * cos + pltpu.roll(x, LANES - ROT_HALF, 1) * sin_lo + pltpu.roll(x, ROT_HALF, 1) * sin_hi


def _ada_kernel(c_ref, w_ref, b_ref, o_ref):
    c = c_ref[...]
    a = (c * _sigmoid(c)).astype(BF16)
    o_ref[...] = jnp.dot(a, w_ref[...].astype(BF16), preferred_element_type=F32) + b_ref[...]


def _ada_mod(c_all, w_ada, b_ada):
    depth, d, n = w_ada.shape
    rows = c_all.shape[0]
    tn = 1024
    return pl.pallas_call(
        _ada_kernel,
        grid=(depth, n // tn),
        in_specs=[
            pl.BlockSpec((rows, d), lambda l, j: (0, 0)),
            pl.BlockSpec((None, d, tn), lambda l, j: (l, 0, j)),
            pl.BlockSpec((None, 1, tn), lambda l, j: (l, 0, j)),
        ],
        out_specs=pl.BlockSpec((None, rows, tn), lambda l, j: (l, 0, j)),
        out_shape=jax.ShapeDtypeStruct((depth, rows, n), F32),
        compiler_params=_cparams(2, 40),
        name="ada_mod",
    )(c_all, w_ada, b_ada.reshape(depth, 1, n))


def _prenorm_kernel(x_ref, mod_ref, g_ref, h_ref):
    h_ref[...] = _pre_norm(x_ref[...], g_ref[...], mod_ref[0], mod_ref[1]).astype(BF16)


def _mod_spec(mod, rows_per_batch, tm):
    _, _, r, d = mod.shape
    if r == 1:
        per = rows_per_batch // tm
        return pl.BlockSpec((None, 3, 1, d), lambda i, *_: (i // per, 0, 0, 0))
    return pl.BlockSpec((None, 3, r, d), lambda i, *_: (0, 0, 0, 0))


def _prenorm(x, mod, g, *, tm, rows_per_batch):
    m, d = x.shape
    return pl.pallas_call(
        _prenorm_kernel,
        grid=(m // tm,),
        in_specs=[
            pl.BlockSpec((tm, d), lambda i: (i, 0)),
            _mod_spec(mod, rows_per_batch, tm),
            pl.BlockSpec((1, d), lambda i: (0, 0)),
        ],
        out_specs=pl.BlockSpec((tm, d), lambda i: (i, 0)),
        out_shape=jax.ShapeDtypeStruct((m, d), BF16),
        compiler_params=_cparams(1, 32),
        name="prenorm",
    )(x, mod, g)


W_BLK = 128
W_STREAMS = 4
TF = W_BLK * W_STREAMS


def _ffn_up_kernel(h_ref, hs_ref, *refs, f_valid):
    g_refs = refs[0:W_STREAMS]
    u_refs = refs[W_STREAMS:2 * W_STREAMS]
    a_ref, as_ref, wg_s, wu_s = refs[2 * W_STREAMS:]
    j = pl.program_id(0)

    def gated(h):
        g = jnp.dot(h, wg_s[...], preferred_element_type=F32)
        u = jnp.dot(h, wu_s[...], preferred_element_type=F32)
        a = (g * _sigmoid(g)) * u
        col = lax.broadcasted_iota(jnp.int32, a.shape, 1)
        return jnp.where(col < f_valid - j * TF, a, 0.0).astype(BF16)

    @pl.when(pl.program_id(1) == 0)
    def _():
        for q in range(W_STREAMS):
            wg_s[:, q * W_BLK:(q + 1) * W_BLK] = g_refs[q][...].astype(BF16)
            wu_s[:, q * W_BLK:(q + 1) * W_BLK] = u_refs[q][...].astype(BF16)
        as_ref[...] = gated(hs_ref[...])

    a_ref[...] = gated(h_ref[...])


def _ffn_up(h, hs, w_gu, lead, *, tm):
    m, d = h.shape
    ms = hs.shape[0]
    f = w_gu.shape[-1] // 2
    squeezed = (None,) * len(lead)
    nb = f // W_BLK
    nj = pl.cdiv(f, TF)

    def wspec(base, q):
        return pl.BlockSpec(
            squeezed + (d, W_BLK),
            lambda j, i: lead + (0, base + jnp.minimum(W_STREAMS * j + q, nb - 1)))

    in_specs = [pl.BlockSpec((tm, d), lambda j, i: (i, 0)),
                pl.BlockSpec((ms, d), lambda j, i: (0, 0))]
    in_specs += [wspec(0, q) for q in range(W_STREAMS)]
    in_specs += [wspec(nb, q) for q in range(W_STREAMS)]
    return pl.pallas_call(
        functools.partial(_ffn_up_kernel, f_valid=f),
        grid=(nj, m // tm),
        in_specs=in_specs,
        out_specs=[pl.BlockSpec((tm, TF), lambda j, i: (i, j)),
                   pl.BlockSpec((ms, TF), lambda j, i: (0, j))],
        out_shape=[jax.ShapeDtypeStruct((m, nj * TF), BF16),
                   jax.ShapeDtypeStruct((ms, nj * TF), BF16)],
        scratch_shapes=[pltpu.VMEM((d, TF), BF16), pltpu.VMEM((d, TF), BF16)],
        compiler_params=_cparams(2, 48),
        name="ffn_up",
    )(h, hs, *([w_gu] * (2 * W_STREAMS)))


EPI_ROWS = 256


def _cast_kernel(w_ref, o_ref, *, k_valid):
    rows = w_ref.shape[1]
    row = lax.broadcasted_iota(jnp.int32, w_ref.shape, 1)
    o_ref[...] = jnp.where(row < k_valid - pl.program_id(0) * rows, w_ref[...], 0.0).astype(BF16)


def _cast_pad_rows(w, k_pad):
    n, k, d = w.shape
    last = k // W_BLK - 1
    return pl.pallas_call(
        functools.partial(_cast_kernel, k_valid=k),
        grid=(k_pad // W_BLK,),
        in_specs=[pl.BlockSpec((n, W_BLK, d), lambda j: (0, jnp.minimum(j, last), 0))],
        out_specs=pl.BlockSpec((n, W_BLK, d), lambda j: (0, j, 0)),
        out_shape=jax.ShapeDtypeStruct((n, k_pad, d), BF16),
        compiler_params=_cparams(1, 40),
        name="cast_weights",
    )(w)


def _resid_kernel(*refs, nk, nxc, weight, emit_h, tm, per_token):
    it = iter(refs)
    lhs_ref, w_ref, xin_ref, modc_ref, gpost_ref = [next(it) for _ in range(5)]
    modn_ref = gpre_ref = None
    if emit_h:
        modn_ref, gpre_ref = next(it), next(it)
    x_out = next(it)
    h_out = next(it) if emit_h else None
    acc = next(it)
    k = pl.program_id(1)

    if nxc > 1:
        xr = tm // nxc

        @pl.when(k < nxc)
        def _():
            x_out[pl.ds(pl.multiple_of(k * xr, xr), xr), :] = xin_ref[...]

    @pl.when(k == 0)
    def _():
        acc[...] = jnp.dot(lhs_ref[...], w_ref[...], preferred_element_type=F32)

    @pl.when(k > 0)
    def _():
        acc[...] += jnp.dot(lhs_ref[...], w_ref[...], preferred_element_type=F32)

    def epilogue(rows):
        y = acc[rows, :]
        x = xin_ref[rows, :] if nxc == 1 else x_out[rows, :]
        if per_token:
            gate = modc_ref[2, rows, :]
        else:
            gate = modc_ref[2]
        xn = x + _gated_post(y, gpost_ref[...], gate, weight)
        x_out[rows, :] = xn
        if emit_h:
            if per_token:
                shift, scale = modn_ref[0, rows, :], modn_ref[1, rows, :]
            else:
                shift, scale = modn_ref[0], modn_ref[1]
            h_out[rows, :] = _pre_norm(xn, gpre_ref[...], shift, scale).astype(BF16)

    @pl.when(k == nk - 1)
    def _():
        if tm <= EPI_ROWS:
            epilogue(slice(None))
        else:
            def body(c, carry):
                epilogue(pl.ds(pl.multiple_of(c * EPI_ROWS, EPI_ROWS), EPI_ROWS))
                return carry
            lax.fori_loop(0, tm // EPI_ROWS, body, 0)


def _resid_proj(lhs, w, mat, x, modc, gpost, modn, gpre, *, tm, tk, rows_per_batch, weight):
    m, d = x.shape
    emit_h = modn is not None
    nk = lhs.shape[1] // tk
    per_token = modc.shape[2] != 1
    nxc = min(nk, 4) if tm >= 512 else 1

    in_specs = [
        pl.BlockSpec((tm, tk), lambda i, k: (i, k)),
        pl.BlockSpec((None, tk, d), lambda i, k: (mat, k, 0)),
    ]
    if nxc > 1:
        in_specs.append(pl.BlockSpec(
            (tm // nxc, d), lambda i, k: (i * nxc + jnp.minimum(k, nxc - 1), 0)))
    else:
        in_specs.append(pl.BlockSpec((tm, d), lambda i, k: (i, 0)))
    in_specs.append(_mod_spec(modc, rows_per_batch, tm))
    in_specs.append(pl.BlockSpec((1, d), lambda i, k: (0, 0)))
    args = [lhs, w, x, modc, gpost]
    out_specs = [pl.BlockSpec((tm, d), lambda i, k: (i, 0))]
    out_shape = [jax.ShapeDtypeStruct((m, d), F32)]
    if emit_h:
        in_specs.append(_mod_spec(modn, rows_per_batch, tm))
        in_specs.append(pl.BlockSpec((1, d), lambda i, k: (0, 0)))
        args += [modn, gpre]
        out_specs.append(pl.BlockSpec((tm, d), lambda i, k: (i, 0)))
        out_shape.append(jax.ShapeDtypeStruct((m, d), BF16))
    outs = pl.pallas_call(
        functools.partial(_resid_kernel, nk=nk, nxc=nxc, weight=weight,
                          emit_h=emit_h, tm=tm, per_token=per_token),
        grid=(m // tm, nk),
        in_specs=in_specs,
        out_specs=out_specs,
        out_shape=out_shape,
        scratch_shapes=[pltpu.VMEM((tm, d), F32)],
        compiler_params=_cparams(2, 56),
        name="resid_proj",
    )(*args)
    return (outs[0], outs[1]) if emit_h else (outs[0], None)


def _inproj_kernel(h_ref, hs_ref, w_ref, o_ref, os_ref, w_s):
    @pl.when(pl.program_id(1) == 0)
    def _():
        w_s[...] = w_ref[...].astype(BF16)
        os_ref[...] = jnp.dot(hs_ref[...], w_s[...], preferred_element_type=F32)

    o_ref[...] = jnp.dot(h_ref[...], w_s[...], preferred_element_type=F32)


def _in_proj(h, hs, w_in, layer, *, tm):
    m, d = h.shape
    ms = hs.shape[0]
    n = w_in.shape[-1]
    tn = 768
    return pl.pallas_call(
        _inproj_kernel,
        grid=(n // tn, m // tm),
        in_specs=[
            pl.BlockSpec((tm, d), lambda j, i: (i, 0)),
            pl.BlockSpec((ms, d), lambda j, i: (0, 0)),
            pl.BlockSpec((None, d, tn), lambda j, i: (layer, 0, j)),
        ],
        out_specs=[pl.BlockSpec((tm, tn), lambda j, i: (i, j)),
                   pl.BlockSpec((ms, tn), lambda j, i: (0, j))],
        out_shape=[jax.ShapeDtypeStruct((m, n), F32), jax.ShapeDtypeStruct((ms, n), F32)],
        scratch_shapes=[pltpu.VMEM((d, tn), BF16)],
        compiler_params=_cparams(2, 48),
        name="in_proj",
    )(h, hs, w_in)


QBLK = 128
PREP_ROWS = 512
BLOCK_UNROLL_DILATED = 8
BLOCK_UNROLL_GQA = 1
NT_DIMS = (((1,), (1,)), ((), ()))


def _band_mask():
    row = lax.broadcasted_iota(jnp.int32, (QBLK, 2 * QBLK), 0)
    col = lax.broadcasted_iota(jnp.int32, (QBLK, 2 * QBLK), 1)
    return (col >= row) & (col <= row + HOPS), col >= QBLK


def _block_softmax(q_pieces, kb, vb, valid, fuse_den):
    s = lax.dot_general(jnp.concatenate(q_pieces, axis=0), kb, NT_DIMS,
                        preferred_element_type=F32)
    maxima, probs = [], []
    for i in range(len(q_pieces)):
        sp = jnp.where(valid, s[i * QBLK:(i + 1) * QBLK, :], NEG_INF)
        mb = jnp.max(sp, axis=1, keepdims=True)
        maxima.append(mb)
        probs.append(jnp.exp(sp - mb).astype(BF16))
    pb = jnp.concatenate(probs, axis=0)
    if fuse_den:
        both = jnp.dot(pb, jnp.concatenate([vb, jnp.ones_like(vb)], axis=1),
                       preferred_element_type=F32)
        return maxima, both[:, 0:LANES], both[:, LANES:2 * LANES]
    num = jnp.dot(pb, vb, preferred_element_type=F32)
    den = jnp.dot(pb, jnp.ones_like(vb), preferred_element_type=F32)
    return maxima, num, den


def _write_tails(kt_ref, vt_ref, k_tail, v_tail):
    for c in range(k_tail.shape[0] // QBLK):
        kt_ref[:, c * QBLK:(c + 1) * QBLK] = k_tail[c * QBLK:(c + 1) * QBLK, :].T
        vt_ref[:, c * QBLK:(c + 1) * QBLK] = v_tail[c * QBLK:(c + 1) * QBLK, :].T


def _dilated_kernel(q_ref, k_ref, v_ref, cos_ref, slo_ref, shi_ref, o_ref, kt_ref, vt_ref,
                    qs, kp, vp, *stat_scratch, dils, t, tail):
    pad = QBLK * max(dils)
    n_pat = len(dils)
    num_refs = stat_scratch[:n_pat]
    max_refs = stat_scratch[n_pat:2 * n_pat]
    den_refs = stat_scratch[2 * n_pat:]
    lane = lax.broadcasted_iota(jnp.int32, (1, LANES), 1)
    low = lane < HEAD_DIM

    kp[0:pad, :] = jnp.zeros((pad, LANES), F32)
    vp[0:pad, :] = jnp.zeros((pad, LANES), F32)

    def prep(c, carry):
        r0 = pl.multiple_of(c * PREP_ROWS, PREP_ROWS)
        rows = pl.ds(r0, PREP_ROWS)
        cos, slo, shi = cos_ref[rows, :], slo_ref[rows, :], shi_ref[rows, :]
        qs[rows, :] = _rope(q_ref[rows, :], cos, slo, shi) * ATTN_SCALE
        kp[pl.ds(pad + r0, PREP_ROWS), :] = _rope(k_ref[rows, :], cos, slo, shi)
        vp[pl.ds(pad + r0, PREP_ROWS), :] = v_ref[rows, :]
        return carry

    lax.fori_loop(0, t // PREP_ROWS, prep, 0)
    _write_tails(kt_ref, vt_ref, kp[pad + t - tail:pad + t, :], v_ref[t - tail:t, :])

    band, current = _band_mask()
    for pi, d in enumerate(dils):
        shift = d.bit_length() - 1

        def block(b, carry, d=d, shift=shift, pi=pi):
            n = lax.shift_right_logical(b, jnp.int32(shift))
            q0 = n * (QBLK * d) + (b & (d - 1))
            k0 = q0 + pad - QBLK * d
            if d == 1:
                qsl = pl.ds(pl.multiple_of(q0, QBLK), QBLK)
                ksl = pl.ds(pl.multiple_of(k0, QBLK), 2 * QBLK)
            else:
                qsl = pl.ds(q0, QBLK, stride=d)
                ksl = pl.ds(k0, 2 * QBLK, stride=d)
            q = qs[qsl, :]
            pieces = [jnp.where(low, q, 0.0).astype(BF16),
                      jnp.where(low, 0.0, q).astype(BF16)]
            (m0, m1), num, den = _block_softmax(
                pieces, kp[ksl, :].astype(BF16), vp[ksl, :].astype(BF16),
                band & (current | (n > 0)), fuse_den=True)
            num_refs[pi][qsl, :] = jnp.where(low, num[0:QBLK, :], num[QBLK:2 * QBLK, :])
            max_refs[pi][qsl, :] = jnp.where(low, m0, m1)
            den_refs[pi][qsl, :] = jnp.where(low, den[0:QBLK, :], den[QBLK:2 * QBLK, :])
            return carry

        lax.fori_loop(0, t // QBLK, block, 0, unroll=BLOCK_UNROLL_DILATED)

    def finish(c, carry):
        rows = pl.ds(pl.multiple_of(c * PREP_ROWS, PREP_ROWS), PREP_ROWS)
        ms = [r[rows, :] for r in max_refs]
        ls = [r[rows, :] for r in den_refs]
        m = functools.reduce(jnp.maximum, ms)
        den = num = None
        for mi, li, r in zip(ms, ls, num_refs):
            w = jnp.exp(mi - m)
            den = w * li if den is None else den + w * li
            num = w * r[rows, :] if num is None else num + w * r[rows, :]
        o_ref[rows, :] = (num / den).astype(BF16)
        return carry

    lax.fori_loop(0, t // PREP_ROWS, finish, 0)


def _attention_dilated(u, tables, *, n_batch, t, dils, q_slab, k_slab, v_slab, n_slab, tail):
    pad = QBLK * max(dils)
    once = pl.Buffered(1)
    table = pl.BlockSpec((t, LANES), lambda b, s: (0, 0), pipeline_mode=once)
    tail_spec = pl.BlockSpec((None, LANES, tail), lambda b, s: (b, s, 0))
    scratch = [pltpu.VMEM((t, LANES), F32)]
    scratch += [pltpu.VMEM((pad + t, LANES), F32) for _ in range(2)]
    scratch += [pltpu.VMEM((t, LANES), F32) for _ in range(3 * len(dils))]
    return pl.pallas_call(
        functools.partial(_dilated_kernel, dils=dils, t=t, tail=tail),
        grid=(n_batch, n_slab),
        in_specs=[
            pl.BlockSpec((t, LANES), lambda b, s: (b, q_slab + s)),
            pl.BlockSpec((t, LANES), lambda b, s: (b, k_slab + s)),
            pl.BlockSpec((t, LANES), lambda b, s: (b, v_slab + s)),
            table, table, table,
        ],
        out_specs=[pl.BlockSpec((t, LANES), lambda b, s: (b, s)), tail_spec, tail_spec],
        out_shape=[
            jax.ShapeDtypeStruct((n_batch * t, n_slab * LANES), BF16),
            jax.ShapeDtypeStruct((n_batch, n_slab * LANES, tail), F32),
            jax.ShapeDtypeStruct((n_batch, n_slab * LANES, tail), F32),
        ],
        scratch_shapes=scratch,
        compiler_params=_cparams(2, 56),
        name="attn_dilated",
    )(u, u, u, *tables)


GQA_SLABS = 4


def _gqa_kernel(q_ref, k_ref, v_ref, cos_ref, slo_ref, shi_ref, sink_ref, o_ref, kt_ref, vt_ref,
                kp, vp, *, t, tail):
    pad = QBLK
    kv_head = pl.program_id(1)
    lane = lax.broadcasted_iota(jnp.int32, (1, LANES), 1)
    low = lane < HEAD_DIM
    keep = lax.shift_right_logical(lane, jnp.int32(6)) == kv_head

    kp[0:pad, :] = jnp.zeros((pad, LANES), BF16)
    vp[0:pad, :] = jnp.zeros((pad, LANES), BF16)

    def prep(c, carry):
        r0 = pl.multiple_of(c * PREP_ROWS, PREP_ROWS)
        rows = pl.ds(r0, PREP_ROWS)
        kr = _rope(k_ref[rows, :], cos_ref[rows, :], slo_ref[rows, :], shi_ref[rows, :])
        v = v_ref[rows, :]
        prow = pl.ds(pad + r0, PREP_ROWS)
        kp[prow, :] = jnp.where(keep, kr, pltpu.roll(kr, HEAD_DIM, 1)).astype(BF16)
        vp[prow, :] = jnp.where(keep, v, pltpu.roll(v, HEAD_DIM, 1)).astype(BF16)
        return carry

    lax.fori_loop(0, t // PREP_ROWS, prep, 0)
    trows = slice(t - tail, t)
    _write_tails(kt_ref, vt_ref,
                 _rope(k_ref[trows, :], cos_ref[trows, :], slo_ref[trows, :], shi_ref[trows, :]),
                 v_ref[trows, :])

    band, current = _band_mask()

    def block(b, carry):
        q0 = pl.multiple_of(b * QBLK, QBLK)
        rows = pl.ds(q0, QBLK)
        cos, slo, shi = cos_ref[rows, :], slo_ref[rows, :], shi_ref[rows, :]
        pieces = []
        for s in range(GQA_SLABS):
            q = _rope(q_ref[rows, s * LANES:(s + 1) * LANES], cos, slo, shi) * ATTN_SCALE
            pieces += [jnp.where(low, q, 0.0).astype(BF16), jnp.where(low, 0.0, q).astype(BF16)]
        ksl = pl.ds(q0, 2 * QBLK)
        maxima, num, den = _block_softmax(pieces, kp[ksl, :], vp[ksl, :],
                                          band & (current | (b > 0)), fuse_den=False)
        for s in range(GQA_SLABS):
            outs = []
            for hh in range(2):
                i = 2 * s + hh
                piece = slice(i * QBLK, (i + 1) * QBLK)
                sink = sink_ref[:, s * LANES + hh * HEAD_DIM:s * LANES + hh * HEAD_DIM + 1]
                outs.append(num[piece, :] / (den[piece, :] + jnp.exp(sink - maxima[i])))
            o_ref[rows, s * LANES:(s + 1) * LANES] = jnp.where(low, outs[0], outs[1]).astype(BF16)
        return carry

    lax.fori_loop(0, t // QBLK, block, 0, unroll=BLOCK_UNROLL_GQA)


def _attention_gqa(u, tables, sinks, *, n_batch, t, q_slab, k_slab, v_slab, tail):
    width = GQA_SLABS * LANES
    once = pl.Buffered(1)
    table = pl.BlockSpec((t, LANES), lambda b, g: (0, 0), pipeline_mode=once)
    tail_spec = pl.BlockSpec((None, LANES, tail), lambda b, g: (b, 0, 0))
    q_blk = q_slab // GQA_SLABS
    return pl.pallas_call(
        functools.partial(_gqa_kernel, t=t, tail=tail),
        grid=(n_batch, N_KV_B),
        in_specs=[
            pl.BlockSpec((t, width), lambda b, g: (b, q_blk + g)),
            pl.BlockSpec((t, LANES), lambda b, g: (b, k_slab)),
            pl.BlockSpec((t, LANES), lambda b, g: (b, v_slab)),
            table, table, table,
            pl.BlockSpec((None, 1, width), lambda b, g: (g, 0, 0)),
        ],
        out_specs=[pl.BlockSpec((t, width), lambda b, g: (b, g)), tail_spec, tail_spec],
        out_shape=[
            jax.ShapeDtypeStruct((n_batch * t, N_KV_B * width), BF16),
            jax.ShapeDtypeStruct((n_batch, LANES, tail), F32),
            jax.ShapeDtypeStruct((n_batch, LANES, tail), F32),
        ],
        scratch_shapes=[pltpu.VMEM((QBLK + t, LANES), BF16), pltpu.VMEM((QBLK + t, LANES), BF16)],
        compiler_params=_cparams(2, 48),
        name="attn_gqa",
    )(u, u, u, *tables, sinks)


CONV_TB = 256
CONV_HALO = 32
CONV_RC = 32


def _layer_norm_swish(y, g, b):
    mu = jnp.mean(y, axis=-1, keepdims=True)
    yc = y - mu
    var = jnp.mean(yc * yc, axis=-1, keepdims=True)
    z = yc * lax.rsqrt(var + LN_EPS) * g + b
    return z * _sigmoid(z)


def _conv_kernel(v0, v1, g0, g1, cw_ref, cb_ref, lg_ref, lb_ref, o_ref, st_ref, ext, shifted):
    tt = pl.program_id(1)
    half = D_CONV // 2

    @pl.when(tt == 0)
    def _():
        ext[0:CONV_HALO, :] = jnp.zeros((CONV_HALO, D_CONV), F32)

    @pl.when(tt > 0)
    def _():
        ext[0:CONV_HALO, :] = ext[CONV_TB:CONV_TB + CONV_HALO, :]

    ext[CONV_HALO:CONV_HALO + CONV_TB, 0:half] = v0[...] * _sigmoid(g0[...])
    ext[CONV_HALO:CONV_HALO + CONV_TB, half:D_CONV] = v1[...] * _sigmoid(g1[...])
    st_ref[...] = ext[CONV_TB:CONV_TB + CONV_HALO, :]

    span = CONV_TB + CONV_HALO - SUBLANES
    for s in range(1, SUBLANES):
        shifted[s - 1, 0:span, :] = ext[s:s + span, :]

    lead = CONV_HALO - (CONV_WIDTH - 1)
    for c in range(CONV_TB // CONV_RC):
        r0 = c * CONV_RC
        y = jnp.zeros((CONV_RC, D_CONV), F32) + cb_ref[...]
        for w in range(CONV_WIDTH):
            s = (lead + w) % SUBLANES
            base = r0 + lead + w - s
            win = ext[base:base + CONV_RC, :] if s == 0 else shifted[s - 1, base:base + CONV_RC, :]
            y = y + win * cw_ref[w:w + 1, :]
        o_ref[r0:r0 + CONV_RC, :] = _layer_norm_swish(y, lg_ref[...], lb_ref[...]).astype(BF16)


def _conv_module(u, cw, cb, lg, lb, *, n_batch, t):
    nt = t // CONV_TB
    half = D_CONV // 2
    v_blk, g_blk = COL_VAL // half, COL_GATE // half
    row = lambda b, i: b * nt + i
    vec = pl.BlockSpec((1, D_CONV), lambda b, i: (0, 0))
    return pl.pallas_call(
        _conv_kernel,
        grid=(n_batch, nt),
        in_specs=[
            pl.BlockSpec((CONV_TB, half), lambda b, i: (row(b, i), v_blk)),
            pl.BlockSpec((CONV_TB, half), lambda b, i: (row(b, i), v_blk + 1)),
            pl.BlockSpec((CONV_TB, half), lambda b, i: (row(b, i), g_blk)),
            pl.BlockSpec((CONV_TB, half), lambda b, i: (row(b, i), g_blk + 1)),
            pl.BlockSpec((CONV_HALO, D_CONV), lambda b, i: (0, 0)),
            vec, vec, vec,
        ],
        out_specs=[
            pl.BlockSpec((CONV_TB, D_CONV), lambda b, i: (row(b, i), 0)),
            pl.BlockSpec((CONV_HALO, D_CONV), lambda b, i: (b, 0)),
        ],
        out_shape=[
            jax.ShapeDtypeStruct((n_batch * t, D_CONV), BF16),
            jax.ShapeDtypeStruct((n_batch * CONV_HALO, D_CONV), F32),
        ],
        scratch_shapes=[pltpu.VMEM((CONV_HALO + CONV_TB, D_CONV), F32),
                        pltpu.VMEM((SUBLANES - 1, CONV_HALO + CONV_TB, D_CONV), F32)],
        compiler_params=_cparams(2, 32),
        name="conv_module",
    )(u, u, u, u, cw, cb, lg, lb)


ROPE_SLABS = tuple(range(SLAB_QA, SLAB_VA)) + tuple(range(SLAB_QB, SLAB_VB))
Q_SLABS = tuple(range(SLAB_QA, SLAB_KA)) + tuple(range(SLAB_QB, SLAB_KB))
VAL_SLABS = tuple(range(COL_VAL // LANES, COL_GATE // LANES))


def _sample_post_kernel(u_ref, cos_ref, slo_ref, shi_ref, o_ref):
    for s in range(N_IN // LANES):
        x = u_ref[:, s * LANES:(s + 1) * LANES]
        if s in ROPE_SLABS:
            x = _rope(x, cos_ref[...], slo_ref[...], shi_ref[...])
            if s in Q_SLABS:
                x = x * ATTN_SCALE
        elif s in VAL_SLABS:
            gs = s + D_CONV // LANES
            x = x * _sigmoid(u_ref[:, gs * LANES:(gs + 1) * LANES])
        o_ref[:, s * LANES:(s + 1) * LANES] = x


def _sample_post(u, tables):
    m, n = u.shape
    full = pl.BlockSpec((m, n), lambda i: (0, 0))
    vec = pl.BlockSpec((1, LANES), lambda i: (0, 0))
    return pl.pallas_call(
        _sample_post_kernel,
        grid=(1,),
        in_specs=[full, vec, vec, vec],
        out_specs=full,
        out_shape=jax.ShapeDtypeStruct((m, n), F32),
        compiler_params=_cparams(1, 16),
        name="sample_post",
    )(u, *tables)


def _pattern_multiplicity(la):
    dist = la - jnp.arange(la, dtype=jnp.int32)
    mult = jnp.zeros((la,), F32)
    for d in DILATIONS_A:
        mult = mult + ((dist % d == 0) & (dist <= HOPS * d)).astype(F32)
    return mult.reshape(1, la)


def _sattn_shift_kernel(kt_ref, vt_ref, q_ref, knr_ref, vnr_ref, knc_ref, vnc_ref, mult_ref,
                        *rest):
    o_ref, ko_ref, vo_ref = rest[-3:]
    nh, e, la = kt_ref.shape
    w = nh * e
    k = kt_ref[...].reshape(w, la)
    v = vt_ref[...].reshape(w, la)
    q = q_ref[...]
    mult = mult_ref[...]
    s = jnp.dot(q.astype(BF16), k.astype(BF16), preferred_element_type=F32)
    s = jnp.where(mult > 0.0, s, NEG_INF)
    s_new = jnp.sum(q * knr_ref[...], axis=1, keepdims=True)
    m = jnp.maximum(jnp.max(s, axis=1, keepdims=True), s_new)
    p = jnp.exp(s - m) * mult
    p_new = jnp.exp(s_new - m) * float(len(DILATIONS_A))
    den = jnp.sum(p, axis=1, keepdims=True) + p_new
    num = lax.dot_general(p.astype(BF16), v.astype(BF16), NT_DIMS, preferred_element_type=F32)
    o_ref[...] = (num + p_new * vnr_ref[...]) / den
    newest = lax.broadcasted_iota(jnp.int32, (1, la), 1) == la - 1
    ko_ref[...] = jnp.where(newest, knc_ref[...], pltpu.roll(k, la - 1, 1)).reshape(nh, e, la)
    vo_ref[...] = jnp.where(newest, vnc_ref[...], pltpu.roll(v, la - 1, 1)).reshape(nh, e, la)


def _sample_attn_shift(kt, vt, prev, layer, q_bd, kn, vn, mult):
    depth, nb, nh, e, la = kt.shape
    w = nh * e
    slab = pl.BlockSpec((None, None, nh, e, la), lambda b: (layer, b, 0, 0, 0))
    tokq = pl.BlockSpec((None, nh, w), lambda b: (b, 0, 0))
    row = pl.BlockSpec((None, 1, w), lambda b: (b, 0, 0))
    col = pl.BlockSpec((None, w, 1), lambda b: (b, 0, 0))
    in_specs = [slab, slab, tokq, row, row, col, col, pl.BlockSpec((1, la), lambda b: (0, 0))]
    args = [kt, vt, q_bd, kn.reshape(nb, 1, w), vn.reshape(nb, 1, w),
            kn.reshape(nb, w, 1), vn.reshape(nb, w, 1), mult]
    aliases = {}
    if prev is not None:
        aliases = {len(args): 1, len(args) + 1: 2}
        in_specs += [pl.BlockSpec(memory_space=pl.ANY)] * 2
        args += list(prev)
    return pl.pallas_call(
        _sattn_shift_kernel,
        grid=(nb,),
        in_specs=in_specs,
        out_specs=[tokq, slab, slab],
        out_shape=[jax.ShapeDtypeStruct((nb, nh, w), F32),
                   jax.ShapeDtypeStruct(kt.shape, F32), jax.ShapeDtypeStruct(vt.shape, F32)],
        input_output_aliases=aliases,
        compiler_params=_cparams(1, 56),
        name="sample_attn_shift",
    )(*args)


def _sattn_b_kernel(kc_ref, vc_ref, q_ref, kn_ref, vn_ref, sink_ref, o_ref):
    q = q_ref[...]
    s = lax.dot_general(q.astype(BF16), kc_ref[...].astype(BF16), NT_DIMS,
                        preferred_element_type=F32)
    s_new = jnp.sum(q * kn_ref[...], axis=1, keepdims=True)
    m = jnp.maximum(jnp.max(s, axis=1, keepdims=True), s_new)
    p = jnp.exp(s - m)
    p_new = jnp.exp(s_new - m)
    den = jnp.sum(p, axis=1, keepdims=True) + p_new + jnp.exp(sink_ref[...] - m)
    num = jnp.dot(p.astype(BF16), vc_ref[...].astype(BF16), preferred_element_type=F32)
    o_ref[...] = (num + p_new * vn_ref[...]) / den


def _sample_attn_b(kc, vc, layer, q_exp, kn, vn, sinks):
    _, nb, lb, w = kc.shape
    nh = q_exp.shape[1]
    cache = pl.BlockSpec((None, None, lb, w), lambda b: (layer, b, 0, 0))
    tokq = pl.BlockSpec((None, nh, w), lambda b: (b, 0, 0))
    tok1 = pl.BlockSpec((None, 1, w), lambda b: (b, 0, 0))
    return pl.pallas_call(
        _sattn_b_kernel,
        grid=(nb,),
        in_specs=[cache, cache, tokq, tok1, tok1, pl.BlockSpec((nh, 1), lambda b: (0, 0))],
        out_specs=tokq,
        out_shape=jax.ShapeDtypeStruct((nb, nh, w), F32),
        compiler_params=_cparams(1, 16),
        name="sample_attn_shared",
    )(kc, vc, q_exp, kn, vn, sinks)


def _sconv_kernel(st_ref, glu_ref, cw_ref, cb_ref, lg_ref, lb_ref, o_ref, ns_ref):
    hist = CONV_WIDTH - 1
    glu = glu_ref[...]
    y = glu * cw_ref[hist:hist + 1, :] + cb_ref[...]
    for w in range(hist):
        y = y + st_ref[:, w * D_CONV:(w + 1) * D_CONV] * cw_ref[w:w + 1, :]
    o_ref[...] = _layer_norm_swish(y, lg_ref[...], lb_ref[...]).astype(BF16)
    ns_ref[:, 0:(hist - 1) * D_CONV] = st_ref[:, D_CONV:hist * D_CONV]
    ns_ref[:, (hist - 1) * D_CONV:hist * D_CONV] = glu


def _sample_conv(state2d, layer, glu, cw, cb, lg, lb):
    _, nb, width = state2d.shape
    st = pl.BlockSpec((None, nb, width), lambda i: (layer, 0, 0))
    tok = pl.BlockSpec((nb, D_CONV), lambda i: (0, 0))
    vec = pl.BlockSpec((1, D_CONV), lambda i: (0, 0))
    return pl.pallas_call(
        _sconv_kernel,
        grid=(1,),
        in_specs=[st, tok, pl.BlockSpec((CONV_HALO, D_CONV), lambda i: (0, 0)), vec, vec, vec],
        out_specs=[tok, pl.BlockSpec((nb, width), lambda i: (0, 0))],
        out_shape=[jax.ShapeDtypeStruct((nb, D_CONV), BF16),
                   jax.ShapeDtypeStruct((nb, width), F32)],
        compiler_params=_cparams(1, 16),
        name="sample_conv",
    )(state2d, glu, cw, cb, lg, lb)


def _shift_kernel(km, kx, kn, vm, vx, vn, ko, vo):
    last = pl.program_id(2) == pl.num_programs(2) - 1
    for main, nxt, new, out in ((km, kx, kn, ko), (vm, vx, vn, vo)):
        rb = main.shape[1]
        out[:, 0:rb - 1] = main[:, 1:rb]
        out[:, rb - 1:rb] = jnp.where(last, new[...], nxt[...])


def _shift_caches(cache_k, cache_v, new_k, new_v, *, batch_chunk, row_block):
    depth, nb, rows, nh, e = cache_k.shape
    main = pl.BlockSpec((None, batch_chunk, row_block, nh, e), lambda l, b, r: (l, b, r, 0, 0))
    nxt = pl.BlockSpec((None, batch_chunk, 1, nh, e),
                       lambda l, b, r: (l, b, jnp.minimum((r + 1) * row_block, rows - 1), 0, 0))
    new = pl.BlockSpec((None, batch_chunk, 1, nh, e), lambda l, b, r: (l, b, 0, 0, 0))
    shape = jax.ShapeDtypeStruct(cache_k.shape, cache_k.dtype)
    return pl.pallas_call(
        _shift_kernel,
        grid=(depth, nb // batch_chunk, rows // row_block),
        in_specs=[main, nxt, new, main, nxt, new],
        out_specs=[main, main],
        out_shape=[shape, shape],
        compiler_params=_cparams(3, 48),
        name="cache_shift",
    )(cache_k, cache_k, new_k, cache_v, cache_v, new_v)


def _rope_tables(pos):
    n = pos.shape[0]
    inv_freq = 1.0 / (ROPE_THETA ** (jnp.arange(ROT_HALF, dtype=F32) / ROT_HALF))
    ang = pos.astype(F32)[:, None] * inv_freq[None, :]
    cos, sin = jnp.cos(ang), jnp.sin(ang)
    rest = HEAD_DIM - 2 * ROT_HALF
    c = jnp.concatenate([cos, cos, jnp.ones((n, rest), F32)], axis=1)
    lo = jnp.concatenate([-sin, jnp.zeros((n, HEAD_DIM - ROT_HALF), F32)], axis=1)
    hi = jnp.concatenate([jnp.zeros((n, ROT_HALF), F32), sin, jnp.zeros((n, rest), F32)], axis=1)
    return tuple(jnp.tile(x, (1, LANES // HEAD_DIM)) for x in (c, lo, hi))


def kernel(x_prompt, x_sample, c_prompt, c_sample, cache_a_k, cache_a_v, cache_b_k, cache_b_v,
           state_c_conv, w_ada, b_ada, g_pre, g_post, w_ffn_gu, w_ffn_down, w_in, w_out,
           attn_sinks, conv_w, conv_b, conv_ln_g, conv_ln_b):
    nbp, t, d = x_prompt.shape
    nbs = x_sample.shape[0]
    mp = nbp * t
    depth = w_ada.shape[0]
    la, lb = cache_a_k.shape[2], cache_b_k.shape[2]
    hist = CONV_WIDTH - 1
    tm_p = 1024

    xp = x_prompt.reshape(mp, d)
    xs = x_sample.reshape(nbs, d)

    s_row = 32
    c_all = jnp.zeros((s_row + nbs, d), F32).at[:nbp].set(c_prompt).at[s_row:].set(c_sample)
    mod_all = _ada_mod(c_all, w_ada, b_ada)

    def mod_p(l, s):
        return mod_all[l, :nbp, 3 * s * d:3 * (s + 1) * d].reshape(nbp, 3, 1, d)

    def mod_s(l, s):
        m = mod_all[l, s_row:, 3 * s * d:3 * (s + 1) * d].reshape(nbs, 3, d)
        return jnp.transpose(m, (1, 0, 2))[None]

    def nxt(l, s):
        return (l, s + 1) if s + 1 < N_SUB else (l + 1, 0)

    tab_p = _rope_tables(jnp.arange(t, dtype=jnp.int32))
    tab_s = _rope_tables(PAST_LEN + jnp.arange(1, dtype=jnp.int32))
    sink_groups = jnp.repeat(attn_sinks, HEAD_DIM, axis=1).reshape(depth, N_KV_B, 1, -1)
    sink_cols = attn_sinks.reshape(depth, N_HEADS_B, 1)
    cw_pad = jnp.pad(conv_w, ((0, 0), (0, CONV_HALO - CONV_WIDTH), (0, 0)))
    vec = lambda a, l: a[l].reshape(1, -1)

    cache_bk2 = cache_b_k.reshape(depth, nbs, lb, N_KV_B * HEAD_DIM)
    cache_bv2 = cache_b_v.reshape(depth, nbs, lb, N_KV_B * HEAD_DIM)
    state2d = state_c_conv.reshape(depth, nbs, hist * D_CONV)
    kv_of_head = (jnp.arange(N_HEADS_B) // (N_HEADS_B // N_KV_B))[:, None] == jnp.arange(N_KV_B)
    cache_akt = jnp.transpose(cache_a_k, (0, 1, 3, 4, 2))
    cache_avt = jnp.transpose(cache_a_v, (0, 1, 3, 4, 2))
    mult_a = _pattern_multiplicity(la)
    head_eye = jnp.eye(N_HEADS_A, dtype=bool)[None, :, :, None]

    hp = _prenorm(xp, mod_p(0, 0), vec(g_pre[0], 0), tm=tm_p, rows_per_batch=t)
    hs = _prenorm(xs, mod_s(0, 0), vec(g_pre[0], 0), tm=nbs, rows_per_batch=nbs)

    st_p = [[] for _ in range(5)]
    new_rows = [[], []]
    shift_a = None
    st_c_s = []

    f_pad = pl.cdiv(w_ffn_down.shape[2], TF) * TF
    wd_bf = _cast_pad_rows(w_ffn_down.reshape(depth * 2, -1, d), f_pad)
    wo_bf = _cast_pad_rows(w_out, w_out.shape[1])

    def resid(lhs, s, x, l, weight, mod_fn, tm, rpb):
        ln, sn = nxt(l, s)
        last = ln >= depth
        if s == 1:
            lhs, w, mat, tk = jnp.concatenate(lhs, axis=1), wo_bf, l, TF
        else:
            w, mat, tk = wd_bf, 2 * l + s // 2, f_pad // 4
        return _resid_proj(
            lhs, w, mat, x, mod_fn(l, s), vec(g_post[l], s),
            None if last else mod_fn(ln, sn), None if last else vec(g_pre[ln], sn),
            tm=tm, tk=tk, rows_per_batch=rpb, weight=weight)

    for l in range(depth):
        a, a_s = _ffn_up(hp, hs, w_ffn_gu, (l, 0), tm=tm_p)
        xp, hp = resid(a, 0, xp, l, 0.5, mod_p, tm_p, t)
        xs, hs = resid(a_s, 0, xs, l, 0.5, mod_s, nbs, nbs)

        u, u_s = _in_proj(hp, hs, w_in, l, tm=tm_p)
        oa, kta, vta = _attention_dilated(
            u, tab_p, n_batch=nbp, t=t, dils=DILATIONS_A, q_slab=SLAB_QA, k_slab=SLAB_KA,
            v_slab=SLAB_VA, n_slab=N_HEADS_A // 2, tail=min(la, t))
        ob, ktb, vtb = _attention_gqa(
            u, tab_p, sink_groups[l], n_batch=nbp, t=t, q_slab=SLAB_QB, k_slab=SLAB_KB,
            v_slab=SLAB_VB, tail=min(lb, t))
        oc, cst = _conv_module(u, cw_pad[l], vec(conv_b, l), vec(conv_ln_g, l), vec(conv_ln_b, l),
                               n_batch=nbp, t=t)
        xp, hp = resid([oa, ob, oc], 1, xp, l, 1.0, mod_p, tm_p, t)

        for dst, tail_t, heads in ((st_p[0], kta, N_HEADS_A), (st_p[1], vta, N_HEADS_A),
                                   (st_p[2], ktb, N_KV_B), (st_p[3], vtb, N_KV_B)):
            dst.append(jnp.transpose(tail_t.reshape(nbp, heads, HEAD_DIM, -1), (0, 3, 1, 2)))
        st_p[4].append(cst.reshape(nbp, CONV_HALO, D_CONV)[:, CONV_HALO - hist:])

        ur = _sample_post(u_s, tab_s)
        seg = lambda lo, hi: ur[:, lo * LANES:hi * LANES]
        qa = seg(SLAB_QA, SLAB_KA).reshape(nbs, 1, N_HEADS_A, HEAD_DIM)
        q_bd = jnp.where(head_eye, qa, 0.0).reshape(nbs, N_HEADS_A, N_HEADS_A * HEAD_DIM)
        oa3, *shift_a = _sample_attn_shift(cache_akt, cache_avt, shift_a, l, q_bd,
                                           seg(SLAB_KA, SLAB_VA), seg(SLAB_VA, SLAB_QB), mult_a)
        oa = jnp.sum(jnp.where(head_eye, oa3.reshape(nbs, N_HEADS_A, N_HEADS_A, HEAD_DIM), 0.0),
                     axis=1)
        qb = seg(SLAB_QB, SLAB_KB).reshape(nbs, N_HEADS_B, 1, HEAD_DIM)
        q_exp = jnp.where(kv_of_head[None, :, :, None], qb, 0.0).reshape(nbs, N_HEADS_B, LANES)
        knb = seg(SLAB_KB, SLAB_VB)
        vnb = seg(SLAB_VB, SLAB_VB + 1)
        ob2 = _sample_attn_b(cache_bk2, cache_bv2, l, q_exp, knb.reshape(nbs, 1, LANES),
                             vnb.reshape(nbs, 1, LANES), sink_cols[l])
        ob = jnp.sum(jnp.where(kv_of_head[None, :, :, None],
                               ob2.reshape(nbs, N_HEADS_B, N_KV_B, HEAD_DIM), 0.0), axis=2)
        glu = ur[:, COL_VAL:COL_GATE]
        oc, ns = _sample_conv(state2d, l, glu, cw_pad[l], vec(conv_b, l), vec(conv_ln_g, l),
                              vec(conv_ln_b, l))
        mix = [oa.reshape(nbs, -1).astype(BF16), ob.reshape(nbs, -1).astype(BF16), oc]
        xs, hs = resid(mix, 1, xs, l, 1.0, mod_s, nbs, nbs)

        a, a_s = _ffn_up(hp, hs, w_ffn_gu, (l, 1), tm=tm_p)
        xp, hp = resid(a, 2, xp, l, 0.5, mod_p, tm_p, t)
        xs, hs = resid(a_s, 2, xs, l, 0.5, mod_s, nbs, nbs)

        new_rows[0].append(knb.reshape(nbs, 1, N_KV_B, HEAD_DIM))
        new_rows[1].append(vnb.reshape(nbs, 1, N_KV_B, HEAD_DIM))
        st_c_s.append(ns.reshape(nbs, hist, D_CONV))

    new_rows = [jnp.stack(r) for r in new_rows]
    shift_a = [jnp.transpose(c, (0, 1, 4, 2, 3)) for c in shift_a]
    shift_b = _shift_caches(cache_b_k, cache_b_v, new_rows[0], new_rows[1],
                            batch_chunk=8, row_block=lb)
    return (xp.reshape(nbp, t, d), xs.reshape(nbs, 1, d),
            *[jnp.stack(s) for s in st_p], *shift_a, *shift_b, jnp.stack(st_c_s))
```
